```python
import math
import jax
import jax.numpy as jnp
from jax import lax
import numpy as np

D_MODEL = 1024
BATCH = 8
SEQ = 4096
DEPTH = 1

CTX_LEN = 256
GRID_W = 64

DN_HEADS = 4
DN_HEAD_DIM = 128
DN_WIDTH = DN_HEADS * DN_HEAD_DIM
DN_CHUNK = 64
CONV_K = 5

S5_WIDTH = D_MODEL - DN_WIDTH
S5_GROUP = 16
S5_GROUPS = S5_WIDTH // S5_GROUP
S5_STATE = 64
S5_DT_MIN = 1e-3
S5_DT_MAX = 1e-1

N_EXPERT_GROUPS = 4
EXPERTS_PER_GROUP = 8
N_EXPERTS = N_EXPERT_GROUPS * EXPERTS_PER_GROUP
TOP_K_IN_GROUP = 2
EXPERT_FF = 512

NORM_EPS = 1e-6
L2_EPS = 1e-6
N_MOD = 6

IN_SIZES = (DN_WIDTH, DN_WIDTH, DN_WIDTH, DN_WIDTH, 2 * DN_HEADS, 2 * DN_HEADS, S5_WIDTH, D_MODEL, D_MODEL)
IN_COLS = 4 * DN_WIDTH + 4 * DN_HEADS + S5_WIDTH + 2 * D_MODEL

kernel_name = 'hybrid_gdn_s5_hmoe_diffusion_block'


def rms_norm(x, w):
    xf = x.astype(jnp.float32)
    y = xf * lax.rsqrt(jnp.mean(xf * xf, axis=-1, keepdims=True) + NORM_EPS)
    return (y * w.astype(jnp.float32)).astype(x.dtype)


def l2_normalize(x):
    xf = x.astype(jnp.float32)
    return (xf * lax.rsqrt(jnp.sum(xf * xf, axis=-1, keepdims=True) + L2_EPS)).astype(x.dtype)


def split_projection(p):
    offs = np.cumsum(np.array(IN_SIZES))[:-1]
    return jnp.split(p, [int(o) for o in offs], axis=-1)


def centred_depthwise_conv(x, w):
    ch = x.shape[-1]
    return lax.conv_general_dilated(
        x, w[:, None, :].astype(x.dtype), window_strides=(1,),
        padding=[(CONV_K // 2, CONV_K // 2)],
        dimension_numbers=('NWC', 'WIO', 'NWC'), feature_group_count=ch)


def conv_context(x, w):
    return centred_depthwise_conv(x, w)


def conv_latent_rows(x, w):
    b, n, ch = x.shape
    rows = n // GRID_W
    y = centred_depthwise_conv(x.reshape(b * rows, GRID_W, ch), w)
    return y.reshape(b, n, ch)


def gated_delta_chunked(q, k, v, g, beta, s0):
    f32 = jnp.float32
    bsz, n, h, dk = q.shape
    dv = v.shape[-1]
    c = DN_CHUNK
    nc = n // c

    def blocks(t):
        t = t.astype(f32).reshape((bsz, nc, c, h) + t.shape[3:])
        return jnp.moveaxis(t, 3, 1)

    qb = blocks(q) * (dk ** -0.5)
    kb_, vb = blocks(k), blocks(v)
    gb, bb = blocks(g), blocks(beta)
    G = jnp.cumsum(gb, axis=-1)
    idx = jnp.arange(c)
    incl = idx[:, None] >= idx[None, :]
    strict = idx[:, None] > idx[None, :]
    diff = G[..., :, None] - G[..., None, :]
    dec_incl = jnp.where(incl, jnp.exp(jnp.where(incl, diff, 0.0)), 0.0)
    dec_strict = jnp.where(strict, dec_incl, 0.0)

    k_beta = kb_ * bb[..., None]
    a_mat = jnp.einsum('bhnid,bhnjd->bhnij', k_beta, kb_) * dec_strict
    t_mat = a_mat + jnp.eye(c, dtype=f32)
    rhs = jnp.concatenate([vb * bb[..., None], k_beta * jnp.exp(G)[..., None]], axis=-1)
    sol = lax.linalg.triangular_solve(t_mat, rhs, left_side=True, lower=True, unit_diagonal=True)
    u, w = sol[..., :dv], sol[..., dv:]

    attn = jnp.einsum('bhnid,bhnjd->bhnij', qb, kb_) * dec_incl
    q_dec = qb * jnp.exp(G)[..., None]
    g_last = G[..., -1]
    k_tail = kb_ * jnp.exp(g_last[..., None] - G)[..., None]

    def step(s, xs):
        w_n, u_n, q_n, a_n, k_n, gl_n = xs
        v_new = u_n - jnp.einsum('bhcd,bhde->bhce', w_n, s)
        o_n = jnp.einsum('bhcd,bhde->bhce', q_n, s) + jnp.einsum('bhij,bhje->bhie', a_n, v_new)
        s = s * jnp.exp(gl_n)[..., None, None] + jnp.einsum('bhcd,bhce->bhde', k_n, v_new)
        return s, o_n

    xs = tuple(jnp.moveaxis(t, 2, 0) for t in (w, u, q_dec, attn, k_tail, g_last))
    s_fin, o = lax.scan(step, s0.astype(f32), xs)
    o = jnp.moveaxis(jnp.moveaxis(o, 0, 2), 1, 3).reshape(bsz, n, h, dv)
    return o, s_fin


def deltanet_bidirectional(q, k, v, a, bt, a_log, dt_bias, s0_f, s0_b):
    f32 = jnp.float32
    g = -jnp.exp(a_log.astype(f32)) * jax.nn.softplus(a.astype(f32) + dt_bias.astype(f32))
    beta = jax.nn.sigmoid(bt.astype(f32))
    o_f, s_f = gated_delta_chunked(q, k, v, g[:, :, 0], beta[:, :, 0], s0_f)
    rev = lambda t: jnp.flip(t, axis=1)
    o_b, s_b = gated_delta_chunked(rev(q), rev(k), rev(v), rev(g[:, :, 1]), rev(beta[:, :, 1]), s0_b)
    return o_f + rev(o_b), s_f, s_b


def s5_discretise(lam_re, lam_im, log_step, b_re, b_im):
    f32 = jnp.float32
    lam_re, lam_im = lam_re.astype(f32), lam_im.astype(f32)
    b_re, b_im = b_re.astype(f32), b_im.astype(f32)
    dt = jnp.exp(log_step.astype(f32))[:, None]
    mag = jnp.exp(lam_re * dt)
    ang = lam_im * dt
    bar_re = mag * jnp.cos(ang)
    bar_im = mag * jnp.sin(ang)
    den = lam_re * lam_re + lam_im * lam_im
    coef_re = ((bar_re - 1.0) * lam_re + bar_im * lam_im) / den
    coef_im = (bar_im * lam_re - (bar_re - 1.0) * lam_im) / den
    bb_re = coef_re[..., None] * b_re - coef_im[..., None] * b_im
    bb_im = coef_re[..., None] * b_im + coef_im[..., None] * b_re
    return bar_re, bar_im, bb_re, bb_im


def complex_linear_combine(e1, e2):
    a1r, a1i, b1r, b1i = e1
    a2r, a2i, b2r, b2i = e2
    return (a2r * a1r - a2i * a1i,
            a2r * a1i + a2i * a1r,
            a2r * b1r - a2i * b1i + b2r,
            a2r * b1i + a2i * b1r + b2i)


def s5_direction(u, lam_re, lam_im, log_step, b_re, b_im, c_re, c_im, x0_re, x0_im):
    bar_re, bar_im, bb_re, bb_im = s5_discretise(lam_re, lam_im, log_step, b_re, b_im)
    bu_re = jnp.einsum('blgc,gpc->lbgp', u, bb_re)
    bu_im = jnp.einsum('blgc,gpc->lbgp', u, bb_im)
    bu_re = bu_re.at[0].add(bar_re * x0_re - bar_im * x0_im)
    bu_im = bu_im.at[0].add(bar_re * x0_im + bar_im * x0_re)
    n = u.shape[1]
    a_re = jnp.broadcast_to(bar_re, (n, 1) + bar_re.shape)
    a_im = jnp.broadcast_to(bar_im, (n, 1) + bar_im.shape)
    _, _, xr, xi = lax.associative_scan(complex_linear_combine, (a_re, a_im, bu_re, bu_im), axis=0)
    y = (jnp.einsum('lbgp,gcp->blgc', xr, c_re.astype(jnp.float32))
         - jnp.einsum('lbgp,gcp->blgc', xi, c_im.astype(jnp.float32)))
    return y, xr[-1], xi[-1]


def mixer_stream(p, conv_fn, states, lp):
    f32 = jnp.float32
    bsz, n, _ = p.shape
    _, _, _, _, a, bt, u, _, _ = split_projection(p)
    qkv = jax.nn.silu(conv_fn(p[..., :3 * DN_WIDTH], lp['conv_w']))
    q, k, v = jnp.split(qkv, 3, axis=-1)
    q = l2_normalize(q.reshape(bsz, n, DN_HEADS, DN_HEAD_DIM))
    k = l2_normalize(k.reshape(bsz, n, DN_HEADS, DN_HEAD_DIM))
    v = v.reshape(bsz, n, DN_HEADS, DN_HEAD_DIM)
    a = a.reshape(bsz, n, 2, DN_HEADS)
    bt = bt.reshape(bsz, n, 2, DN_HEADS)
    s0_f, s0_b, xf0_re, xf0_im, xb0_re, xb0_im = states
    o_a, s_f, s_b = deltanet_bidirectional(q, k, v, a, bt, lp['a_log'], lp['dt_bias'], s0_f, s0_b)

    u_g = u.astype(f32).reshape(bsz, n, S5_GROUPS, S5_GROUP)
    y_f, xf_re, xf_im = s5_direction(u_g, lp['s5_lam_re'][0], lp['s5_lam_im'][0], lp['s5_log_step'][0],
                                     lp['s5_b_re'][0], lp['s5_b_im'][0], lp['s5_c_re'][0], lp['s5_c_im'][0],
                                     xf0_re, xf0_im)
    y_b, xb_re, xb_im = s5_direction(jnp.flip(u_g, axis=1), lp['s5_lam_re'][1], lp['s5_lam_im'][1],
                                     lp['s5_log_step'][1], lp['s5_b_re'][1], lp['s5_b_im'][1],
                                     lp['s5_c_re'][1], lp['s5_c_im'][1], xb0_re, xb0_im)
    d_skip = lp['s5_d'].astype(f32).reshape(S5_GROUPS, S5_GROUP)
    y_b5 = y_f + jnp.flip(y_b, axis=1) + d_skip * u_g
    y_b5 = y_b5.reshape(bsz, n, S5_WIDTH).astype(p.dtype)
    return o_a.astype(p.dtype), y_b5, (s_f, s_b, xf_re, xf_im, xb_re, xb_im)


def mixer_merge(p, o_a, y_b5, lp):
    bsz, n, _ = p.shape
    _, _, _, z, _, _, _, gate_a, gate_b = split_projection(p)
    z = z.reshape(bsz, n, DN_HEADS, DN_HEAD_DIM)
    o_a = rms_norm(o_a, lp['o_norm_w']) * jax.nn.silu(z)
    y_a = o_a.reshape(bsz, n, DN_WIDTH) @ lp['w_a_up']
    glu_in, glu_gate = jnp.split(jax.nn.gelu(y_b5) @ lp['w_glu'], 2, axis=-1)
    y_b = glu_in * jax.nn.sigmoid(glu_gate)
    y = jax.nn.sigmoid(gate_a) * y_a + jax.nn.sigmoid(gate_b) * y_b
    return y @ lp['w_out']


def hier_moe(h, lp):
    shp = h.shape
    t = h.reshape(-1, D_MODEL)
    n_tok = t.shape[0]
    g_logit = (t @ lp['w_router_group']).astype(jnp.float32) + lp['b_router_group'].astype(jnp.float32)
    g_prob = jax.nn.softmax(g_logit, axis=-1)
    g_sel = jnp.argmax(g_logit, axis=-1)
    g_w = jnp.take_along_axis(g_prob, g_sel[:, None], axis=1)
    e_logit = (t @ lp['w_router_expert']).astype(jnp.float32) + lp['b_router_expert'].astype(jnp.float32)
    e_logit = e_logit.reshape(n_tok, N_EXPERT_GROUPS, EXPERTS_PER_GROUP)
    e_in_group = jnp.take_along_axis(e_logit, g_sel[:, None, None], axis=1)[:, 0]
    top_v, top_i = lax.top_k(e_in_group, TOP_K_IN_GROUP)
    weights = g_w * jax.nn.softmax(top_v, axis=-1)
    expert_ids = (g_sel[:, None] * EXPERTS_PER_GROUP + top_i).reshape(-1)
    order = jnp.argsort(expert_ids)
    tok = order // TOP_K_IN_GROUP
    xs = t[tok]
    group_sizes = jnp.bincount(expert_ids, length=N_EXPERTS).astype(jnp.int32)
    hg = lax.ragged_dot(xs, lp['w_gate_e'], group_sizes)
    hu = lax.ragged_dot(xs, lp['w_up_e'], group_sizes)
    y = lax.ragged_dot(jax.nn.silu(hg) * hu, lp['w_down_e'], group_sizes)
    y = y * weights.reshape(-1)[order][:, None].astype(y.dtype)
    out = jnp.zeros_like(t).at[tok].add(y)
    return out.reshape(shp)


def hybrid_layer(x, ctx, c, c_ctx, lp, with_ctx_out):
    bsz = x.shape[0]
    f32 = jnp.float32
    mod_lat = (jax.nn.silu(c) @ lp['w_mod'] + lp['b_mod'])[:, None, :]
    mod_ctx = (jax.nn.silu(c_ctx) @ lp['w_mod'] + lp['b_mod'])[None, None, :]
    sh1, sc1, g1, sh2, sc2, g2 = jnp.split(mod_lat, N_MOD, axis=-1)
    csh1, csc1, cg1, csh2, csc2, cg2 = jnp.split(mod_ctx, N_MOD, axis=-1)
    pre_w, post_w = lp['pre_norm_w'], lp['post_norm_w']

    h_ctx = rms_norm(ctx, pre_w[0]) * (1.0 + csc1) + csh1
    h_lat = rms_norm(x, pre_w[0]) * (1.0 + sc1) + sh1
    p_ctx = h_ctx @ lp['w_in']
    p_lat = h_lat @ lp['w_in']
    zeros_s = jnp.zeros((bsz, DN_HEADS, DN_HEAD_DIM, DN_HEAD_DIM), f32)
    zeros_x = jnp.zeros((bsz, S5_GROUPS, S5_STATE), f32)
    states0 = (zeros_s, zeros_s, zeros_x, zeros_x, zeros_x, zeros_x)
    oa_c, yb_c, states_c = mixer_stream(p_ctx, conv_context, states0, lp)
    oa_l, yb_l, _ = mixer_stream(p_lat, conv_latent_rows, states_c, lp)
    x = x + g1 * rms_norm(mixer_merge(p_lat, oa_l, yb_l, lp), post_w[0])

    h2 = rms_norm(x, pre_w[1]) * (1.0 + sc2) + sh2
    x = x + g2 * rms_norm(hier_moe(h2, lp), post_w[1])

    if with_ctx_out:
        ctx = ctx + cg1 * rms_norm(mixer_merge(p_ctx, oa_c, yb_c, lp), post_w[0])
        h2c = rms_norm(ctx, pre_w[1]) * (1.0 + csc2) + csh2
        ctx = ctx + cg2 * rms_norm(hier_moe(h2c, lp), post_w[1])
    return x, ctx


def setup_inputs(seed: int = 0) -> dict:
    key = jax.random.key(seed)
    ks = jax.random.split(key, 32)
    nrm = jax.random.normal
    D = D_MODEL
    L = DEPTH
    n_idx = jnp.arange(S5_STATE, dtype=jnp.float32)
    dt = jnp.exp(jax.random.uniform(ks[10], (L, 2, DN_HEADS), minval=math.log(1e-3), maxval=math.log(1e-1)))
    return {
        'x': nrm(ks[0], (BATCH, SEQ, D), jnp.float32),
        'c': nrm(ks[1], (BATCH, D), jnp.float32),
        'ctx': nrm(ks[2], (BATCH, CTX_LEN, D), jnp.float32),
        'c_ctx': nrm(ks[3], (D,), jnp.float32),
        'w_mod': nrm(ks[4], (L, D, N_MOD * D)) * (0.5 * D ** -0.5),
        'b_mod': nrm(ks[5], (L, N_MOD * D)) * 0.02,
        'pre_norm_w': 1.0 + 0.02 * nrm(ks[6], (L, 2, D)),
        'post_norm_w': 1.0 + 0.02 * nrm(ks[7], (L, 2, D)),
        'w_in': nrm(ks[8], (L, D, IN_COLS)) * D ** -0.5,
        'conv_w': nrm(ks[9], (L, CONV_K, 3 * DN_WIDTH)) * CONV_K ** -0.5,
        'a_log': jnp.log(jax.random.uniform(ks[11], (L, 2, DN_HEADS), minval=1.0, maxval=16.0)),
        'dt_bias': dt + jnp.log(-jnp.expm1(-dt)),
        'o_norm_w': 1.0 + 0.02 * nrm(ks[12], (L, DN_HEAD_DIM)),
        'w_a_up': nrm(ks[13], (L, DN_WIDTH, D)) * DN_WIDTH ** -0.5,
        's5_lam_re': -0.5 + 0.01 * nrm(ks[14], (L, 2, S5_GROUPS, S5_STATE)),
        's5_lam_im': math.pi * n_idx + 0.01 * nrm(ks[15], (L, 2, S5_GROUPS, S5_STATE)),
        's5_log_step': jax.random.uniform(ks[16], (L, 2, S5_GROUPS), minval=math.log(S5_DT_MIN), maxval=math.log(S5_DT_MAX)),
        's5_b_re': nrm(ks[17], (L, 2, S5_GROUPS, S5_STATE, S5_GROUP)) * (2 * S5_GROUP) ** -0.5,
        's5_b_im': nrm(ks[18], (L, 2, S5_GROUPS, S5_STATE, S5_GROUP)) * (2 * S5_GROUP) ** -0.5,
        's5_c_re': nrm(ks[19], (L, 2, S5_GROUPS, S5_GROUP, S5_STATE)) * S5_STATE ** -0.5,
        's5_c_im': nrm(ks[20], (L, 2, S5_GROUPS, S5_GROUP, S5_STATE)) * S5_STATE ** -0.5,
        's5_d': nrm(ks[21], (L, S5_WIDTH)),
        'w_glu': nrm(ks[22], (L, S5_WIDTH, 2 * D)) * S5_WIDTH ** -0.5,
        'w_out': nrm(ks[23], (L, D, D)) * D ** -0.5,
        'w_router_group': nrm(ks[24], (L, D, N_EXPERT_GROUPS)) * D ** -0.5,
        'b_router_group': nrm(ks[25], (L, N_EXPERT_GROUPS)) * 0.01,
        'w_router_expert': nrm(ks[26], (L, D, N_EXPERTS)) * D ** -0.5,
        'b_router_expert': nrm(ks[27], (L, N_EXPERTS)) * 0.01,
        'w_gate_e': nrm(ks[28], (L, N_EXPERTS, D, EXPERT_FF)) * D ** -0.5,
        'w_up_e': nrm(ks[29], (L, N_EXPERTS, D, EXPERT_FF)) * D ** -0.5,
        'w_down_e': nrm(ks[30], (L, N_EXPERTS, EXPERT_FF, D)) * EXPERT_FF ** -0.5,
    }


def reference(x, c, ctx, c_ctx, w_mod, b_mod, pre_norm_w, post_norm_w, w_in, conv_w, a_log, dt_bias,
              o_norm_w, w_a_up, s5_lam_re, s5_lam_im, s5_log_step, s5_b_re, s5_b_im, s5_c_re, s5_c_im,
              s5_d, w_glu, w_out, w_router_group, b_router_group, w_router_expert, b_router_expert,
              w_gate_e, w_up_e, w_down_e):
    for i in range(DEPTH):
        lp = {
            'w_mod': w_mod[i], 'b_mod': b_mod[i], 'pre_norm_w': pre_norm_w[i], 'post_norm_w': post_norm_w[i],
            'w_in': w_in[i], 'conv_w': conv_w[i], 'a_log': a_log[i], 'dt_bias': dt_bias[i],
            'o_norm_w': o_norm_w[i], 'w_a_up': w_a_up[i],
            's5_lam_re': s5_lam_re[i], 's5_lam_im': s5_lam_im[i], 's5_log_step': s5_log_step[i],
            's5_b_re': s5_b_re[i], 's5_b_im': s5_b_im[i], 's5_c_re': s5_c_re[i], 's5_c_im': s5_c_im[i],
            's5_d': s5_d[i], 'w_glu': w_glu[i], 'w_out': w_out[i],
            'w_router_group': w_router_group[i], 'b_router_group': b_router_group[i],
            'w_router_expert': w_router_expert[i], 'b_router_expert': b_router_expert[i],
            'w_gate_e': w_gate_e[i], 'w_up_e': w_up_e[i], 'w_down_e': w_down_e[i],
        }
        x, ctx = hybrid_layer(x, ctx, c, c_ctx, lp, i < DEPTH - 1)
    return x
```

```python
import functools

import jax
import jax.numpy as jnp
from jax import lax
from jax.experimental import pallas as pl
from jax.experimental.pallas import tpu as pltpu

F32 = jnp.float32
BF16 = jnp.bfloat16
HIGHEST = lax.Precision.HIGHEST

D_MODEL = 1024
GRID_W = 64
DN_HEADS = 4
DN_HEAD_DIM = 128
DN_WIDTH = DN_HEADS * DN_HEAD_DIM
DN_CHUNK = 64
CONV_K = 5
S5_WIDTH = D_MODEL - DN_WIDTH
S5_GROUP = 16
S5_GROUPS = S5_WIDTH // S5_GROUP
S5_STATE = 64
S5_HALF_GROUPS = S5_GROUPS // 2
S5_HALF_STATE = S5_HALF_GROUPS * S5_STATE
S5_LANES = 2 * 2 * S5_HALF_STATE
N_EXPERT_GROUPS = 4
EXPERTS_PER_GROUP = 8
N_EXPERTS = N_EXPERT_GROUPS * EXPERTS_PER_GROUP
EXPERT_FF = 512
NORM_EPS = 1e-6
L2_EPS = 1e-6
N_MOD = 6
LANES = 128
SUBLANES = 8
VMEM_LIMIT = 56 * 1024 * 1024

TOK_TILE = 256
S5_TILE = 64
EXPERT_TILE = 256
RANK_TILE = 512


def _cparams(sem):
    return pltpu.CompilerParams(dimension_semantics=sem, vmem_limit_bytes=VMEM_LIMIT)


def _bdot(a, b):
    return jnp.dot(a.astype(BF16), b.astype(BF16), preferred_element_type=F32)


def _bdot_nt(a, b):
    return lax.dot_general(a.astype(BF16), b.astype(BF16), (((1,), (1,)), ((), ())),
                           preferred_element_type=F32)


def _bdot_tn(a, b):
    return lax.dot_general(a.astype(BF16), b.astype(BF16), (((0,), (0,)), ((), ())),
                           preferred_element_type=F32)


def _silu(x):
    return x * jax.nn.sigmoid(x)


def _rms(x, w):
    return x * lax.rsqrt(jnp.mean(x * x, axis=-1, keepdims=True) + NORM_EPS) * w


def _mod_kernel(c_ref, w_ref, b_ref, o_ref):
    o_ref[...] = jnp.dot(_silu(c_ref[...]), w_ref[...], precision=HIGHEST,
                         preferred_element_type=F32) + b_ref[...]


def _modulation(c16, w_mod, b_mod):
    n = w_mod.shape[1]
    return pl.pallas_call(
        _mod_kernel,
        out_shape=jax.ShapeDtypeStruct((c16.shape[0], n), F32),
        grid=(n // D_MODEL,),
        in_specs=[pl.BlockSpec(c16.shape, lambda j: (0, 0)),
                  pl.BlockSpec((D_MODEL, D_MODEL), lambda j: (0, j)),
                  pl.BlockSpec((1, D_MODEL), lambda j: (0, j))],
        out_specs=pl.BlockSpec((c16.shape[0], D_MODEL), lambda j: (0, j)),
        compiler_params=_cparams(("arbitrary",)),
        name="modulation",
    )(c16, w_mod, b_mod)


def _inproj_kernel(x_ref, mod_ref, prew_ref, wqkv_ref, wz_ref, wab_ref, wu_ref, wg_ref, convw_ref,
                   alog_ref, dtb_ref, *out_refs, seg_len, full):
    if full:
        qkvn_ref, gb_ref, u_ref, z_ref, gates_ref = out_refs
    else:
        qkvn_ref, gb_ref, u_ref = out_refs
    tm = x_ref.shape[0]
    mod = mod_ref[0]
    h = _rms(x_ref[...], prew_ref[...]) * (1.0 + mod[1:2, :]) + mod[0:1, :]
    hb = h.astype(BF16)

    qkv = jnp.dot(hb, wqkv_ref[...], preferred_element_type=F32)
    pos = lax.broadcasted_iota(jnp.int32, (tm, 1), 0) % seg_len
    acc = qkv * convw_ref[CONV_K // 2:CONV_K // 2 + 1, :]
    for k in range(CONV_K):
        s = k - CONV_K // 2
        if s == 0:
            continue
        shifted = pltpu.roll(qkv, (-s) % tm, 0)
        ok = (pos + s >= 0) & (pos + s < seg_len)
        acc = acc + jnp.where(ok, shifted, 0.0) * convw_ref[k:k + 1, :]
    act = _silu(acc)
    for j in range(3 * DN_HEADS):
        sl = slice(j * DN_HEAD_DIM, (j + 1) * DN_HEAD_DIM)
        a = act[:, sl]
        if j < 2 * DN_HEADS:
            a = a * lax.rsqrt(jnp.sum(a * a, axis=-1, keepdims=True) + L2_EPS)
        qkvn_ref[:, sl] = a

    ab = jnp.dot(hb, wab_ref[...], preferred_element_type=F32)
    xa = ab + dtb_ref[...]
    softplus = jnp.maximum(xa, 0.0) + jnp.log1p(jnp.exp(-jnp.abs(xa)))
    g = -jnp.exp(alog_ref[...]) * softplus
    lane = lax.broadcasted_iota(jnp.int32, ab.shape, 1)
    gb_ref[...] = jnp.where(lane < 2 * DN_HEADS, g, jax.nn.sigmoid(ab))

    u_ref[...] = jnp.dot(hb, wu_ref[...], preferred_element_type=F32)
    if full:
        z_ref[...] = jnp.dot(hb, wz_ref[...], preferred_element_type=F32)
        gates_ref[...] = jnp.dot(hb, wg_ref[...], preferred_element_type=F32)


def _input_projection(x2d, mod3, mod_row0, mod_row_stride, prew, wts, convw, alog, dtb, *, bsz, seq, seg_len, full):
    wqkv, wz, wab, wu, wg = wts
    tm = TOK_TILE
    nt = seq // tm
    rows = bsz * seq
    row_map = lambda b, i: (b * nt + i, 0)
    const = lambda b, i: (0, 0)
    out_shape = [jax.ShapeDtypeStruct((rows, 3 * DN_WIDTH), F32),
                 jax.ShapeDtypeStruct((rows, LANES), F32),
                 jax.ShapeDtypeStruct((seq, bsz * S5_WIDTH), F32)]
    out_specs = [pl.BlockSpec((tm, 3 * DN_WIDTH), row_map),
                 pl.BlockSpec((tm, LANES), row_map),
                 pl.BlockSpec((tm, S5_WIDTH), lambda b, i: (i, b))]
    if full:
        out_shape += [jax.ShapeDtypeStruct((rows, DN_WIDTH), F32),
                      jax.ShapeDtypeStruct((rows, 2 * D_MODEL), F32)]
        out_specs += [pl.BlockSpec((tm, DN_WIDTH), row_map),
                      pl.BlockSpec((tm, 2 * D_MODEL), row_map)]
    return pl.pallas_call(
        functools.partial(_inproj_kernel, seg_len=seg_len, full=full),
        out_shape=out_shape,
        grid=(bsz, nt),
        in_specs=[pl.BlockSpec((tm, D_MODEL), row_map),
                  pl.BlockSpec((1, N_MOD, D_MODEL), lambda b, i: (mod_row0 + mod_row_stride * b, 0, 0)),
                  pl.BlockSpec((1, D_MODEL), const),
                  pl.BlockSpec(wqkv.shape, const),
                  pl.BlockSpec(wz.shape, const),
                  pl.BlockSpec(wab.shape, const),
                  pl.BlockSpec(wu.shape, const),
                  pl.BlockSpec(wg.shape, const),
                  pl.BlockSpec(convw.shape, const),
                  pl.BlockSpec((1, LANES), const),
                  pl.BlockSpec((1, LANES), const)],
        out_specs=out_specs,
        compiler_params=_cparams(("arbitrary", "arbitrary")),
        name="input_projection_full" if full else "input_projection_ctx",
    )(x2d, mod3, prew, wqkv, wz, wab, wu, wg, convw, alog, dtb)


def _delta_kernel(qf_ref, kf_ref, vf_ref, gbf_ref, qb_ref, kb_ref, vb_ref, gbb_ref, s0_ref,
                  of_ref, ob_ref, sout_ref, state_ref, *, n_chunks):
    c = DN_CHUNK
    i = pl.program_id(1)

    @pl.when(i == 0)
    def _():
        state_ref[...] = s0_ref[0]

    row = lax.broadcasted_iota(jnp.int32, (c, c), 0)
    col = lax.broadcasted_iota(jnp.int32, (c, c), 1)
    eye = (row == col).astype(F32)
    scale = DN_HEAD_DIM ** -0.5
    dirs = ((qf_ref, kf_ref, vf_ref, gbf_ref, of_ref), (qb_ref, kb_ref, vb_ref, gbb_ref, ob_ref))
    for d, (q_ref, k_ref, v_ref, gb_ref, o_ref) in enumerate(dirs):
        incl = (row >= col) if d == 0 else (row <= col)
        strict = (row > col) if d == 0 else (row < col)
        last = c - 1 if d == 0 else 0
        gb = gb_ref[...]
        gcum = jnp.dot(incl.astype(F32), gb, precision=HIGHEST, preferred_element_type=F32)
        gcum_t = gcum.T
        for h in range(DN_HEADS):
            lane = d * DN_HEADS + h
            hs = slice(h * DN_HEAD_DIM, (h + 1) * DN_HEAD_DIM)
            g_col = gcum[:, lane:lane + 1]
            g_row = gcum_t[lane:lane + 1, :]
            g_last = gcum[last:last + 1, lane:lane + 1]
            beta = gb[:, 2 * DN_HEADS + lane:2 * DN_HEADS + lane + 1]
            dec = jnp.where(incl, jnp.exp(jnp.where(incl, g_col - g_row, 0.0)), 0.0)
            q = q_ref[:, hs] * scale
            k = k_ref[:, hs]
            v = v_ref[:, hs]
            k_beta = k * beta
            e_g = jnp.exp(g_col)
            a_mat = jnp.where(strict, _bdot_nt(k_beta, k) * dec, 0.0)
            attn = _bdot_nt(q, k) * dec
            npow = -a_mat
            t_inv = eye + npow
            for _ in range(5):
                npow = _bdot(npow, npow)
                t_inv = t_inv + _bdot(t_inv, npow)
            sol = _bdot(t_inv, jnp.concatenate([v * beta, k_beta * e_g], axis=1))
            u = sol[:, :DN_HEAD_DIM]
            w = sol[:, DN_HEAD_DIM:]
            s = state_ref[lane]
            v_new = u - _bdot(w, s)
            o_ref[:, hs] = _bdot(q * e_g, s) + _bdot(attn, v_new)
            k_tail = k * jnp.exp(g_last - g_col)
            state_ref[lane] = s * jnp.exp(g_last) + _bdot_tn(k_tail, v_new)

    @pl.when(i == n_chunks - 1)
    def _():
        sout_ref[0] = state_ref[...]


def _delta_rule(qkvn, gb, s0, *, bsz, seq):
    c = DN_CHUNK
    nc = seq // c
    fwd = lambda j: (lambda b, i: (b * nc + i, j))
    bwd = lambda j: (lambda b, i: (b * nc + nc - 1 - i, j))
    blk = lambda m: pl.BlockSpec((c, DN_WIDTH), m)
    gblk = lambda m: pl.BlockSpec((c, LANES), m)
    st = pl.BlockSpec((1, 2 * DN_HEADS, DN_HEAD_DIM, DN_HEAD_DIM), lambda b, i: (b, 0, 0, 0))
    return pl.pallas_call(
        functools.partial(_delta_kernel, n_chunks=nc),
        out_shape=[jax.ShapeDtypeStruct((bsz * seq, DN_WIDTH), F32),
                   jax.ShapeDtypeStruct((bsz * seq, DN_WIDTH), F32),
                   jax.ShapeDtypeStruct(s0.shape, F32)],
        grid=(bsz, nc),
        in_specs=[blk(fwd(0)), blk(fwd(1)), blk(fwd(2)), gblk(fwd(0)),
                  blk(bwd(0)), blk(bwd(1)), blk(bwd(2)), gblk(bwd(0)), st],
        out_specs=[blk(fwd(0)), blk(bwd(0)), st],
        scratch_shapes=[pltpu.VMEM((2 * DN_HEADS, DN_HEAD_DIM, DN_HEAD_DIM), F32)],
        compiler_params=_cparams(("arbitrary", "arbitrary")),
        name="delta_rule",
    )(qkvn, qkvn, qkvn, gb, qkvn, qkvn, qkvn, gb, s0)


def _s5_disc_kernel(lre_ref, lim_ref, ls_ref, bre_ref, bim_ref, are_ref, aim_ref, bbre_ref, bbim_ref):
    lam_re = lre_ref[...]
    lam_im = lim_ref[...]
    dt = jnp.exp(ls_ref[...])
    mag = jnp.exp(lam_re * dt)
    ang = lam_im * dt
    bar_re = mag * jnp.cos(ang)
    bar_im = mag * jnp.sin(ang)
    den = lam_re * lam_re + lam_im * lam_im
    coef_re = ((bar_re - 1.0) * lam_re + bar_im * lam_im) / den
    coef_im = (bar_im * lam_re - (bar_re - 1.0) * lam_im) / den
    are_ref[...] = bar_re
    aim_ref[...] = bar_im
    bbre_ref[...] = coef_re * bre_ref[...] - coef_im * bim_ref[...]
    bbim_ref[...] = coef_re * bim_ref[...] + coef_im * bre_ref[...]


def _s5_discretise(lam_re, lam_im, log_step, b_re, b_im):
    rep = lambda t: jnp.repeat(t.reshape(2 * S5_GROUPS, -1), S5_GROUP, axis=0)
    rows = 2 * S5_GROUPS * S5_GROUP
    tr = lambda t: jnp.swapaxes(t, -1, -2).reshape(rows, S5_STATE)
    outs = pl.pallas_call(
        _s5_disc_kernel,
        out_shape=[jax.ShapeDtypeStruct((rows, S5_STATE), F32)] * 4,
        name="s5_discretise",
    )(rep(lam_re), rep(lam_im), rep(log_step[..., None]), tr(b_re), tr(b_im))
    shp = (2, S5_GROUPS, S5_GROUP, S5_STATE)
    bar_re, bar_im, bb_re, bb_im = (o.reshape(shp) for o in outs)
    return bar_re[:, :, 0], bar_im[:, :, 0], bb_re, bb_im


def _s5_kernel(uf_ref, ub_ref, wb_ref, wcre_ref, wcim_ref, are_ref, aim_ref, x0_ref, *rest, tt, n_tiles, emit_y):
    if emit_y:
        yf_ref, yb_ref, xout_ref, buf_ref, xst_ref = rest
    else:
        xout_ref, buf_ref, xst_ref = rest
        yf_ref = yb_ref = None
    i = pl.program_id(0)
    hs = S5_HALF_STATE
    half_ch = S5_WIDTH // 2
    blk = 512

    @pl.when(i == 0)
    def _():
        xst_ref[...] = x0_ref[...]

    for d, (u_ref, y_ref) in enumerate(((uf_ref, yf_ref), (ub_ref, yb_ref))):
        u = u_ref[...].astype(BF16)
        for hh in range(2):
            buf_ref[d, :, hh * 2 * hs:(hh + 1) * 2 * hs] = jnp.dot(
                u[:, hh * half_ch:(hh + 1) * half_ch], wb_ref[d, hh], preferred_element_type=F32)
        for hh in range(2):
            for b in range(hs // blk):
                lr = hh * 2 * hs + b * blk
                li = lr + hs
                a_re = are_ref[d, :, lr:lr + blk]
                a_im = aim_ref[d, :, lr:lr + blk]

                def body(t, carry, d=d, lr=lr, li=li, a_re=a_re, a_im=a_im):
                    x_re, x_im = carry
                    step = t if d == 0 else tt - 1 - t
                    r0 = pl.multiple_of(step * SUBLANES, SUBLANES)
                    n_re = a_re * x_re - a_im * x_im + buf_ref[d, pl.ds(r0, SUBLANES), lr:lr + blk]
                    n_im = a_re * x_im + a_im * x_re + buf_ref[d, pl.ds(r0, SUBLANES), li:li + blk]
                    buf_ref[d, pl.ds(r0, SUBLANES), lr:lr + blk] = n_re
                    buf_ref[d, pl.ds(r0, SUBLANES), li:li + blk] = n_im
                    return n_re, n_im

                x_re, x_im = lax.fori_loop(0, tt, body, (xst_ref[d, :, lr:lr + blk], xst_ref[d, :, li:li + blk]),
                                           unroll=8)
                xst_ref[d, :, lr:lr + blk] = x_re
                xst_ref[d, :, li:li + blk] = x_im
        if emit_y:
            for hh in range(2):
                x_re = buf_ref[d, :, hh * 2 * hs:hh * 2 * hs + hs]
                x_im = buf_ref[d, :, hh * 2 * hs + hs:(hh + 1) * 2 * hs]
                y_ref[:, hh * half_ch:(hh + 1) * half_ch] = (
                    _bdot(x_re, wcre_ref[d, hh]) - _bdot(x_im, wcim_ref[d, hh]))

    @pl.when(i == n_tiles - 1)
    def _():
        xout_ref[...] = xst_ref[...]


def _s5_scan(u_rows, s5w, x0, *, seq, bsz, emit_y):
    wb, wc_re, wc_im, a_re, a_im = s5w
    tt = S5_TILE
    nt = seq // tt
    rows = tt * bsz
    full = lambda a: pl.BlockSpec(a.shape, lambda i: (0,) * a.ndim)
    ublk = lambda m: pl.BlockSpec((rows, S5_WIDTH), m)
    fwd = lambda i: (i, 0)
    bwd = lambda i: (nt - 1 - i, 0)
    out_shape = [jax.ShapeDtypeStruct(x0.shape, F32)]
    out_specs = [full(x0)]
    if emit_y:
        out_shape = [jax.ShapeDtypeStruct(u_rows.shape, F32)] * 2 + out_shape
        out_specs = [ublk(fwd), ublk(bwd)] + out_specs
    return pl.pallas_call(
        functools.partial(_s5_kernel, tt=tt, n_tiles=nt, emit_y=emit_y),
        out_shape=out_shape,
        grid=(nt,),
        in_specs=[ublk(fwd), ublk(bwd), full(wb), full(wc_re), full(wc_im), full(a_re), full(a_im), full(x0)],
        out_specs=out_specs,
        scratch_shapes=[pltpu.VMEM((2, rows, S5_LANES), F32), pltpu.VMEM(x0.shape, F32)],
        compiler_params=_cparams(("arbitrary",)),
        name="s5_scan_y" if emit_y else "s5_scan_state",
    )(u_rows, u_rows, wb, wc_re, wc_im, a_re, a_im, x0)


def _s5_weights(bar_re, bar_im, bb_re, bb_im, c_re, c_im, bsz):
    hg = S5_HALF_GROUPS
    eye = jnp.eye(hg, dtype=F32)

    def in_mat(bb):
        t = bb.reshape(2, 2, hg, S5_GROUP, S5_STATE)
        t = jnp.einsum('dhicp,ij->dhicjp', t, eye)
        return t.reshape(2, 2, hg * S5_GROUP, hg * S5_STATE)

    def out_mat(cc):
        t = cc.reshape(2, 2, hg, S5_GROUP, S5_STATE)
        t = jnp.einsum('dhicp,ij->dhipjc', t, eye)
        return t.reshape(2, 2, hg * S5_STATE, hg * S5_GROUP)

    wb = jnp.concatenate([in_mat(bb_re), in_mat(bb_im)], axis=-1).astype(BF16)
    wc_re = out_mat(c_re).astype(BF16)
    wc_im = out_mat(c_im).astype(BF16)

    def lanes(a):
        t = a.reshape(2, 2, 1, S5_HALF_STATE)
        t = jnp.broadcast_to(t, (2, 2, 2, S5_HALF_STATE)).reshape(2, 1, S5_LANES)
        return jnp.broadcast_to(t, (2, bsz, S5_LANES))

    return wb, wc_re, wc_im, lanes(bar_re), lanes(bar_im)


def _merge_kernel(x_ref, mod_ref, of_ref, ob_ref, z_ref, yf_ref, yb_ref, u_ref, gates_ref, onw_ref, waup_ref,
                  s5d_ref, wglu_ref, wout_ref, postw_ref, prew_ref, wr_ref, br_ref,
                  x1_ref, h2_ref, route_ref):
    mod = mod_ref[0]
    o_a = of_ref[...] + ob_ref[...]
    z = z_ref[...]
    parts = []
    for h in range(DN_HEADS):
        hs = slice(h * DN_HEAD_DIM, (h + 1) * DN_HEAD_DIM)
        parts.append(_rms(o_a[:, hs], onw_ref[...]) * _silu(z[:, hs]))
    y_a = _bdot(jnp.concatenate(parts, axis=1), waup_ref[...])
    y5 = yf_ref[...] + yb_ref[...] + s5d_ref[...] * u_ref[...]
    glu = _bdot(jax.nn.gelu(y5, approximate=True), wglu_ref[...])
    y_b = glu[:, :D_MODEL] * jax.nn.sigmoid(glu[:, D_MODEL:])
    gates = gates_ref[...]
    y = jax.nn.sigmoid(gates[:, :D_MODEL]) * y_a + jax.nn.sigmoid(gates[:, D_MODEL:]) * y_b
    mixed = _bdot(y, wout_ref[...])
    x1 = x_ref[...] + mod[2:3, :] * _rms(mixed, postw_ref[...])
    x1_ref[...] = x1
    h2 = _rms(x1, prew_ref[...]) * (1.0 + mod[4:5, :]) + mod[3:4, :]
    h2_ref[...] = h2

    logits = jnp.dot(h2, wr_ref[...], precision=HIGHEST, preferred_element_type=F32) + br_ref[...]
    lane = lax.broadcasted_iota(jnp.int32, logits.shape, 1)
    neg = -jnp.inf
    big = jnp.int32(LANES)
    first = lambda hit: jnp.min(jnp.where(hit, lane, big), axis=-1, keepdims=True)
    g_logit = jnp.where(lane < N_EXPERT_GROUPS, logits, neg)
    g_max = jnp.max(g_logit, axis=-1, keepdims=True)
    g_sel = first(g_logit == g_max)
    g_w = 1.0 / jnp.sum(jnp.exp(g_logit - g_max), axis=-1, keepdims=True)
    e_idx = lane - N_EXPERT_GROUPS
    in_group = (e_idx >= 0) & (e_idx < N_EXPERTS) & ((e_idx // EXPERTS_PER_GROUP) == g_sel)
    e_logit = jnp.where(in_group, logits, neg)
    m1 = jnp.max(e_logit, axis=-1, keepdims=True)
    i1 = first(e_logit == m1)
    e_logit2 = jnp.where(lane == i1, neg, e_logit)
    m2 = jnp.max(e_logit2, axis=-1, keepdims=True)
    i2 = first(e_logit2 == m2)
    e2 = jnp.exp(m2 - m1)
    w1 = g_w / (1.0 + e2)
    w2 = g_w * e2 / (1.0 + e2)
    id1 = (i1 - N_EXPERT_GROUPS).astype(F32)
    id2 = (i2 - N_EXPERT_GROUPS).astype(F32)
    route_ref[...] = jnp.where(lane == 0, id1, jnp.where(lane == 1, id2, jnp.where(lane == 2, w1,
                               jnp.where(lane == 3, w2, 0.0))))


def _merge(x2d, mod3, o_f, o_b, z, y_f, y_b, u_tb, gates, wts, *, bsz, seq):
    onw, waup, s5d, wglu, wout, postw, prew, wr, br = wts
    tm = TOK_TILE
    nt = seq // tm
    rows = bsz * seq
    row_map = lambda b, i: (b * nt + i, 0)
    tb_map = lambda b, i: (i, b)
    const = lambda b, i: (0, 0)
    cs = lambda a: pl.BlockSpec(a.shape, const)
    return pl.pallas_call(
        _merge_kernel,
        out_shape=[jax.ShapeDtypeStruct((rows, D_MODEL), F32),
                   jax.ShapeDtypeStruct((rows, D_MODEL), F32),
                   jax.ShapeDtypeStruct((rows, LANES), F32)],
        grid=(bsz, nt),
        in_specs=[pl.BlockSpec((tm, D_MODEL), row_map),
                  pl.BlockSpec((1, N_MOD, D_MODEL), lambda b, i: (b, 0, 0)),
                  pl.BlockSpec((tm, DN_WIDTH), row_map),
                  pl.BlockSpec((tm, DN_WIDTH), row_map),
                  pl.BlockSpec((tm, DN_WIDTH), row_map),
                  pl.BlockSpec((tm, S5_WIDTH), tb_map),
                  pl.BlockSpec((tm, S5_WIDTH), tb_map),
                  pl.BlockSpec((tm, S5_WIDTH), tb_map),
                  pl.BlockSpec((tm, 2 * D_MODEL), row_map),
                  cs(onw), cs(waup), cs(s5d), cs(wglu), cs(wout), cs(postw), cs(prew), cs(wr), cs(br)],
        out_specs=[pl.BlockSpec((tm, D_MODEL), row_map),
                   pl.BlockSpec((tm, D_MODEL), row_map),
                   pl.BlockSpec((tm, LANES), row_map)],
        compiler_params=_cparams(("arbitrary", "arbitrary")),
        name="merge_router",
    )(x2d, mod3, o_f, o_b, z, y_f, y_b, u_tb, gates, onw, waup, s5d, wglu, wout, postw, prew, wr, br)


def _rank_kernel(route_ref, rank_ref, count_ref, carry_ref):
    i = pl.program_id(0)

    @pl.when(i == 0)
    def _():
        carry_ref[...] = jnp.zeros_like(carry_ref)

    route = route_ref[...]
    r = route.shape[0]
    lane = lax.broadcasted_iota(jnp.int32, route.shape, 1).astype(F32)
    oh0 = (lane == route[:, 0:1]).astype(F32)
    oh1 = (lane == route[:, 1:2]).astype(F32)
    both = oh0 + oh1
    row = lax.broadcasted_iota(jnp.int32, (r, r), 0)
    col = lax.broadcasted_iota(jnp.int32, (r, r), 1)
    before = _bdot((row > col).astype(F32), both) + carry_ref[...]
    rank0 = jnp.sum(oh0 * before, axis=-1, keepdims=True)
    rank1 = jnp.sum(oh1 * before, axis=-1, keepdims=True)
    lane_i = lax.broadcasted_iota(jnp.int32, route.shape, 1)
    rank_ref[...] = jnp.where(lane_i == 0, rank0, jnp.where(lane_i == 1, rank1, 0.0))
    carry_ref[...] = carry_ref[...] + jnp.sum(both, axis=0, keepdims=True)
    count_ref[...] = carry_ref[...]


def _expert_ranks(route):
    rows = route.shape[0]
    r = RANK_TILE
    return pl.pallas_call(
        _rank_kernel,
        out_shape=[jax.ShapeDtypeStruct((rows, LANES), F32), jax.ShapeDtypeStruct((1, LANES), F32)],
        grid=(rows // r,),
        in_specs=[pl.BlockSpec((r, LANES), lambda i: (i, 0))],
        out_specs=[pl.BlockSpec((r, LANES), lambda i: (i, 0)), pl.BlockSpec((1, LANES), lambda i: (0, 0))],
        scratch_shapes=[pltpu.VMEM((1, LANES), F32)],
        compiler_params=_cparams(("arbitrary",)),
        name="expert_ranks",
    )(route)


def _row_copy(src_ref, src_row, dst_ref, dst_row, sem):
    return pltpu.make_async_copy(src_ref.at[pl.ds(src_row, 1), :], dst_ref.at[pl.ds(dst_row, 1), :], sem)


def _dispatch_kernel(pos_ref, h2_ref, xs_in_ref, xs_ref, sem):
    del xs_in_ref
    tm = h2_ref.shape[0]

    def start(t, _):
        _row_copy(h2_ref, t, xs_ref, pos_ref[0, 0, 2 * t], sem).start()
        _row_copy(h2_ref, t, xs_ref, pos_ref[0, 0, 2 * t + 1], sem).start()
        return 0

    lax.fori_loop(0, tm, start, 0)

    def wait(t, _):
        _row_copy(h2_ref, 0, xs_ref, 0, sem).wait()
        _row_copy(h2_ref, 0, xs_ref, 0, sem).wait()
        return 0

    lax.fori_loop(0, tm, wait, 0)


def _dispatch(pos3, h2, xs_zero):
    rows = h2.shape[0]
    tm = TOK_TILE
    return pl.pallas_call(
        _dispatch_kernel,
        out_shape=jax.ShapeDtypeStruct(xs_zero.shape, xs_zero.dtype),
        grid=(rows // tm,),
        in_specs=[pl.BlockSpec((1, 1, 2 * tm), lambda i: (i, 0, 0), memory_space=pltpu.SMEM),
                  pl.BlockSpec((tm, D_MODEL), lambda i: (i, 0)),
                  pl.BlockSpec(memory_space=pl.ANY)],
        out_specs=pl.BlockSpec(memory_space=pl.ANY),
        scratch_shapes=[pltpu.SemaphoreType.DMA],
        input_output_aliases={2: 0},
        compiler_params=_cparams(("arbitrary",)),
        name="dispatch",
    )(pos3, h2, xs_zero)


def _expert_kernel(te_ref, tv_ref, xs_ref, wg_ref, wu_ref, wd_ref, ys_ref):
    i = pl.program_id(0)

    @pl.when(tv_ref[i] > 0)
    def _():
        x = xs_ref[...].astype(BF16)
        hg = jnp.dot(x, wg_ref[0], preferred_element_type=F32)
        hu = jnp.dot(x, wu_ref[0], preferred_element_type=F32)
        ys_ref[...] = _bdot(_silu(hg) * hu, wd_ref[0])

    @pl.when(tv_ref[i] == 0)
    def _():
        ys_ref[...] = jnp.zeros_like(ys_ref)


def _expert_mlp(tile_expert, tile_valid, xs, wg, wu, wd):
    rows = xs.shape[0]
    te = EXPERT_TILE
    return pl.pallas_call(
        _expert_kernel,
        out_shape=jax.ShapeDtypeStruct((rows, D_MODEL), F32),
        grid_spec=pltpu.PrefetchScalarGridSpec(
            num_scalar_prefetch=2,
            grid=(rows // te,),
            in_specs=[pl.BlockSpec((te, D_MODEL), lambda i, e, v: (i, 0)),
                      pl.BlockSpec((1, D_MODEL, EXPERT_FF), lambda i, e, v: (e[i], 0, 0)),
                      pl.BlockSpec((1, D_MODEL, EXPERT_FF), lambda i, e, v: (e[i], 0, 0)),
                      pl.BlockSpec((1, EXPERT_FF, D_MODEL), lambda i, e, v: (e[i], 0, 0))],
            out_specs=pl.BlockSpec((te, D_MODEL), lambda i, e, v: (i, 0))),
        compiler_params=_cparams(("arbitrary",)),
        name="expert_mlp",
    )(tile_expert, tile_valid, xs, wg, wu, wd)


def _combine_kernel(pos_ref, x1_ref, mod_ref, route_ref, postw_ref, ys_ref, out_ref, buf_ref, sem):
    tm = x1_ref.shape[0]

    def start(t, _):
        _row_copy(ys_ref, pos_ref[0, 0, 2 * t], buf_ref.at[0], t, sem).start()
        _row_copy(ys_ref, pos_ref[0, 0, 2 * t + 1], buf_ref.at[1], t, sem).start()
        return 0

    lax.fori_loop(0, tm, start, 0)

    def wait(t, _):
        _row_copy(ys_ref, 0, buf_ref.at[0], 0, sem).wait()
        _row_copy(ys_ref, 0, buf_ref.at[1], 0, sem).wait()
        return 0

    lax.fori_loop(0, tm, wait, 0)
    route = route_ref[...]
    moe = route[:, 2:3] * buf_ref[0] + route[:, 3:4] * buf_ref[1]
    out_ref[...] = x1_ref[...] + mod_ref[0][5:6, :] * _rms(moe, postw_ref[...])


def _combine(pos3, x1, mod3, route, postw, ys, *, bsz, seq):
    tm = TOK_TILE
    nt = seq // tm
    row_map = lambda b, i: (b * nt + i, 0)
    return pl.pallas_call(
        _combine_kernel,
        out_shape=jax.ShapeDtypeStruct(x1.shape, F32),
        grid=(bsz, nt),
        in_specs=[pl.BlockSpec((1, 1, 2 * tm), lambda b, i: (b * nt + i, 0, 0), memory_space=pltpu.SMEM),
                  pl.BlockSpec((tm, D_MODEL), row_map),
                  pl.BlockSpec((1, N_MOD, D_MODEL), lambda b, i: (b, 0, 0)),
                  pl.BlockSpec((tm, LANES), row_map),
                  pl.BlockSpec((1, D_MODEL), lambda b, i: (0, 0)),
                  pl.BlockSpec(memory_space=pl.ANY)],
        out_specs=pl.BlockSpec((tm, D_MODEL), row_map),
        scratch_shapes=[pltpu.VMEM((2, tm, D_MODEL), F32), pltpu.SemaphoreType.DMA],
        compiler_params=_cparams(("arbitrary", "arbitrary")),
        name="combine",
    )(pos3, x1, mod3, route, postw, ys)


def _lane_pad(v, n=LANES):
    v = v.reshape(1, -1)
    return jnp.pad(v, ((0, 0), (0, n - v.shape[1])))


def kernel(x, c, ctx, c_ctx, w_mod, b_mod, pre_norm_w, post_norm_w, w_in, conv_w, a_log, dt_bias, o_norm_w,
           w_a_up, s5_lam_re, s5_lam_im, s5_log_step, s5_b_re, s5_b_im, s5_c_re, s5_c_im, s5_d, w_glu, w_out,
           w_router_group, b_router_group, w_router_expert, b_router_expert, w_gate_e, w_up_e, w_down_e):
    assert w_mod.shape[0] == 1, "single-layer block"
    bsz, seq, _ = x.shape
    ctx_len = ctx.shape[1]
    row = lambda v: v.reshape(1, -1)

    c16 = jnp.concatenate([c, c_ctx[None, :], jnp.zeros((2 * SUBLANES - bsz - 1, D_MODEL), F32)], axis=0)
    mod3 = _modulation(c16, w_mod[0], row(b_mod[0])).reshape(2 * SUBLANES, N_MOD, D_MODEL)

    wi = w_in[0]
    o_z, o_a, o_u, o_g = 3 * DN_WIDTH, 4 * DN_WIDTH, 4 * DN_WIDTH + 4 * DN_HEADS, 4 * DN_WIDTH + 4 * DN_HEADS + S5_WIDTH
    wts_in = (wi[:, :o_z].astype(BF16), wi[:, o_z:o_a].astype(BF16),
              jnp.pad(wi[:, o_a:o_u], ((0, 0), (0, LANES - 4 * DN_HEADS))).astype(BF16),
              wi[:, o_u:o_g].astype(BF16), wi[:, o_g:].astype(BF16))
    convw = jnp.pad(conv_w[0], ((0, SUBLANES - CONV_K), (0, 0)))
    alog = _lane_pad(a_log[0])
    dtb = _lane_pad(dt_bias[0])
    prew0, prew1 = row(pre_norm_w[0, 0]), row(pre_norm_w[0, 1])
    postw0, postw1 = row(post_norm_w[0, 0]), row(post_norm_w[0, 1])

    bar_re, bar_im, bb_re, bb_im = _s5_discretise(s5_lam_re[0], s5_lam_im[0], s5_log_step[0], s5_b_re[0], s5_b_im[0])
    s5w = _s5_weights(bar_re, bar_im, bb_re, bb_im, s5_c_re[0], s5_c_im[0], bsz)

    qkvn_c, gb_c, u_c = _input_projection(ctx.reshape(bsz * ctx_len, D_MODEL), mod3, bsz, 0, prew0, wts_in, convw,
                                          alog, dtb, bsz=bsz, seq=ctx_len, seg_len=ctx_len, full=False)
    s_zero = jnp.zeros((bsz, 2 * DN_HEADS, DN_HEAD_DIM, DN_HEAD_DIM), F32)
    _, _, s_ctx = _delta_rule(qkvn_c, gb_c, s_zero, bsz=bsz, seq=ctx_len)
    x_zero = jnp.zeros((2, bsz, S5_LANES), F32)
    (x_ctx,) = _s5_scan(u_c.reshape(ctx_len * bsz, S5_WIDTH), s5w, x_zero, seq=ctx_len, bsz=bsz, emit_y=False)

    x2d = x.reshape(bsz * seq, D_MODEL)
    qkvn, gb, u_tb, z, gates = _input_projection(x2d, mod3, 0, 1, prew0, wts_in, convw, alog, dtb,
                                                 bsz=bsz, seq=seq, seg_len=GRID_W, full=True)
    o_f, o_b, _ = _delta_rule(qkvn, gb, s_ctx, bsz=bsz, seq=seq)
    y_f, y_b, _ = _s5_scan(u_tb.reshape(seq * bsz, S5_WIDTH), s5w, x_ctx, seq=seq, bsz=bsz, emit_y=True)
    tb = lambda a: a.reshape(seq, bsz * S5_WIDTH)

    w_router = jnp.pad(jnp.concatenate([w_router_group[0], w_router_expert[0]], axis=1),
                       ((0, 0), (0, LANES - N_EXPERT_GROUPS - N_EXPERTS)))
    b_router = _lane_pad(jnp.concatenate([b_router_group[0], b_router_expert[0]]))
    wts_merge = (row(o_norm_w[0]), w_a_up[0].astype(BF16), row(s5_d[0]), w_glu[0].astype(BF16),
                 w_out[0].astype(BF16), postw0, prew1, w_router, b_router)
    x1, h2, route = _merge(x2d, mod3, o_f, o_b, z, tb(y_f), tb(y_b), u_tb, gates, wts_merge, bsz=bsz, seq=seq)

    rank, count = _expert_ranks(route)
    n_tok = bsz * seq
    te = EXPERT_TILE
    counts = count[0, :N_EXPERTS].astype(jnp.int32)
    padded = ((counts + te - 1) // te) * te
    ends = jnp.cumsum(padded)
    starts = ends - padded
    ids = route[:, :2].astype(jnp.int32)
    pos = starts[ids] + rank[:, :2].astype(jnp.int32)
    pos3 = pos.reshape(n_tok // TOK_TILE, 1, 2 * TOK_TILE)
    n_sorted = 2 * n_tok + N_EXPERTS * te
    tile_start = jnp.arange(n_sorted // te, dtype=jnp.int32) * te
    tile_valid = (tile_start < ends[-1]).astype(jnp.int32)
    tile_expert = jnp.minimum(jnp.searchsorted(ends, tile_start, side='right'), N_EXPERTS - 1).astype(jnp.int32)

    xs = _dispatch(pos3, h2, jnp.zeros((n_sorted, D_MODEL), F32))
    ys = _expert_mlp(tile_expert, tile_valid, xs, w_gate_e[0].astype(BF16), w_up_e[0].astype(BF16),
                     w_down_e[0].astype(BF16))
    out = _combine(pos3, x1, mod3, route, postw1, ys, bsz=bsz, seq=seq)
    return out.reshape(bsz, seq, D_MODEL)
```

```python
import functools

import jax
import jax.numpy as jnp
from jax import lax
from jax.experimental import pallas as pl
from jax.experimental.pallas import tpu as pltpu

F32 = jnp.float32
BF16 = jnp.bfloat16
HIGHEST = lax.Precision.HIGHEST

D_MODEL = 1024
GRID_W = 64
DN_HEADS = 4
DN_HEAD_DIM = 128
DN_WIDTH = DN_HEADS * DN_HEAD_DIM
DN_CHUNK = 64
CONV_K = 5
S5_WIDTH = D_MODEL - DN_WIDTH
S5_GROUP = 16
S5_GROUPS = S5_WIDTH // S5_GROUP
S5_STATE = 64
S5_HALF_GROUPS = S5_GROUPS // 2
S5_HALF_STATE = S5_HALF_GROUPS * S5_STATE
S5_LANES = 2 * 2 * S5_HALF_STATE
N_EXPERT_GROUPS = 4
EXPERTS_PER_GROUP = 8
N_EXPERTS = N_EXPERT_GROUPS * EXPERTS_PER_GROUP
EXPERT_FF = 512
NORM_EPS = 1e-6
L2_EPS = 1e-6
N_MOD = 6
LANES = 128
SUBLANES = 8
VMEM_LIMIT = 56 * 1024 * 1024

TOK_TILE = 256
S5_TILE = 64
EXPERT_TILE = 256
RANK_TILE = 512


def _cparams(sem):
    return pltpu.CompilerParams(dimension_semantics=sem, vmem_limit_bytes=VMEM_LIMIT)


def _bdot(a, b):
    return jnp.dot(a.astype(BF16), b.astype(BF16), preferred_element_type=F32)


def _bdot_nt(a, b):
    return lax.dot_general(a.astype(BF16), b.astype(BF16), (((1,), (1,)), ((), ())),
                           preferred_element_type=F32)


def _bdot_tn(a, b):
    return lax.dot_general(a.astype(BF16), b.astype(BF16), (((0,), (0,)), ((), ())),
                           preferred_element_type=F32)


def _silu(x):
    return x * jax.nn.sigmoid(x)


def _rms(x, w):
    return x * lax.rsqrt(jnp.mean(x * x, axis=-1, keepdims=True) + NORM_EPS) * w


def _mod_kernel(c_ref, w_ref, b_ref, o_ref):
    o_ref[...] = jnp.dot(_silu(c_ref[...]), w_ref[...], precision=HIGHEST,
                         preferred_element_type=F32) + b_ref[...]


def _modulation(c16, w_mod, b_mod):
    n = w_mod.shape[1]
    return pl.pallas_call(
        _mod_kernel,
        out_shape=jax.ShapeDtypeStruct((c16.shape[0], n), F32),
        grid=(n // D_MODEL,),
        in_specs=[pl.BlockSpec(c16.shape, lambda j: (0, 0)),
                  pl.BlockSpec((D_MODEL, D_MODEL), lambda j: (0, j)),
                  pl.BlockSpec((1, D_MODEL), lambda j: (0, j))],
        out_specs=pl.BlockSpec((c16.shape[0], D_MODEL), lambda j: (0, j)),
        compiler_params=_cparams(("arbitrary",)),
        name="modulation",
    )(c16, w_mod, b_mod)


def _inproj_kernel(x_ref, mod_ref, prew_ref, wqkv_ref, wz_ref, wab_ref, wu_ref, wg_ref, convw_ref,
                   alog_ref, dtb_ref, *out_refs, seg_len, full):
    if full:
        qkvn_ref, gb_ref, u_ref, z_ref, gates_ref = out_refs
    else:
        qkvn_ref, gb_ref, u_ref = out_refs
    tm = x_ref.shape[0]
    mod = mod_ref[0]
    h = _rms(x_ref[...], prew_ref[...]) * (1.0 + mod[1:2, :]) + mod[0:1, :]
    hb = h.astype(BF16)

    qkv = jnp.dot(hb, wqkv_ref[...], preferred_element_type=F32)
    pos = lax.broadcasted_iota(jnp.int32, (tm, 1), 0) % seg_len
    acc = qkv * convw_ref[CONV_K // 2:CONV_K // 2 + 1, :]
    for k in range(CONV_K):
        s = k - CONV_K // 2
        if s == 0:
            continue
        shifted = pltpu.roll(qkv, (-s) % tm, 0)
        ok = (pos + s >= 0) & (pos + s < seg_len)
        acc = acc + jnp.where(ok, shifted, 0.0) * convw_ref[k:k + 1, :]
    act = _silu(acc)
    for j in range(3 * DN_HEADS):
        sl = slice(j * DN_HEAD_DIM, (j + 1) * DN_HEAD_DIM)
        a = act[:, sl]
        if j < 2 * DN_HEADS:
            a = a * lax.rsqrt(jnp.sum(a * a, axis=-1, keepdims=True) + L2_EPS)
        qkvn_ref[:, sl] = a

    ab = jnp.dot(hb, wab_ref[...], preferred_element_type=F32)
    xa = ab + dtb_ref[...]
    softplus = jnp.maximum(xa, 0.0) + jnp.log1p(jnp.exp(-jnp.abs(xa)))
    g = -jnp.exp(alog_ref[...]) * softplus
    lane = lax.broadcasted_iota(jnp.int32, ab.shape, 1)
    gb_ref[...] = jnp.where(lane < 2 * DN_HEADS, g, jax.nn.sigmoid(ab))

    u_ref[...] = jnp.dot(hb, wu_ref[...], preferred_element_type=F32)
    if full:
        z_ref[...] = jnp.dot(hb, wz_ref[...], preferred_element_type=F32)
        gates_ref[...] = jnp.dot(hb, wg_ref[...], preferred_element_type=F32)


def _input_projection(x2d, mod3, mod_row0, mod_row_stride, prew, wts, convw, alog, dtb, *, bsz, seq, seg_len, full):
    wqkv, wz, wab, wu, wg = wts
    tm = TOK_TILE
    nt = seq // tm
    rows = bsz * seq
    row_map = lambda b, i: (b * nt + i, 0)
    const = lambda b, i: (0, 0)
    out_shape = [jax.ShapeDtypeStruct((rows, 3 * DN_WIDTH), F32),
                 jax.ShapeDtypeStruct((rows, LANES), F32),
                 jax.ShapeDtypeStruct((rows, S5_WIDTH), F32)]
    out_specs = [pl.BlockSpec((tm, 3 * DN_WIDTH), row_map),
                 pl.BlockSpec((tm, LANES), row_map),
                 pl.BlockSpec((tm, S5_WIDTH), row_map)]
    if full:
        out_shape += [jax.ShapeDtypeStruct((rows, DN_WIDTH), F32),
                      jax.ShapeDtypeStruct((rows, 2 * D_MODEL), F32)]
        out_specs += [pl.BlockSpec((tm, DN_WIDTH), row_map),
                      pl.BlockSpec((tm, 2 * D_MODEL), row_map)]
    return pl.pallas_call(
        functools.partial(_inproj_kernel, seg_len=seg_len, full=full),
        out_shape=out_shape,
        grid=(bsz, nt),
        in_specs=[pl.BlockSpec((tm, D_MODEL), row_map),
                  pl.BlockSpec((1, N_MOD, D_MODEL), lambda b, i: (mod_row0 + mod_row_stride * b, 0, 0)),
                  pl.BlockSpec((1, D_MODEL), const),
                  pl.BlockSpec(wqkv.shape, const),
                  pl.BlockSpec(wz.shape, const),
                  pl.BlockSpec(wab.shape, const),
                  pl.BlockSpec(wu.shape, const),
                  pl.BlockSpec(wg.shape, const),
                  pl.BlockSpec(convw.shape, const),
                  pl.BlockSpec((1, LANES), const),
                  pl.BlockSpec((1, LANES), const)],
        out_specs=out_specs,
        compiler_params=_cparams(("arbitrary", "arbitrary")),
        name="input_projection_full" if full else "input_projection_ctx",
    )(x2d, mod3, prew, wqkv, wz, wab, wu, wg, convw, alog, dtb)


def _delta_kernel(qf_ref, kf_ref, vf_ref, gbf_ref, qb_ref, kb_ref, vb_ref, gbb_ref, s0_ref,
                  of_ref, ob_ref, sout_ref, state_ref, *, n_chunks):
    c = DN_CHUNK
    i = pl.program_id(1)

    @pl.when(i == 0)
    def _():
        state_ref[...] = s0_ref[0]

    row = lax.broadcasted_iota(jnp.int32, (c, c), 0)
    col = lax.broadcasted_iota(jnp.int32, (c, c), 1)
    eye = (row == col).astype(F32)
    scale = DN_HEAD_DIM ** -0.5
    dirs = ((qf_ref, kf_ref, vf_ref, gbf_ref, of_ref), (qb_ref, kb_ref, vb_ref, gbb_ref, ob_ref))
    chains = []
    for d, (q_ref, k_ref, v_ref, gb_ref, o_ref) in enumerate(dirs):
        incl = (row >= col) if d == 0 else (row <= col)
        strict = (row > col) if d == 0 else (row < col)
        last = c - 1 if d == 0 else 0
        gb = gb_ref[...]
        gcum = jnp.dot(incl.astype(F32), gb, precision=HIGHEST, preferred_element_type=F32)
        gcum_t = gcum.T
        for h in range(DN_HEADS):
            lane = d * DN_HEADS + h
            hs = slice(h * DN_HEAD_DIM, (h + 1) * DN_HEAD_DIM)
            g_col = gcum[:, lane:lane + 1]
            g_row = gcum_t[lane:lane + 1, :]
            g_last = gcum[last:last + 1, lane:lane + 1]
            beta = gb[:, 2 * DN_HEADS + lane:2 * DN_HEADS + lane + 1]
            dec = jnp.where(incl, jnp.exp(jnp.where(incl, g_col - g_row, 0.0)), 0.0)
            q = q_ref[:, hs] * scale
            k = k_ref[:, hs]
            v = v_ref[:, hs]
            k_beta = k * beta
            e_g = jnp.exp(g_col)
            chains.append(dict(lane=lane, hs=hs, o_ref=o_ref, strict=strict, dec=dec, q=q, k=k, k_beta=k_beta,
                               e_g=e_g, rhs=jnp.concatenate([v * beta, k_beta * e_g], axis=1),
                               k_tail=k * jnp.exp(g_last - g_col), e_last=jnp.exp(g_last)))
    for ch in chains:
        ch['npow'] = -jnp.where(ch['strict'], _bdot_nt(ch['k_beta'], ch['k']) * ch['dec'], 0.0)
        ch['attn'] = _bdot_nt(ch['q'], ch['k']) * ch['dec']
        ch['t_inv'] = eye + ch['npow']
    for _ in range(5):
        for ch in chains:
            ch['npow'] = _bdot(ch['npow'], ch['npow'])
        for ch in chains:
            ch['t_inv'] = ch['t_inv'] + _bdot(ch['t_inv'], ch['npow'])
    for ch in chains:
        ch['sol'] = _bdot(ch['t_inv'], ch['rhs'])
    for ch in chains:
        ch['s'] = state_ref[ch['lane']]
        ch['ws'] = _bdot(ch['sol'][:, DN_HEAD_DIM:], ch['s'])
        ch['qs'] = _bdot(ch['q'] * ch['e_g'], ch['s'])
    for ch in chains:
        ch['v_new'] = ch['sol'][:, :DN_HEAD_DIM] - ch['ws']
        ch['o_ref'][:, ch['hs']] = ch['qs'] + _bdot(ch['attn'], ch['v_new'])
    for ch in chains:
        state_ref[ch['lane']] = ch['s'] * ch['e_last'] + _bdot_tn(ch['k_tail'], ch['v_new'])

    @pl.when(i == n_chunks - 1)
    def _():
        sout_ref[0] = state_ref[...]


def _delta_rule(qkvn, gb, s0, *, bsz, seq):
    c = DN_CHUNK
    nc = seq // c
    fwd = lambda j: (lambda b, i: (b * nc + i, j))
    bwd = lambda j: (lambda b, i: (b * nc + nc - 1 - i, j))
    blk = lambda m: pl.BlockSpec((c, DN_WIDTH), m)
    gblk = lambda m: pl.BlockSpec((c, LANES), m)
    st = pl.BlockSpec((1, 2 * DN_HEADS, DN_HEAD_DIM, DN_HEAD_DIM), lambda b, i: (b, 0, 0, 0))
    return pl.pallas_call(
        functools.partial(_delta_kernel, n_chunks=nc),
        out_shape=[jax.ShapeDtypeStruct((bsz * seq, DN_WIDTH), F32),
                   jax.ShapeDtypeStruct((bsz * seq, DN_WIDTH), F32),
                   jax.ShapeDtypeStruct(s0.shape, F32)],
        grid=(bsz, nc),
        in_specs=[blk(fwd(0)), blk(fwd(1)), blk(fwd(2)), gblk(fwd(0)),
                  blk(bwd(0)), blk(bwd(1)), blk(bwd(2)), gblk(bwd(0)), st],
        out_specs=[blk(fwd(0)), blk(bwd(0)), st],
        scratch_shapes=[pltpu.VMEM((2 * DN_HEADS, DN_HEAD_DIM, DN_HEAD_DIM), F32)],
        compiler_params=_cparams(("arbitrary", "arbitrary")),
        name="delta_rule",
    )(qkvn, qkvn, qkvn, gb, qkvn, qkvn, qkvn, gb, s0)


def _s5_disc_kernel(lre_ref, lim_ref, ls_ref, bre_ref, bim_ref, are_ref, aim_ref, bbre_ref, bbim_ref):
    lam_re = lre_ref[...]
    lam_im = lim_ref[...]
    dt = jnp.exp(ls_ref[...])
    mag = jnp.exp(lam_re * dt)
    ang = lam_im * dt
    bar_re = mag * jnp.cos(ang)
    bar_im = mag * jnp.sin(ang)
    den = lam_re * lam_re + lam_im * lam_im
    coef_re = ((bar_re - 1.0) * lam_re + bar_im * lam_im) / den
    coef_im = (bar_im * lam_re - (bar_re - 1.0) * lam_im) / den
    are_ref[...] = bar_re
    aim_ref[...] = bar_im
    bbre_ref[...] = coef_re * bre_ref[...] - coef_im * bim_ref[...]
    bbim_ref[...] = coef_re * bim_ref[...] + coef_im * bre_ref[...]


def _s5_discretise(lam_re, lam_im, log_step, b_re, b_im):
    rep = lambda t: jnp.repeat(t.reshape(2 * S5_GROUPS, -1), S5_GROUP, axis=0)
    rows = 2 * S5_GROUPS * S5_GROUP
    tr = lambda t: jnp.swapaxes(t, -1, -2).reshape(rows, S5_STATE)
    outs = pl.pallas_call(
        _s5_disc_kernel,
        out_shape=[jax.ShapeDtypeStruct((rows, S5_STATE), F32)] * 4,
        name="s5_discretise",
    )(rep(lam_re), rep(lam_im), rep(log_step[..., None]), tr(b_re), tr(b_im))
    shp = (2, S5_GROUPS, S5_GROUP, S5_STATE)
    bar_re, bar_im, bb_re, bb_im = (o.reshape(shp) for o in outs)
    return bar_re[:, :, 0], bar_im[:, :, 0], bb_re, bb_im


def _s5_kernel(uf_ref, ub_ref, wb_ref, wcre_ref, wcim_ref, are_ref, aim_ref, x0_ref, *rest, tt, n_tiles, emit_y):
    if emit_y:
        yf_ref, yb_ref, xout_ref, buf_ref, xst_ref, us_ref, ys_ref = rest
    else:
        xout_ref, buf_ref, xst_ref, us_ref = rest
        yf_ref = yb_ref = ys_ref = None
    i = pl.program_id(0)
    bsz = uf_ref.shape[0]
    hs = S5_HALF_STATE
    n_lane_tiles = S5_WIDTH // LANES
    blk = 512

    @pl.when(i == 0)
    def _():
        xst_ref[...] = x0_ref[...]

    for d, (u_ref, y_ref) in enumerate(((uf_ref, yf_ref), (ub_ref, yb_ref))):
        for b in range(bsz):
            for j in range(n_lane_tiles):
                us_ref[j, pl.ds(b, tt, stride=bsz), :] = u_ref[b, :, j * LANES:(j + 1) * LANES]
        for hh in range(2):
            u_h = jnp.concatenate([us_ref[2 * hh], us_ref[2 * hh + 1]], axis=1)
            buf_ref[d, :, hh * 2 * hs:(hh + 1) * 2 * hs] = _bdot(u_h, wb_ref[d, hh])
        for hh in range(2):
            for p in range(hs // blk):
                lr = hh * 2 * hs + p * blk
                li = lr + hs
                a_re = are_ref[d, :, lr:lr + blk]
                a_im = aim_ref[d, :, lr:lr + blk]

                def body(t, carry, d=d, lr=lr, li=li, a_re=a_re, a_im=a_im):
                    x_re, x_im = carry
                    step = t if d == 0 else tt - 1 - t
                    r0 = pl.multiple_of(step * SUBLANES, SUBLANES)
                    n_re = a_re * x_re - a_im * x_im + buf_ref[d, pl.ds(r0, SUBLANES), lr:lr + blk]
                    n_im = a_re * x_im + a_im * x_re + buf_ref[d, pl.ds(r0, SUBLANES), li:li + blk]
                    buf_ref[d, pl.ds(r0, SUBLANES), lr:lr + blk] = n_re
                    buf_ref[d, pl.ds(r0, SUBLANES), li:li + blk] = n_im
                    return n_re, n_im

                x_re, x_im = lax.fori_loop(0, tt, body, (xst_ref[d, :, lr:lr + blk], xst_ref[d, :, li:li + blk]),
                                           unroll=8)
                xst_ref[d, :, lr:lr + blk] = x_re
                xst_ref[d, :, li:li + blk] = x_im
        if emit_y:
            for hh in range(2):
                x_re = buf_ref[d, :, hh * 2 * hs:hh * 2 * hs + hs]
                x_im = buf_ref[d, :, hh * 2 * hs + hs:(hh + 1) * 2 * hs]
                y_h = _bdot(x_re, wcre_ref[d, hh]) - _bdot(x_im, wcim_ref[d, hh])
                ys_ref[2 * hh] = y_h[:, :LANES]
                ys_ref[2 * hh + 1] = y_h[:, LANES:]
            for b in range(bsz):
                for j in range(n_lane_tiles):
                    y_ref[b, :, j * LANES:(j + 1) * LANES] = ys_ref[j, pl.ds(b, tt, stride=bsz), :]

    @pl.when(i == n_tiles - 1)
    def _():
        xout_ref[...] = xst_ref[...]


def _s5_scan(u3, s5w, x0, *, emit_y):
    wb, wc_re, wc_im, a_re, a_im = s5w
    bsz, seq, _ = u3.shape
    assert bsz == SUBLANES, "one time step of all batches must fill one sublane group"
    tt = S5_TILE
    nt = seq // tt
    rows = tt * bsz
    full = lambda a: pl.BlockSpec(a.shape, lambda i: (0,) * a.ndim)
    ublk = lambda m: pl.BlockSpec((bsz, tt, S5_WIDTH), m)
    fwd = lambda i: (0, i, 0)
    bwd = lambda i: (0, nt - 1 - i, 0)
    out_shape = [jax.ShapeDtypeStruct(x0.shape, F32)]
    out_specs = [full(x0)]
    scratch = [pltpu.VMEM((2, rows, S5_LANES), F32), pltpu.VMEM(x0.shape, F32),
               pltpu.VMEM((S5_WIDTH // LANES, rows, LANES), F32)]
    if emit_y:
        out_shape = [jax.ShapeDtypeStruct(u3.shape, F32)] * 2 + out_shape
        out_specs = [ublk(fwd), ublk(bwd)] + out_specs
        scratch.append(pltpu.VMEM((S5_WIDTH // LANES, rows, LANES), F32))
    return pl.pallas_call(
        functools.partial(_s5_kernel, tt=tt, n_tiles=nt, emit_y=emit_y),
        out_shape=out_shape,
        grid=(nt,),
        in_specs=[ublk(fwd), ublk(bwd), full(wb), full(wc_re), full(wc_im), full(a_re), full(a_im), full(x0)],
        out_specs=out_specs,
        scratch_shapes=scratch,
        compiler_params=_cparams(("arbitrary",)),
        name="s5_scan_y" if emit_y else "s5_scan_state",
    )(u3, u3, wb, wc_re, wc_im, a_re, a_im, x0)


def _s5_weights(bar_re, bar_im, bb_re, bb_im, c_re, c_im, bsz):
    hg = S5_HALF_GROUPS
    eye = jnp.eye(hg, dtype=F32)

    def in_mat(bb):
        t = bb.reshape(2, 2, hg, S5_GROUP, S5_STATE)
        t = jnp.einsum('dhicp,ij->dhicjp', t, eye)
        return t.reshape(2, 2, hg * S5_GROUP, hg * S5_STATE)

    def out_mat(cc):
        t = cc.reshape(2, 2, hg, S5_GROUP, S5_STATE)
        t = jnp.einsum('dhicp,ij->dhipjc', t, eye)
        return t.reshape(2, 2, hg * S5_STATE, hg * S5_GROUP)

    wb = jnp.concatenate([in_mat(bb_re), in_mat(bb_im)], axis=-1).astype(BF16)
    wc_re = out_mat(c_re).astype(BF16)
    wc_im = out_mat(c_im).astype(BF16)

    def lanes(a):
        t = a.reshape(2, 2, 1, S5_HALF_STATE)
        t = jnp.broadcast_to(t, (2, 2, 2, S5_HALF_STATE)).reshape(2, 1, S5_LANES)
        return jnp.broadcast_to(t, (2, bsz, S5_LANES))

    return wb, wc_re, wc_im, lanes(bar_re), lanes(bar_im)


def _merge_kernel(x_ref, mod_ref, of_ref, ob_ref, z_ref, yf_ref, yb_ref, u_ref, gates_ref, onw_ref, waup_ref,
                  s5d_ref, wglu_ref, wout_ref, postw_ref, prew_ref, wr_ref, br_ref,
                  x1_ref, h2_ref, route_ref):
    mod = mod_ref[0]
    o_a = of_ref[...] + ob_ref[...]
    z = z_ref[...]
    parts = []
    for h in range(DN_HEADS):
        hs = slice(h * DN_HEAD_DIM, (h + 1) * DN_HEAD_DIM)
        parts.append(_rms(o_a[:, hs], onw_ref[...]) * _silu(z[:, hs]))
    y_a = _bdot(jnp.concatenate(parts, axis=1), waup_ref[...])
    y5 = yf_ref[...] + yb_ref[...] + s5d_ref[...] * u_ref[...]
    glu = _bdot(jax.nn.gelu(y5, approximate=True), wglu_ref[...])
    y_b = glu[:, :D_MODEL] * jax.nn.sigmoid(glu[:, D_MODEL:])
    gates = gates_ref[...]
    y = jax.nn.sigmoid(gates[:, :D_MODEL]) * y_a + jax.nn.sigmoid(gates[:, D_MODEL:]) * y_b
    mixed = _bdot(y, wout_ref[...])
    x1 = x_ref[...] + mod[2:3, :] * _rms(mixed, postw_ref[...])
    x1_ref[...] = x1
    h2 = _rms(x1, prew_ref[...]) * (1.0 + mod[4:5, :]) + mod[3:4, :]
    h2_ref[...] = h2

    logits = jnp.dot(h2, wr_ref[...], precision=HIGHEST, preferred_element_type=F32) + br_ref[...]
    lane = lax.broadcasted_iota(jnp.int32, logits.shape, 1)
    neg = -jnp.inf
    big = jnp.int32(LANES)
    first = lambda hit: jnp.min(jnp.where(hit, lane, big), axis=-1, keepdims=True)
    g_logit = jnp.where(lane < N_EXPERT_GROUPS, logits, neg)
    g_max = jnp.max(g_logit, axis=-1, keepdims=True)
    g_sel = first(g_logit == g_max)
    g_w = 1.0 / jnp.sum(jnp.exp(g_logit - g_max), axis=-1, keepdims=True)
    e_idx = lane - N_EXPERT_GROUPS
    in_group = (e_idx >= 0) & (e_idx < N_EXPERTS) & ((e_idx // EXPERTS_PER_GROUP) == g_sel)
    e_logit = jnp.where(in_group, logits, neg)
    m1 = jnp.max(e_logit, axis=-1, keepdims=True)
    i1 = first(e_logit == m1)
    e_logit2 = jnp.where(lane == i1, neg, e_logit)
    m2 = jnp.max(e_logit2, axis=-1, keepdims=True)
    i2 = first(e_logit2 == m2)
    e2 = jnp.exp(m2 - m1)
    w1 = g_w / (1.0 + e2)
    w2 = g_w * e2 / (1.0 + e2)
    id1 = (i1 - N_EXPERT_GROUPS).astype(F32)
    id2 = (i2 - N_EXPERT_GROUPS).astype(F32)
    route_ref[...] = jnp.where(lane == 0, id1, jnp.where(lane == 1, id2, jnp.where(lane == 2, w1,
                               jnp.where(lane == 3, w2, 0.0))))


def _merge(x2d, mod3, o_f, o_b, z, y_f, y_b, u, gates, wts, *, bsz, seq):
    onw, waup, s5d, wglu, wout, postw, prew, wr, br = wts
    tm = TOK_TILE
    nt = seq // tm
    rows = bsz * seq
    row_map = lambda b, i: (b * nt + i, 0)
    const = lambda b, i: (0, 0)
    cs = lambda a: pl.BlockSpec(a.shape, const)
    return pl.pallas_call(
        _merge_kernel,
        out_shape=[jax.ShapeDtypeStruct((rows, D_MODEL), F32),
                   jax.ShapeDtypeStruct((rows, D_MODEL), F32),
                   jax.ShapeDtypeStruct((rows, LANES), F32)],
        grid=(bsz, nt),
        in_specs=[pl.BlockSpec((tm, D_MODEL), row_map),
                  pl.BlockSpec((1, N_MOD, D_MODEL), lambda b, i: (b, 0, 0)),
                  pl.BlockSpec((tm, DN_WIDTH), row_map),
                  pl.BlockSpec((tm, DN_WIDTH), row_map),
                  pl.BlockSpec((tm, DN_WIDTH), row_map),
                  pl.BlockSpec((tm, S5_WIDTH), row_map),
                  pl.BlockSpec((tm, S5_WIDTH), row_map),
                  pl.BlockSpec((tm, S5_WIDTH), row_map),
                  pl.BlockSpec((tm, 2 * D_MODEL), row_map),
                  cs(onw), cs(waup), cs(s5d), cs(wglu), cs(wout), cs(postw), cs(prew), cs(wr), cs(br)],
        out_specs=[pl.BlockSpec((tm, D_MODEL), row_map),
                   pl.BlockSpec((tm, D_MODEL), row_map),
                   pl.BlockSpec((tm, LANES), row_map)],
        compiler_params=_cparams(("arbitrary", "arbitrary")),
        name="merge_router",
    )(x2d, mod3, o_f, o_b, z, y_f, y_b, u, gates, onw, waup, s5d, wglu, wout, postw, prew, wr, br)


def _rank_kernel(route_ref, rank_ref, count_ref, carry_ref):
    i = pl.program_id(0)

    @pl.when(i == 0)
    def _():
        carry_ref[...] = jnp.zeros_like(carry_ref)

    route = route_ref[...]
    r = route.shape[0]
    lane = lax.broadcasted_iota(jnp.int32, route.shape, 1).astype(F32)
    oh0 = (lane == route[:, 0:1]).astype(F32)
    oh1 = (lane == route[:, 1:2]).astype(F32)
    both = oh0 + oh1
    row = lax.broadcasted_iota(jnp.int32, (r, r), 0)
    col = lax.broadcasted_iota(jnp.int32, (r, r), 1)
    before = _bdot((row > col).astype(F32), both) + carry_ref[...]
    rank0 = jnp.sum(oh0 * before, axis=-1, keepdims=True)
    rank1 = jnp.sum(oh1 * before, axis=-1, keepdims=True)
    lane_i = lax.broadcasted_iota(jnp.int32, route.shape, 1)
    rank_ref[...] = jnp.where(lane_i == 0, rank0, jnp.where(lane_i == 1, rank1, 0.0))
    carry_ref[...] = carry_ref[...] + jnp.sum(both, axis=0, keepdims=True)
    count_ref[...] = carry_ref[...]


def _expert_ranks(route):
    rows = route.shape[0]
    r = RANK_TILE
    return pl.pallas_call(
        _rank_kernel,
        out_shape=[jax.ShapeDtypeStruct((rows, LANES), F32), jax.ShapeDtypeStruct((1, LANES), F32)],
        grid=(rows // r,),
        in_specs=[pl.BlockSpec((r, LANES), lambda i: (i, 0))],
        out_specs=[pl.BlockSpec((r, LANES), lambda i: (i, 0)), pl.BlockSpec((1, LANES), lambda i: (0, 0))],
        scratch_shapes=[pltpu.VMEM((1, LANES), F32)],
        compiler_params=_cparams(("arbitrary",)),
        name="expert_ranks",
    )(route)


def _row_copy(src_ref, src_row, dst_ref, dst_row, sem):
    return pltpu.make_async_copy(src_ref.at[pl.ds(src_row, 1), :], dst_ref.at[pl.ds(dst_row, 1), :], sem)


def _row_token_kernel(pos_ref, inv_ref, *, n_rows):
    i = pl.program_id(0)
    tm = pos_ref.shape[-1] // 2
    base = i * tm

    @pl.when(i == 0)
    def _():
        def clear(r, _):
            inv_ref[r // LANES, r % LANES] = 0
            return 0

        lax.fori_loop(0, n_rows, clear, 0, unroll=8)

    def body(t, _):
        for s in range(2):
            p = pos_ref[0, 0, 2 * t + s]
            inv_ref[p // LANES, p % LANES] = base + t
        return 0

    lax.fori_loop(0, tm, body, 0, unroll=8)


def _row_tokens(pos3, n_rows):
    n_tiles, _, two_tm = pos3.shape
    return pl.pallas_call(
        functools.partial(_row_token_kernel, n_rows=n_rows),
        out_shape=jax.ShapeDtypeStruct((n_rows // LANES, LANES), jnp.int32),
        grid=(n_tiles,),
        in_specs=[pl.BlockSpec((1, 1, two_tm), lambda i: (i, 0, 0), memory_space=pltpu.SMEM)],
        out_specs=pl.BlockSpec(memory_space=pltpu.SMEM),
        compiler_params=_cparams(("arbitrary",)),
        name="row_tokens",
    )(pos3)


def _expert_kernel(te_ref, tv_ref, inv_ref, invn_ref, h2_ref, wg_ref, wu_ref, wd_ref, ys_ref, xbuf_ref, sems,
                   *, n_tiles):
    del te_ref
    i = pl.program_id(0)
    te = ys_ref.shape[0]
    slot = i % 2

    def issue(idx_ref, s):
        def start(r, _):
            _row_copy(h2_ref, idx_ref[0, 0, r], xbuf_ref.at[s], r, sems.at[s]).start()
            return 0

        lax.fori_loop(0, te, start, 0, unroll=8)

    @pl.when(i == 0)
    def _():
        issue(inv_ref, 0)

    @pl.when(i + 1 < n_tiles)
    def _():
        issue(invn_ref, 1 - slot)

    def wait(r, _):
        _row_copy(h2_ref, 0, xbuf_ref.at[slot], 0, sems.at[slot]).wait()
        return 0

    lax.fori_loop(0, te, wait, 0, unroll=8)

    @pl.when(tv_ref[i] > 0)
    def _():
        x = xbuf_ref[slot].astype(BF16)
        hg = jnp.dot(x, wg_ref[0], preferred_element_type=F32)
        hu = jnp.dot(x, wu_ref[0], preferred_element_type=F32)
        ys_ref[...] = _bdot(_silu(hg) * hu, wd_ref[0])

    @pl.when(tv_ref[i] == 0)
    def _():
        ys_ref[...] = jnp.zeros_like(ys_ref)


def _expert_mlp(tile_expert, tile_valid, inv3, h2, wg, wu, wd):
    n_tiles, _, te = inv3.shape
    smem_blk = lambda m: pl.BlockSpec((1, 1, te), m, memory_space=pltpu.SMEM)
    return pl.pallas_call(
        functools.partial(_expert_kernel, n_tiles=n_tiles),
        out_shape=jax.ShapeDtypeStruct((n_tiles * te, D_MODEL), F32),
        grid_spec=pltpu.PrefetchScalarGridSpec(
            num_scalar_prefetch=2,
            grid=(n_tiles,),
            in_specs=[smem_blk(lambda i, e, v: (i, 0, 0)),
                      smem_blk(lambda i, e, v: (jnp.minimum(i + 1, n_tiles - 1), 0, 0)),
                      pl.BlockSpec(memory_space=pl.ANY),
                      pl.BlockSpec((1, D_MODEL, EXPERT_FF), lambda i, e, v: (e[i], 0, 0)),
                      pl.BlockSpec((1, D_MODEL, EXPERT_FF), lambda i, e, v: (e[i], 0, 0)),
                      pl.BlockSpec((1, EXPERT_FF, D_MODEL), lambda i, e, v: (e[i], 0, 0))],
            out_specs=pl.BlockSpec((te, D_MODEL), lambda i, e, v: (i, 0)),
            scratch_shapes=[pltpu.VMEM((2, te, D_MODEL), F32), pltpu.SemaphoreType.DMA((2,))]),
        compiler_params=_cparams(("arbitrary",)),
        name="expert_mlp",
    )(tile_expert, tile_valid, inv3, inv3, h2, wg, wu, wd)


def _combine_kernel(pos_ref, posn_ref, x1_ref, mod_ref, route_ref, postw_ref, ys_ref, out_ref, buf_ref, sems,
                    *, n_tiles):
    tm = x1_ref.shape[0]
    i = pl.program_id(0)
    slot = i % 2

    def issue(p_ref, s):
        def start(t, _):
            _row_copy(ys_ref, p_ref[0, 0, 2 * t], buf_ref.at[s, 0], t, sems.at[s]).start()
            _row_copy(ys_ref, p_ref[0, 0, 2 * t + 1], buf_ref.at[s, 1], t, sems.at[s]).start()
            return 0

        lax.fori_loop(0, tm, start, 0, unroll=8)

    @pl.when(i == 0)
    def _():
        issue(pos_ref, 0)

    @pl.when(i + 1 < n_tiles)
    def _():
        issue(posn_ref, 1 - slot)

    def wait(t, _):
        _row_copy(ys_ref, 0, buf_ref.at[slot, 0], 0, sems.at[slot]).wait()
        _row_copy(ys_ref, 0, buf_ref.at[slot, 1], 0, sems.at[slot]).wait()
        return 0

    lax.fori_loop(0, tm, wait, 0, unroll=8)
    route = route_ref[...]
    moe = route[:, 2:3] * buf_ref[slot, 0] + route[:, 3:4] * buf_ref[slot, 1]
    out_ref[...] = x1_ref[...] + mod_ref[0][5:6, :] * _rms(moe, postw_ref[...])


def _combine(pos3, x1, mod3, route, postw, ys, *, bsz, seq):
    tm = TOK_TILE
    nt = seq // tm
    n_tiles = bsz * nt
    row_map = lambda i: (i, 0)
    return pl.pallas_call(
        functools.partial(_combine_kernel, n_tiles=n_tiles),
        out_shape=jax.ShapeDtypeStruct(x1.shape, F32),
        grid=(n_tiles,),
        in_specs=[pl.BlockSpec((1, 1, 2 * tm), lambda i: (i, 0, 0), memory_space=pltpu.SMEM),
                  pl.BlockSpec((1, 1, 2 * tm), lambda i: (jnp.minimum(i + 1, n_tiles - 1), 0, 0),
                               memory_space=pltpu.SMEM),
                  pl.BlockSpec((tm, D_MODEL), row_map),
                  pl.BlockSpec((1, N_MOD, D_MODEL), lambda i: (i // nt, 0, 0)),
                  pl.BlockSpec((tm, LANES), row_map),
                  pl.BlockSpec((1, D_MODEL), lambda i: (0, 0)),
                  pl.BlockSpec(memory_space=pl.ANY)],
        out_specs=pl.BlockSpec((tm, D_MODEL), row_map),
        scratch_shapes=[pltpu.VMEM((2, 2, tm, D_MODEL), F32), pltpu.SemaphoreType.DMA((2,))],
        compiler_params=_cparams(("arbitrary",)),
        name="combine",
    )(pos3, pos3, x1, mod3, route, postw, ys)


def _lane_pad(v, n=LANES):
    v = v.reshape(1, -1)
    return jnp.pad(v, ((0, 0), (0, n - v.shape[1])))


def kernel(x, c, ctx, c_ctx, w_mod, b_mod, pre_norm_w, post_norm_w, w_in, conv_w, a_log, dt_bias, o_norm_w,
           w_a_up, s5_lam_re, s5_lam_im, s5_log_step, s5_b_re, s5_b_im, s5_c_re, s5_c_im, s5_d, w_glu, w_out,
           w_router_group, b_router_group, w_router_expert, b_router_expert, w_gate_e, w_up_e, w_down_e):
    assert w_mod.shape[0] == 1, "single-layer block"
    bsz, seq, _ = x.shape
    ctx_len = ctx.shape[1]
    row = lambda v: v.reshape(1, -1)

    c16 = jnp.concatenate([c, c_ctx[None, :], jnp.zeros((2 * SUBLANES - bsz - 1, D_MODEL), F32)], axis=0)
    mod3 = _modulation(c16, w_mod[0], row(b_mod[0])).reshape(2 * SUBLANES, N_MOD, D_MODEL)

    wi = w_in[0]
    o_z, o_a, o_u, o_g = 3 * DN_WIDTH, 4 * DN_WIDTH, 4 * DN_WIDTH + 4 * DN_HEADS, 4 * DN_WIDTH + 4 * DN_HEADS + S5_WIDTH
    wts_in = (wi[:, :o_z].astype(BF16), wi[:, o_z:o_a].astype(BF16),
              jnp.pad(wi[:, o_a:o_u], ((0, 0), (0, LANES - 4 * DN_HEADS))).astype(BF16),
              wi[:, o_u:o_g].astype(BF16), wi[:, o_g:].astype(BF16))
    convw = jnp.pad(conv_w[0], ((0, SUBLANES - CONV_K), (0, 0)))
    alog = _lane_pad(a_log[0])
    dtb = _lane_pad(dt_bias[0])
    prew0, prew1 = row(pre_norm_w[0, 0]), row(pre_norm_w[0, 1])
    postw0, postw1 = row(post_norm_w[0, 0]), row(post_norm_w[0, 1])

    bar_re, bar_im, bb_re, bb_im = _s5_discretise(s5_lam_re[0], s5_lam_im[0], s5_log_step[0], s5_b_re[0], s5_b_im[0])
    s5w = _s5_weights(bar_re, bar_im, bb_re, bb_im, s5_c_re[0], s5_c_im[0], bsz)

    qkvn_c, gb_c, u_c = _input_projection(ctx.reshape(bsz * ctx_len, D_MODEL), mod3, bsz, 0, prew0, wts_in, convw,
                                          alog, dtb, bsz=bsz, seq=ctx_len, seg_len=ctx_len, full=False)
    s_zero = jnp.zeros((bsz, 2 * DN_HEADS, DN_HEAD_DIM, DN_HEAD_DIM), F32)
    _, _, s_ctx = _delta_rule(qkvn_c, gb_c, s_zero, bsz=bsz, seq=ctx_len)
    x_zero = jnp.zeros((2, bsz, S5_LANES), F32)
    (x_ctx,) = _s5_scan(u_c.reshape(bsz, ctx_len, S5_WIDTH), s5w, x_zero, emit_y=False)

    x2d = x.reshape(bsz * seq, D_MODEL)
    qkvn, gb, u, z, gates = _input_projection(x2d, mod3, 0, 1, prew0, wts_in, convw, alog, dtb,
                                                 bsz=bsz, seq=seq, seg_len=GRID_W, full=True)
    o_f, o_b, _ = _delta_rule(qkvn, gb, s_ctx, bsz=bsz, seq=seq)
    y_f, y_b, _ = _s5_scan(u.reshape(bsz, seq, S5_WIDTH), s5w, x_ctx, emit_y=True)
    rows2d = lambda a: a.reshape(bsz * seq, S5_WIDTH)

    w_router = jnp.pad(jnp.concatenate([w_router_group[0], w_router_expert[0]], axis=1),
                       ((0, 0), (0, LANES - N_EXPERT_GROUPS - N_EXPERTS)))
    b_router = _lane_pad(jnp.concatenate([b_router_group[0], b_router_expert[0]]))
    wts_merge = (row(o_norm_w[0]), w_a_up[0].astype(BF16), row(s5_d[0]), w_glu[0].astype(BF16),
                 w_out[0].astype(BF16), postw0, prew1, w_router, b_router)
    x1, h2, route = _merge(x2d, mod3, o_f, o_b, z, rows2d(y_f), rows2d(y_b), u, gates, wts_merge, bsz=bsz, seq=seq)

    rank, count = _expert_ranks(route)
    n_tok = bsz * seq
    te = EXPERT_TILE
    counts = count[0, :N_EXPERTS].astype(jnp.int32)
    padded = ((counts + te - 1) // te) * te
    ends = jnp.cumsum(padded)
    starts = ends - padded
    ids = route[:, :2].astype(jnp.int32)
    one_hot = ids[:, :, None] == jnp.arange(N_EXPERTS, dtype=jnp.int32)
    pos = jnp.sum(jnp.where(one_hot, starts, 0), axis=-1) + rank[:, :2].astype(jnp.int32)
    pos3 = pos.reshape(n_tok // TOK_TILE, 1, 2 * TOK_TILE)
    n_sorted = 2 * n_tok + N_EXPERTS * te
    tile_start = jnp.arange(n_sorted // te, dtype=jnp.int32) * te
    tile_valid = (tile_start < ends[-1]).astype(jnp.int32)
    tile_expert = jnp.minimum(jnp.sum((ends[None, :] <= tile_start[:, None]).astype(jnp.int32), axis=1), N_EXPERTS - 1)

    inv3 = _row_tokens(pos3, n_sorted).reshape(n_sorted // te, 1, te)
    ys = _expert_mlp(tile_expert, tile_valid, inv3, h2, w_gate_e[0].astype(BF16), w_up_e[0].astype(BF16),
                     w_down_e[0].astype(BF16))
    out = _combine(pos3, x1, mod3, route, postw1, ys, bsz=bsz, seq=seq)
    return out.reshape(bsz, seq, D_MODEL)
```

```python
import functools

import jax
import jax.numpy as jnp
from jax import lax
from jax.experimental import pallas as pl
from jax.experimental.pallas import tpu as pltpu

F32 = jnp.float32
BF16 = jnp.bfloat16
HIGHEST = lax.Precision.HIGHEST

D_MODEL = 1024
GRID_W = 64
DN_HEADS = 4
DN_HEAD_DIM = 128
DN_WIDTH = DN_HEADS * DN_HEAD_DIM
DN_CHUNK = 64
CONV_K = 5
S5_WIDTH = D_MODEL - DN_WIDTH
S5_GROUP = 16
S5_GROUPS = S5_WIDTH // S5_GROUP
S5_STATE = 64
S5_HALF_GROUPS = S5_GROUPS // 2
S5_HALF_STATE = S5_HALF_GROUPS * S5_STATE
S5_LANES = 2 * 2 * S5_HALF_STATE
N_EXPERT_GROUPS = 4
EXPERTS_PER_GROUP = 8
N_EXPERTS = N_EXPERT_GROUPS * EXPERTS_PER_GROUP
EXPERT_FF = 512
NORM_EPS = 1e-6
L2_EPS = 1e-6
N_MOD = 6
LANES = 128
SUBLANES = 8
VMEM_LIMIT = 56 * 1024 * 1024

TOK_TILE = 256
S5_TILE = 64
EXPERT_TILE = 256
RANK_TILE = 512


def _cparams(sem):
    return pltpu.CompilerParams(dimension_semantics=sem, vmem_limit_bytes=VMEM_LIMIT)


def _bdot(a, b):
    return jnp.dot(a.astype(BF16), b.astype(BF16), preferred_element_type=F32)


def _bdot_nt(a, b):
    return lax.dot_general(a.astype(BF16), b.astype(BF16), (((1,), (1,)), ((), ())),
                           preferred_element_type=F32)


def _bdot_tn(a, b):
    return lax.dot_general(a.astype(BF16), b.astype(BF16), (((0,), (0,)), ((), ())),
                           preferred_element_type=F32)


def _silu(x):
    return x * jax.nn.sigmoid(x)


def _rms(x, w):
    return x * lax.rsqrt(jnp.mean(x * x, axis=-1, keepdims=True) + NORM_EPS) * w


def _mod_kernel(c_ref, w_ref, b_ref, o_ref):
    o_ref[...] = jnp.dot(_silu(c_ref[...]), w_ref[...], precision=HIGHEST,
                         preferred_element_type=F32) + b_ref[...]


def _modulation(c16, w_mod, b_mod):
    n = w_mod.shape[1]
    return pl.pallas_call(
        _mod_kernel,
        out_shape=jax.ShapeDtypeStruct((c16.shape[0], n), F32),
        grid=(n // D_MODEL,),
        in_specs=[pl.BlockSpec(c16.shape, lambda j: (0, 0)),
                  pl.BlockSpec((D_MODEL, D_MODEL), lambda j: (0, j)),
                  pl.BlockSpec((1, D_MODEL), lambda j: (0, j))],
        out_specs=pl.BlockSpec((c16.shape[0], D_MODEL), lambda j: (0, j)),
        compiler_params=_cparams(("arbitrary",)),
        name="modulation",
    )(c16, w_mod, b_mod)


def _inproj_kernel(x_ref, mod_ref, prew_ref, wqkv_ref, wz_ref, wab_ref, wu_ref, wg_ref, convw_ref,
                   alog_ref, dtb_ref, *out_refs, seg_len, full):
    if full:
        qkvn_ref, gb_ref, u_ref, z_ref, gates_ref = out_refs
    else:
        qkvn_ref, gb_ref, u_ref = out_refs
    tm = x_ref.shape[0]
    mod = mod_ref[0]
    h = _rms(x_ref[...], prew_ref[...]) * (1.0 + mod[1:2, :]) + mod[0:1, :]
    hb = h.astype(BF16)

    qkv = jnp.dot(hb, wqkv_ref[...], preferred_element_type=F32)
    pos = lax.broadcasted_iota(jnp.int32, (tm, 1), 0) % seg_len
    acc = qkv * convw_ref[CONV_K // 2:CONV_K // 2 + 1, :]
    for k in range(CONV_K):
        s = k - CONV_K // 2
        if s == 0:
            continue
        shifted = pltpu.roll(qkv, (-s) % tm, 0)
        ok = (pos + s >= 0) & (pos + s < seg_len)
        acc = acc + jnp.where(ok, shifted, 0.0) * convw_ref[k:k + 1, :]
    act = _silu(acc)
    for j in range(3 * DN_HEADS):
        sl = slice(j * DN_HEAD_DIM, (j + 1) * DN_HEAD_DIM)
        a = act[:, sl]
        if j < 2 * DN_HEADS:
            a = a * lax.rsqrt(jnp.sum(a * a, axis=-1, keepdims=True) + L2_EPS)
        qkvn_ref[:, sl] = a

    ab = jnp.dot(hb, wab_ref[...], preferred_element_type=F32)
    xa = ab + dtb_ref[...]
    softplus = jnp.maximum(xa, 0.0) + jnp.log1p(jnp.exp(-jnp.abs(xa)))
    g = -jnp.exp(alog_ref[...]) * softplus
    lane = lax.broadcasted_iota(jnp.int32, ab.shape, 1)
    gb_ref[...] = jnp.where(lane < 2 * DN_HEADS, g, jax.nn.sigmoid(ab))

    u_ref[...] = jnp.dot(hb, wu_ref[...], preferred_element_type=F32)
    if full:
        z_ref[...] = jnp.dot(hb, wz_ref[...], preferred_element_type=F32)
        gates_ref[...] = jnp.dot(hb, wg_ref[...], preferred_element_type=F32)


def _input_projection(x2d, mod3, mod_row0, mod_row_stride, prew, wts, convw, alog, dtb, *, bsz, seq, seg_len, full):
    wqkv, wz, wab, wu, wg = wts
    tm = TOK_TILE
    nt = seq // tm
    rows = bsz * seq
    row_map = lambda b, i: (b * nt + i, 0)
    const = lambda b, i: (0, 0)
    out_shape = [jax.ShapeDtypeStruct((rows, 3 * DN_WIDTH), F32),
                 jax.ShapeDtypeStruct((rows, LANES), F32),
                 jax.ShapeDtypeStruct((rows, S5_WIDTH), F32)]
    out_specs = [pl.BlockSpec((tm, 3 * DN_WIDTH), row_map),
                 pl.BlockSpec((tm, LANES), row_map),
                 pl.BlockSpec((tm, S5_WIDTH), row_map)]
    if full:
        out_shape += [jax.ShapeDtypeStruct((rows, DN_WIDTH), F32),
                      jax.ShapeDtypeStruct((rows, 2 * D_MODEL), F32)]
        out_specs += [pl.BlockSpec((tm, DN_WIDTH), row_map),
                      pl.BlockSpec((tm, 2 * D_MODEL), row_map)]
    return pl.pallas_call(
        functools.partial(_inproj_kernel, seg_len=seg_len, full=full),
        out_shape=out_shape,
        grid=(bsz, nt),
        in_specs=[pl.BlockSpec((tm, D_MODEL), row_map),
                  pl.BlockSpec((1, N_MOD, D_MODEL), lambda b, i: (mod_row0 + mod_row_stride * b, 0, 0)),
                  pl.BlockSpec((1, D_MODEL), const),
                  pl.BlockSpec(wqkv.shape, const),
                  pl.BlockSpec(wz.shape, const),
                  pl.BlockSpec(wab.shape, const),
                  pl.BlockSpec(wu.shape, const),
                  pl.BlockSpec(wg.shape, const),
                  pl.BlockSpec(convw.shape, const),
                  pl.BlockSpec((1, LANES), const),
                  pl.BlockSpec((1, LANES), const)],
        out_specs=out_specs,
        compiler_params=_cparams(("arbitrary", "arbitrary")),
        name="input_projection_full" if full else "input_projection_ctx",
    )(x2d, mod3, prew, wqkv, wz, wab, wu, wg, convw, alog, dtb)


def _delta_kernel(qf_ref, kf_ref, vf_ref, gbf_ref, qb_ref, kb_ref, vb_ref, gbb_ref, s0_ref,
                  of_ref, ob_ref, sout_ref, state_ref, *, n_chunks):
    c = DN_CHUNK
    i = pl.program_id(1)

    @pl.when(i == 0)
    def _():
        state_ref[...] = s0_ref[0]

    row = lax.broadcasted_iota(jnp.int32, (c, c), 0)
    col = lax.broadcasted_iota(jnp.int32, (c, c), 1)
    eye = (row == col).astype(F32)
    scale = DN_HEAD_DIM ** -0.5
    dirs = ((qf_ref, kf_ref, vf_ref, gbf_ref, of_ref), (qb_ref, kb_ref, vb_ref, gbb_ref, ob_ref))
    chains = []
    for d, (q_ref, k_ref, v_ref, gb_ref, o_ref) in enumerate(dirs):
        incl = (row >= col) if d == 0 else (row <= col)
        strict = (row > col) if d == 0 else (row < col)
        last = c - 1 if d == 0 else 0
        gb = gb_ref[...]
        gcum = jnp.dot(incl.astype(F32), gb, precision=HIGHEST, preferred_element_type=F32)
        gcum_t = gcum.T
        for h in range(DN_HEADS):
            lane = d * DN_HEADS + h
            hs = slice(h * DN_HEAD_DIM, (h + 1) * DN_HEAD_DIM)
            g_col = gcum[:, lane:lane + 1]
            g_row = gcum_t[lane:lane + 1, :]
            g_last = gcum[last:last + 1, lane:lane + 1]
            beta = gb[:, 2 * DN_HEADS + lane:2 * DN_HEADS + lane + 1]
            dec = jnp.where(incl, jnp.exp(jnp.where(incl, g_col - g_row, 0.0)), 0.0)
            q = q_ref[:, hs] * scale
            k = k_ref[:, hs]
            v = v_ref[:, hs]
            k_beta = k * beta
            e_g = jnp.exp(g_col)
            chains.append(dict(lane=lane, hs=hs, o_ref=o_ref, strict=strict, dec=dec, q=q, k=k, k_beta=k_beta,
                               e_g=e_g, rhs=jnp.concatenate([v * beta, k_beta * e_g], axis=1),
                               k_tail=k * jnp.exp(g_last - g_col), e_last=jnp.exp(g_last)))
    for ch in chains:
        ch['npow'] = -jnp.where(ch['strict'], _bdot_nt(ch['k_beta'], ch['k']) * ch['dec'], 0.0)
        ch['attn'] = _bdot_nt(ch['q'], ch['k']) * ch['dec']
        ch['t_inv'] = eye + ch['npow']
    for _ in range(5):
        for ch in chains:
            ch['npow'] = _bdot(ch['npow'], ch['npow'])
        for ch in chains:
            ch['t_inv'] = ch['t_inv'] + _bdot(ch['t_inv'], ch['npow'])
    for ch in chains:
        ch['sol'] = _bdot(ch['t_inv'], ch['rhs'])
    for ch in chains:
        ch['s'] = state_ref[ch['lane']]
        ch['ws'] = _bdot(ch['sol'][:, DN_HEAD_DIM:], ch['s'])
        ch['qs'] = _bdot(ch['q'] * ch['e_g'], ch['s'])
    for ch in chains:
        ch['v_new'] = ch['sol'][:, :DN_HEAD_DIM] - ch['ws']
        ch['o_ref'][:, ch['hs']] = ch['qs'] + _bdot(ch['attn'], ch['v_new'])
    for ch in chains:
        state_ref[ch['lane']] = ch['s'] * ch['e_last'] + _bdot_tn(ch['k_tail'], ch['v_new'])

    @pl.when(i == n_chunks - 1)
    def _():
        sout_ref[0] = state_ref[...]


def _delta_rule(qkvn, gb, s0, *, bsz, seq):
    c = DN_CHUNK
    nc = seq // c
    fwd = lambda j: (lambda b, i: (b * nc + i, j))
    bwd = lambda j: (lambda b, i: (b * nc + nc - 1 - i, j))
    blk = lambda m: pl.BlockSpec((c, DN_WIDTH), m)
    gblk = lambda m: pl.BlockSpec((c, LANES), m)
    st = pl.BlockSpec((1, 2 * DN_HEADS, DN_HEAD_DIM, DN_HEAD_DIM), lambda b, i: (b, 0, 0, 0))
    return pl.pallas_call(
        functools.partial(_delta_kernel, n_chunks=nc),
        out_shape=[jax.ShapeDtypeStruct((bsz * seq, DN_WIDTH), F32),
                   jax.ShapeDtypeStruct((bsz * seq, DN_WIDTH), F32),
                   jax.ShapeDtypeStruct(s0.shape, F32)],
        grid=(bsz, nc),
        in_specs=[blk(fwd(0)), blk(fwd(1)), blk(fwd(2)), gblk(fwd(0)),
                  blk(bwd(0)), blk(bwd(1)), blk(bwd(2)), gblk(bwd(0)), st],
        out_specs=[blk(fwd(0)), blk(bwd(0)), st],
        scratch_shapes=[pltpu.VMEM((2 * DN_HEADS, DN_HEAD_DIM, DN_HEAD_DIM), F32)],
        compiler_params=_cparams(("arbitrary", "arbitrary")),
        name="delta_rule",
    )(qkvn, qkvn, qkvn, gb, qkvn, qkvn, qkvn, gb, s0)


def _s5_disc_kernel(lre_ref, lim_ref, ls_ref, bre_ref, bim_ref, are_ref, aim_ref, bbre_ref, bbim_ref):
    lam_re = lre_ref[...]
    lam_im = lim_ref[...]
    dt = jnp.exp(ls_ref[...])
    mag = jnp.exp(lam_re * dt)
    ang = lam_im * dt
    bar_re = mag * jnp.cos(ang)
    bar_im = mag * jnp.sin(ang)
    den = lam_re * lam_re + lam_im * lam_im
    coef_re = ((bar_re - 1.0) * lam_re + bar_im * lam_im) / den
    coef_im = (bar_im * lam_re - (bar_re - 1.0) * lam_im) / den
    are_ref[...] = bar_re
    aim_ref[...] = bar_im
    bbre_ref[...] = coef_re * bre_ref[...] - coef_im * bim_ref[...]
    bbim_ref[...] = coef_re * bim_ref[...] + coef_im * bre_ref[...]


def _s5_discretise(lam_re, lam_im, log_step, b_re, b_im):
    rep = lambda t: jnp.repeat(t.reshape(2 * S5_GROUPS, -1), S5_GROUP, axis=0)
    rows = 2 * S5_GROUPS * S5_GROUP
    tr = lambda t: jnp.swapaxes(t, -1, -2).reshape(rows, S5_STATE)
    outs = pl.pallas_call(
        _s5_disc_kernel,
        out_shape=[jax.ShapeDtypeStruct((rows, S5_STATE), F32)] * 4,
        name="s5_discretise",
    )(rep(lam_re), rep(lam_im), rep(log_step[..., None]), tr(b_re), tr(b_im))
    shp = (2, S5_GROUPS, S5_GROUP, S5_STATE)
    bar_re, bar_im, bb_re, bb_im = (o.reshape(shp) for o in outs)
    return bar_re[:, :, 0], bar_im[:, :, 0], bb_re, bb_im


def _s5_kernel(uf_ref, ub_ref, wb_ref, wcre_ref, wcim_ref, are_ref, aim_ref, x0_ref, *rest, tt, n_tiles, emit_y):
    if emit_y:
        yf_ref, yb_ref, xout_ref, buf_ref, xst_ref, us_ref, ys_ref = rest
    else:
        xout_ref, buf_ref, xst_ref, us_ref = rest
        yf_ref = yb_ref = ys_ref = None
    i = pl.program_id(0)
    bsz = uf_ref.shape[0]
    hs = S5_HALF_STATE
    n_lane_tiles = S5_WIDTH // LANES
    blk = 512

    @pl.when(i == 0)
    def _():
        xst_ref[...] = x0_ref[...]

    for d, (u_ref, y_ref) in enumerate(((uf_ref, yf_ref), (ub_ref, yb_ref))):
        for b in range(bsz):
            for j in range(n_lane_tiles):
                us_ref[j, pl.ds(b, tt, stride=bsz), :] = u_ref[b, :, j * LANES:(j + 1) * LANES]
        for hh in range(2):
            u_h = jnp.concatenate([us_ref[2 * hh], us_ref[2 * hh + 1]], axis=1)
            buf_ref[d, :, hh * 2 * hs:(hh + 1) * 2 * hs] = _bdot(u_h, wb_ref[d, hh])
        for hh in range(2):
            for p in range(hs // blk):
                lr = hh * 2 * hs + p * blk
                li = lr + hs
                a_re = are_ref[d, :, lr:lr + blk]
                a_im = aim_ref[d, :, lr:lr + blk]

                def body(t, carry, d=d, lr=lr, li=li, a_re=a_re, a_im=a_im):
                    x_re, x_im = carry
                    step = t if d == 0 else tt - 1 - t
                    r0 = pl.multiple_of(step * SUBLANES, SUBLANES)
                    n_re = a_re * x_re - a_im * x_im + buf_ref[d, pl.ds(r0, SUBLANES), lr:lr + blk]
                    n_im = a_re * x_im + a_im * x_re + buf_ref[d, pl.ds(r0, SUBLANES), li:li + blk]
                    buf_ref[d, pl.ds(r0, SUBLANES), lr:lr + blk] = n_re
                    buf_ref[d, pl.ds(r0, SUBLANES), li:li + blk] = n_im
                    return n_re, n_im

                x_re, x_im = lax.fori_loop(0, tt, body, (xst_ref[d, :, lr:lr + blk], xst_ref[d, :, li:li + blk]),
                                           unroll=8)
                xst_ref[d, :, lr:lr + blk] = x_re
                xst_ref[d, :, li:li + blk] = x_im
        if emit_y:
            for hh in range(2):
                x_re = buf_ref[d, :, hh * 2 * hs:hh * 2 * hs + hs]
                x_im = buf_ref[d, :, hh * 2 * hs + hs:(hh + 1) * 2 * hs]
                y_h = _bdot(x_re, wcre_ref[d, hh]) - _bdot(x_im, wcim_ref[d, hh])
                ys_ref[2 * hh] = y_h[:, :LANES]
                ys_ref[2 * hh + 1] = y_h[:, LANES:]
            for b in range(bsz):
                for j in range(n_lane_tiles):
                    y_ref[b, :, j * LANES:(j + 1) * LANES] = ys_ref[j, pl.ds(b, tt, stride=bsz), :]

    @pl.when(i == n_tiles - 1)
    def _():
        xout_ref[...] = xst_ref[...]


def _s5_scan(u3, s5w, x0, *, emit_y):
    wb, wc_re, wc_im, a_re, a_im = s5w
    bsz, seq, _ = u3.shape
    assert bsz == SUBLANES, "one time step of all batches must fill one sublane group"
    tt = S5_TILE
    nt = seq // tt
    rows = tt * bsz
    full = lambda a: pl.BlockSpec(a.shape, lambda i: (0,) * a.ndim)
    ublk = lambda m: pl.BlockSpec((bsz, tt, S5_WIDTH), m)
    fwd = lambda i: (0, i, 0)
    bwd = lambda i: (0, nt - 1 - i, 0)
    out_shape = [jax.ShapeDtypeStruct(x0.shape, F32)]
    out_specs = [full(x0)]
    scratch = [pltpu.VMEM((2, rows, S5_LANES), F32), pltpu.VMEM(x0.shape, F32),
               pltpu.VMEM((S5_WIDTH // LANES, rows, LANES), F32)]
    if emit_y:
        out_shape = [jax.ShapeDtypeStruct(u3.shape, F32)] * 2 + out_shape
        out_specs = [ublk(fwd), ublk(bwd)] + out_specs
        scratch.append(pltpu.VMEM((S5_WIDTH // LANES, rows, LANES), F32))
    return pl.pallas_call(
        functools.partial(_s5_kernel, tt=tt, n_tiles=nt, emit_y=emit_y),
        out_shape=out_shape,
        grid=(nt,),
        in_specs=[ublk(fwd), ublk(bwd), full(wb), full(wc_re), full(wc_im), full(a_re), full(a_im), full(x0)],
        out_specs=out_specs,
        scratch_shapes=scratch,
        compiler_params=_cparams(("arbitrary",)),
        name="s5_scan_y" if emit_y else "s5_scan_state",
    )(u3, u3, wb, wc_re, wc_im, a_re, a_im, x0)


def _s5_weights(bar_re, bar_im, bb_re, bb_im, c_re, c_im, bsz):
    hg = S5_HALF_GROUPS
    eye = jnp.eye(hg, dtype=F32)

    def in_mat(bb):
        t = bb.reshape(2, 2, hg, S5_GROUP, S5_STATE)
        t = jnp.einsum('dhicp,ij->dhicjp', t, eye)
        return t.reshape(2, 2, hg * S5_GROUP, hg * S5_STATE)

    def out_mat(cc):
        t = cc.reshape(2, 2, hg, S5_GROUP, S5_STATE)
        t = jnp.einsum('dhicp,ij->dhipjc', t, eye)
        return t.reshape(2, 2, hg * S5_STATE, hg * S5_GROUP)

    wb = jnp.concatenate([in_mat(bb_re), in_mat(bb_im)], axis=-1).astype(BF16)
    wc_re = out_mat(c_re).astype(BF16)
    wc_im = out_mat(c_im).astype(BF16)

    def lanes(a):
        t = a.reshape(2, 2, 1, S5_HALF_STATE)
        t = jnp.broadcast_to(t, (2, 2, 2, S5_HALF_STATE)).reshape(2, 1, S5_LANES)
        return jnp.broadcast_to(t, (2, bsz, S5_LANES))

    return wb, wc_re, wc_im, lanes(bar_re), lanes(bar_im)


def _merge_kernel(x_ref, mod_ref, of_ref, ob_ref, z_ref, yf_ref, yb_ref, u_ref, gates_ref, onw_ref, waup_ref,
                  s5d_ref, wglu_ref, wout_ref, postw_ref, prew_ref, wr_ref, br_ref,
                  x1_ref, h2_ref, route_ref):
    mod = mod_ref[0]
    o_a = of_ref[...] + ob_ref[...]
    z = z_ref[...]
    parts = []
    for h in range(DN_HEADS):
        hs = slice(h * DN_HEAD_DIM, (h + 1) * DN_HEAD_DIM)
        parts.append(_rms(o_a[:, hs], onw_ref[...]) * _silu(z[:, hs]))
    y_a = _bdot(jnp.concatenate(parts, axis=1), waup_ref[...])
    y5 = yf_ref[...] + yb_ref[...] + s5d_ref[...] * u_ref[...]
    glu = _bdot(jax.nn.gelu(y5, approximate=True), wglu_ref[...])
    y_b = glu[:, :D_MODEL] * jax.nn.sigmoid(glu[:, D_MODEL:])
    gates = gates_ref[...]
    y = jax.nn.sigmoid(gates[:, :D_MODEL]) * y_a + jax.nn.sigmoid(gates[:, D_MODEL:]) * y_b
    mixed = _bdot(y, wout_ref[...])
    x1 = x_ref[...] + mod[2:3, :] * _rms(mixed, postw_ref[...])
    x1_ref[...] = x1
    h2 = _rms(x1, prew_ref[...]) * (1.0 + mod[4:5, :]) + mod[3:4, :]
    h2_ref[...] = h2

    logits = _bdot(h2, wr_ref[...]) + br_ref[...]
    lane = lax.broadcasted_iota(jnp.int32, logits.shape, 1)
    neg = -jnp.inf
    big = jnp.int32(LANES)
    first = lambda hit: jnp.min(jnp.where(hit, lane, big), axis=-1, keepdims=True)
    g_logit = jnp.where(lane < N_EXPERT_GROUPS, logits, neg)
    g_max = jnp.max(g_logit, axis=-1, keepdims=True)
    g_sel = first(g_logit == g_max)
    g_w = 1.0 / jnp.sum(jnp.exp(g_logit - g_max), axis=-1, keepdims=True)
    e_idx = lane - N_EXPERT_GROUPS
    in_group = (e_idx >= 0) & (e_idx < N_EXPERTS) & ((e_idx // EXPERTS_PER_GROUP) == g_sel)
    e_logit = jnp.where(in_group, logits, neg)
    m1 = jnp.max(e_logit, axis=-1, keepdims=True)
    i1 = first(e_logit == m1)
    e_logit2 = jnp.where(lane == i1, neg, e_logit)
    m2 = jnp.max(e_logit2, axis=-1, keepdims=True)
    i2 = first(e_logit2 == m2)
    e2 = jnp.exp(m2 - m1)
    w1 = g_w / (1.0 + e2)
    w2 = g_w * e2 / (1.0 + e2)
    id1 = (i1 - N_EXPERT_GROUPS).astype(F32)
    id2 = (i2 - N_EXPERT_GROUPS).astype(F32)
    route_ref[...] = jnp.where(lane == 0, id1, jnp.where(lane == 1, id2, jnp.where(lane == 2, w1,
                               jnp.where(lane == 3, w2, 0.0))))


def _merge(x2d, mod3, o_f, o_b, z, y_f, y_b, u, gates, wts, *, bsz, seq):
    onw, waup, s5d, wglu, wout, postw, prew, wr, br = wts
    tm = TOK_TILE
    nt = seq // tm
    rows = bsz * seq
    row_map = lambda b, i: (b * nt + i, 0)
    const = lambda b, i: (0, 0)
    cs = lambda a: pl.BlockSpec(a.shape, const)
    return pl.pallas_call(
        _merge_kernel,
        out_shape=[jax.ShapeDtypeStruct((rows, D_MODEL), F32),
                   jax.ShapeDtypeStruct((rows, D_MODEL), F32),
                   jax.ShapeDtypeStruct((rows, LANES), F32)],
        grid=(bsz, nt),
        in_specs=[pl.BlockSpec((tm, D_MODEL), row_map),
                  pl.BlockSpec((1, N_MOD, D_MODEL), lambda b, i: (b, 0, 0)),
                  pl.BlockSpec((tm, DN_WIDTH), row_map),
                  pl.BlockSpec((tm, DN_WIDTH), row_map),
                  pl.BlockSpec((tm, DN_WIDTH), row_map),
                  pl.BlockSpec((tm, S5_WIDTH), row_map),
                  pl.BlockSpec((tm, S5_WIDTH), row_map),
                  pl.BlockSpec((tm, S5_WIDTH), row_map),
                  pl.BlockSpec((tm, 2 * D_MODEL), row_map),
                  cs(onw), cs(waup), cs(s5d), cs(wglu), cs(wout), cs(postw), cs(prew), cs(wr), cs(br)],
        out_specs=[pl.BlockSpec((tm, D_MODEL), row_map),
                   pl.BlockSpec((tm, D_MODEL), row_map),
                   pl.BlockSpec((tm, LANES), row_map)],
        compiler_params=_cparams(("arbitrary", "arbitrary")),
        name="merge_router",
    )(x2d, mod3, o_f, o_b, z, y_f, y_b, u, gates, onw, waup, s5d, wglu, wout, postw, prew, wr, br)


def _rank_kernel(route_ref, rank_ref, count_ref, carry_ref):
    i = pl.program_id(0)

    @pl.when(i == 0)
    def _():
        carry_ref[...] = jnp.zeros_like(carry_ref)

    route = route_ref[...]
    r = route.shape[0]
    lane = lax.broadcasted_iota(jnp.int32, route.shape, 1).astype(F32)
    oh0 = (lane == route[:, 0:1]).astype(F32)
    oh1 = (lane == route[:, 1:2]).astype(F32)
    both = oh0 + oh1
    row = lax.broadcasted_iota(jnp.int32, (r, r), 0)
    col = lax.broadcasted_iota(jnp.int32, (r, r), 1)
    before = _bdot((row > col).astype(F32), both) + carry_ref[...]
    rank0 = jnp.sum(oh0 * before, axis=-1, keepdims=True)
    rank1 = jnp.sum(oh1 * before, axis=-1, keepdims=True)
    lane_i = lax.broadcasted_iota(jnp.int32, route.shape, 1)
    rank_ref[...] = jnp.where(lane_i == 0, rank0, jnp.where(lane_i == 1, rank1, 0.0))
    carry_ref[...] = carry_ref[...] + jnp.sum(both, axis=0, keepdims=True)
    count_ref[...] = carry_ref[...]


def _expert_ranks(route):
    rows = route.shape[0]
    r = RANK_TILE
    return pl.pallas_call(
        _rank_kernel,
        out_shape=[jax.ShapeDtypeStruct((rows, LANES), F32), jax.ShapeDtypeStruct((1, LANES), F32)],
        grid=(rows // r,),
        in_specs=[pl.BlockSpec((r, LANES), lambda i: (i, 0))],
        out_specs=[pl.BlockSpec((r, LANES), lambda i: (i, 0)), pl.BlockSpec((1, LANES), lambda i: (0, 0))],
        scratch_shapes=[pltpu.VMEM((1, LANES), F32)],
        compiler_params=_cparams(("arbitrary",)),
        name="expert_ranks",
    )(route)


def _row_copy(src_ref, src_row, dst_ref, dst_row, sem):
    return pltpu.make_async_copy(src_ref.at[pl.ds(src_row, 1), :], dst_ref.at[pl.ds(dst_row, 1), :], sem)


def _row_token_kernel(pos_ref, zeros_ref, inv_ref, sem, *, lane_bits):
    i = pl.program_id(0)
    tm = pos_ref.shape[-1] // 2
    base = i * tm

    @pl.when(i == 0)
    def _():
        clear = pltpu.make_async_copy(zeros_ref, inv_ref, sem)
        clear.start()
        clear.wait()

    def body(t, _):
        for s in range(2):
            p = pos_ref[0, 0, s * tm + t]
            inv_ref[lax.shift_right_logical(p, lane_bits), p & (LANES - 1)] = base + t
        return 0

    lax.fori_loop(0, tm, body, 0, unroll=16)


def _row_tokens(pos3, n_rows):
    n_tiles, _, two_tm = pos3.shape
    lane_bits = LANES.bit_length() - 1
    assert 1 << lane_bits == LANES
    shape = (n_rows // LANES, LANES)
    return pl.pallas_call(
        functools.partial(_row_token_kernel, lane_bits=lane_bits),
        out_shape=jax.ShapeDtypeStruct(shape, jnp.int32),
        grid=(n_tiles,),
        in_specs=[pl.BlockSpec((1, 1, two_tm), lambda i: (i, 0, 0), memory_space=pltpu.SMEM),
                  pl.BlockSpec(memory_space=pl.ANY)],
        out_specs=pl.BlockSpec(memory_space=pltpu.SMEM),
        scratch_shapes=[pltpu.SemaphoreType.DMA],
        compiler_params=_cparams(("arbitrary",)),
        name="row_tokens",
    )(pos3, jnp.zeros(shape, jnp.int32))


def _gather_rows(src_ref, idx_ref, n, dst_ref, sem):
    for r in range(n):
        _row_copy(src_ref, idx_ref[0, 0, r], dst_ref, r, sem).start()


def _wait_rows(src_ref, dst_ref, sem):
    pltpu.make_async_copy(src_ref.at[pl.ds(0, dst_ref.shape[0]), :], dst_ref, sem).wait()


def _expert_kernel(te_ref, inv_ref, invn_ref, h2_ref, wg_ref, wu_ref, wd_ref, ys_ref, xbuf_ref, sems, *, n_tiles):
    del te_ref
    i = pl.program_id(0)
    te = ys_ref.shape[0]
    slot = i % 2

    @pl.when(i == 0)
    def _():
        _gather_rows(h2_ref, inv_ref, te, xbuf_ref.at[0], sems.at[0])

    _wait_rows(h2_ref, xbuf_ref.at[slot], sems.at[slot])
    _gather_rows(h2_ref, invn_ref, te, xbuf_ref.at[1 - slot], sems.at[1 - slot])
    x = xbuf_ref[slot].astype(BF16)
    hg = jnp.dot(x, wg_ref[0], preferred_element_type=F32)
    hu = jnp.dot(x, wu_ref[0], preferred_element_type=F32)
    ys_ref[...] = _bdot(_silu(hg) * hu, wd_ref[0])

    @pl.when(i == n_tiles - 1)
    def _():
        _wait_rows(h2_ref, xbuf_ref.at[1 - slot], sems.at[1 - slot])


def _expert_mlp(tile_expert, inv3, h2, wg, wu, wd):
    n_tiles, _, te = inv3.shape
    smem_blk = lambda m: pl.BlockSpec((1, 1, te), m, memory_space=pltpu.SMEM)
    return pl.pallas_call(
        functools.partial(_expert_kernel, n_tiles=n_tiles),
        out_shape=jax.ShapeDtypeStruct((n_tiles * te, D_MODEL), F32),
        grid_spec=pltpu.PrefetchScalarGridSpec(
            num_scalar_prefetch=1,
            grid=(n_tiles,),
            in_specs=[smem_blk(lambda i, e: (i, 0, 0)),
                      smem_blk(lambda i, e: (jnp.minimum(i + 1, n_tiles - 1), 0, 0)),
                      pl.BlockSpec(memory_space=pl.ANY),
                      pl.BlockSpec((1, D_MODEL, EXPERT_FF), lambda i, e: (e[i], 0, 0)),
                      pl.BlockSpec((1, D_MODEL, EXPERT_FF), lambda i, e: (e[i], 0, 0)),
                      pl.BlockSpec((1, EXPERT_FF, D_MODEL), lambda i, e: (e[i], 0, 0))],
            out_specs=pl.BlockSpec((te, D_MODEL), lambda i, e: (i, 0)),
            scratch_shapes=[pltpu.VMEM((2, te, D_MODEL), F32), pltpu.SemaphoreType.DMA((2,))]),
        compiler_params=_cparams(("arbitrary",)),
        name="expert_mlp",
    )(tile_expert, inv3, inv3, h2, wg, wu, wd)


def _combine_kernel(pos_ref, posn_ref, x1_ref, mod_ref, route_ref, postw_ref, ys_ref, out_ref, buf_ref, sems,
                    *, n_tiles):
    tm = x1_ref.shape[0]
    i = pl.program_id(0)
    slot = i % 2

    @pl.when(i == 0)
    def _():
        _gather_rows(ys_ref, pos_ref, 2 * tm, buf_ref.at[0], sems.at[0])

    _wait_rows(ys_ref, buf_ref.at[slot], sems.at[slot])
    _gather_rows(ys_ref, posn_ref, 2 * tm, buf_ref.at[1 - slot], sems.at[1 - slot])
    route = route_ref[...]
    moe = route[:, 2:3] * buf_ref[slot, 0:tm, :] + route[:, 3:4] * buf_ref[slot, tm:2 * tm, :]
    out_ref[...] = x1_ref[...] + mod_ref[0][5:6, :] * _rms(moe, postw_ref[...])

    @pl.when(i == n_tiles - 1)
    def _():
        _wait_rows(ys_ref, buf_ref.at[1 - slot], sems.at[1 - slot])


def _combine(pos3, x1, mod3, route, postw, ys, *, bsz, seq):
    tm = TOK_TILE
    nt = seq // tm
    n_tiles = bsz * nt
    row_map = lambda i: (i, 0)
    return pl.pallas_call(
        functools.partial(_combine_kernel, n_tiles=n_tiles),
        out_shape=jax.ShapeDtypeStruct(x1.shape, F32),
        grid=(n_tiles,),
        in_specs=[pl.BlockSpec((1, 1, 2 * tm), lambda i: (i, 0, 0), memory_space=pltpu.SMEM),
                  pl.BlockSpec((1, 1, 2 * tm), lambda i: (jnp.minimum(i + 1, n_tiles - 1), 0, 0),
                               memory_space=pltpu.SMEM),
                  pl.BlockSpec((tm, D_MODEL), row_map),
                  pl.BlockSpec((1, N_MOD, D_MODEL), lambda i: (i // nt, 0, 0)),
                  pl.BlockSpec((tm, LANES), row_map),
                  pl.BlockSpec((1, D_MODEL), lambda i: (0, 0)),
                  pl.BlockSpec(memory_space=pl.ANY)],
        out_specs=pl.BlockSpec((tm, D_MODEL), row_map),
        scratch_shapes=[pltpu.VMEM((2, 2 * tm, D_MODEL), F32), pltpu.SemaphoreType.DMA((2,))],
        compiler_params=_cparams(("arbitrary",)),
        name="combine",
    )(pos3, pos3, x1, mod3, route, postw, ys)


def _lane_pad(v, n=LANES):
    v = v.reshape(1, -1)
    return jnp.pad(v, ((0, 0), (0, n - v.shape[1])))


def kernel(x, c, ctx, c_ctx, w_mod, b_mod, pre_norm_w, post_norm_w, w_in, conv_w, a_log, dt_bias, o_norm_w,
           w_a_up, s5_lam_re, s5_lam_im, s5_log_step, s5_b_re, s5_b_im, s5_c_re, s5_c_im, s5_d, w_glu, w_out,
           w_router_group, b_router_group, w_router_expert, b_router_expert, w_gate_e, w_up_e, w_down_e):
    assert w_mod.shape[0] == 1, "single-layer block"
    bsz, seq, _ = x.shape
    ctx_len = ctx.shape[1]
    row = lambda v: v.reshape(1, -1)

    c16 = jnp.concatenate([c, c_ctx[None, :], jnp.zeros((2 * SUBLANES - bsz - 1, D_MODEL), F32)], axis=0)
    mod3 = _modulation(c16, w_mod[0], row(b_mod[0])).reshape(2 * SUBLANES, N_MOD, D_MODEL)

    wi = w_in[0]
    o_z, o_a, o_u, o_g = 3 * DN_WIDTH, 4 * DN_WIDTH, 4 * DN_WIDTH + 4 * DN_HEADS, 4 * DN_WIDTH + 4 * DN_HEADS + S5_WIDTH
    wts_in = (wi[:, :o_z].astype(BF16), wi[:, o_z:o_a].astype(BF16),
              jnp.pad(wi[:, o_a:o_u], ((0, 0), (0, LANES - 4 * DN_HEADS))).astype(BF16),
              wi[:, o_u:o_g].astype(BF16), wi[:, o_g:].astype(BF16))
    convw = jnp.pad(conv_w[0], ((0, SUBLANES - CONV_K), (0, 0)))
    alog = _lane_pad(a_log[0])
    dtb = _lane_pad(dt_bias[0])
    prew0, prew1 = row(pre_norm_w[0, 0]), row(pre_norm_w[0, 1])
    postw0, postw1 = row(post_norm_w[0, 0]), row(post_norm_w[0, 1])

    bar_re, bar_im, bb_re, bb_im = _s5_discretise(s5_lam_re[0], s5_lam_im[0], s5_log_step[0], s5_b_re[0], s5_b_im[0])
    s5w = _s5_weights(bar_re, bar_im, bb_re, bb_im, s5_c_re[0], s5_c_im[0], bsz)

    qkvn_c, gb_c, u_c = _input_projection(ctx.reshape(bsz * ctx_len, D_MODEL), mod3, bsz, 0, prew0, wts_in, convw,
                                          alog, dtb, bsz=bsz, seq=ctx_len, seg_len=ctx_len, full=False)
    s_zero = jnp.zeros((bsz, 2 * DN_HEADS, DN_HEAD_DIM, DN_HEAD_DIM), F32)
    _, _, s_ctx = _delta_rule(qkvn_c, gb_c, s_zero, bsz=bsz, seq=ctx_len)
    x_zero = jnp.zeros((2, bsz, S5_LANES), F32)
    (x_ctx,) = _s5_scan(u_c.reshape(bsz, ctx_len, S5_WIDTH), s5w, x_zero, emit_y=False)

    x2d = x.reshape(bsz * seq, D_MODEL)
    qkvn, gb, u, z, gates = _input_projection(x2d, mod3, 0, 1, prew0, wts_in, convw, alog, dtb,
                                                 bsz=bsz, seq=seq, seg_len=GRID_W, full=True)
    o_f, o_b, _ = _delta_rule(qkvn, gb, s_ctx, bsz=bsz, seq=seq)
    y_f, y_b, _ = _s5_scan(u.reshape(bsz, seq, S5_WIDTH), s5w, x_ctx, emit_y=True)
    rows2d = lambda a: a.reshape(bsz * seq, S5_WIDTH)

    w_router = jnp.pad(jnp.concatenate([w_router_group[0], w_router_expert[0]], axis=1),
                       ((0, 0), (0, LANES - N_EXPERT_GROUPS - N_EXPERTS))).astype(BF16)
    b_router = _lane_pad(jnp.concatenate([b_router_group[0], b_router_expert[0]]))
    wts_merge = (row(o_norm_w[0]), w_a_up[0].astype(BF16), row(s5_d[0]), w_glu[0].astype(BF16),
                 w_out[0].astype(BF16), postw0, prew1, w_router, b_router)
    x1, h2, route = _merge(x2d, mod3, o_f, o_b, z, rows2d(y_f), rows2d(y_b), u, gates, wts_merge, bsz=bsz, seq=seq)

    rank, count = _expert_ranks(route)
    n_tok = bsz * seq
    te = EXPERT_TILE
    counts = count[0, :N_EXPERTS].astype(jnp.int32)
    padded = ((counts + te - 1) // te) * te
    ends = jnp.cumsum(padded)
    starts = ends - padded
    ids = route[:, :2].astype(jnp.int32)
    one_hot = ids[:, :, None] == jnp.arange(N_EXPERTS, dtype=jnp.int32)
    pos = jnp.sum(jnp.where(one_hot, starts, 0), axis=-1) + rank[:, :2].astype(jnp.int32)
    pos3 = jnp.swapaxes(pos.reshape(n_tok // TOK_TILE, TOK_TILE, 2), 1, 2).reshape(n_tok // TOK_TILE, 1, 2 * TOK_TILE)
    n_sorted = 2 * n_tok + N_EXPERTS * te
    tile_start = jnp.arange(n_sorted // te, dtype=jnp.int32) * te
    tile_expert = jnp.minimum(jnp.sum((ends[None, :] <= tile_start[:, None]).astype(jnp.int32), axis=1), N_EXPERTS - 1)

    inv3 = _row_tokens(pos3, n_sorted).reshape(n_sorted // te, 1, te)
    ys = _expert_mlp(tile_expert, inv3, h2, w_gate_e[0].astype(BF16), w_up_e[0].astype(BF16),
                     w_down_e[0].astype(BF16))
    out = _combine(pos3, x1, mod3, route, postw1, ys, bsz=bsz, seq=seq)
    return out.reshape(bsz, seq, D_MODEL)
```

```python
import functools

import jax
import jax.numpy as jnp
from jax import lax
from jax.experimental import pallas as pl
from jax.experimental.pallas import tpu as pltpu

F32 = jnp.float32
BF16 = jnp.bfloat16
HIGHEST = lax.Precision.HIGHEST

D_MODEL = 1024
GRID_W = 64
DN_HEADS = 4
DN_HEAD_DIM = 128
DN_WIDTH = DN_HEADS * DN_HEAD_DIM
DN_CHUNK = 64
CONV_K = 5
S5_WIDTH = D_MODEL - DN_WIDTH
S5_GROUP = 16
S5_GROUPS = S5_WIDTH // S5_GROUP
S5_STATE = 64
S5_HALF_GROUPS = S5_GROUPS // 2
S5_HALF_STATE = S5_HALF_GROUPS * S5_STATE
S5_LANES = 2 * 2 * S5_HALF_STATE
N_EXPERT_GROUPS = 4
EXPERTS_PER_GROUP = 8
N_EXPERTS = N_EXPERT_GROUPS * EXPERTS_PER_GROUP
EXPERT_FF = 512
NORM_EPS = 1e-6
L2_EPS = 1e-6
N_MOD = 6
LANES = 128
SUBLANES = 8
TOKEN_TILE_ROWS = D_MODEL // LANES
VMEM_LIMIT = 56 * 1024 * 1024

TOK_TILE = 256
PROJ_TILE = 512
S5_TILE = 64
DN_CHUNKS_PER_STEP = 4
EXPERT_TILE = 512
RANK_TILE = 512


def _cparams(sem):
    return pltpu.CompilerParams(dimension_semantics=sem, vmem_limit_bytes=VMEM_LIMIT)


def _bdot(a, b):
    return jnp.dot(a.astype(BF16), b.astype(BF16), preferred_element_type=F32)


def _bdot_nt(a, b):
    return lax.dot_general(a.astype(BF16), b.astype(BF16), (((1,), (1,)), ((), ())),
                           preferred_element_type=F32)


def _bdot_tn(a, b):
    return lax.dot_general(a.astype(BF16), b.astype(BF16), (((0,), (0,)), ((), ())),
                           preferred_element_type=F32)


def _silu(x):
    return x * jax.nn.sigmoid(x)


def _rms(x, w):
    return x * lax.rsqrt(jnp.mean(x * x, axis=-1, keepdims=True) + NORM_EPS) * w


def _store_token_tiles(ref, val):
    n = val.shape[0]
    for j in range(TOKEN_TILE_ROWS):
        ref[pl.ds(j, n, stride=TOKEN_TILE_ROWS), :] = val[:, j * LANES:(j + 1) * LANES]


def _load_token_tiles(ref, first_token, n):
    return jnp.concatenate([ref[pl.ds(first_token * TOKEN_TILE_ROWS + j, n, stride=TOKEN_TILE_ROWS), :]
                            for j in range(TOKEN_TILE_ROWS)], axis=1)


def _mod_kernel(c_ref, w_ref, b_ref, o_ref):
    o_ref[...] = jnp.dot(_silu(c_ref[...]), w_ref[...], precision=HIGHEST,
                         preferred_element_type=F32) + b_ref[...]


def _modulation(c16, w_mod, b_mod):
    n = w_mod.shape[1]
    return pl.pallas_call(
        _mod_kernel,
        out_shape=jax.ShapeDtypeStruct((c16.shape[0], n), F32),
        grid=(n // D_MODEL,),
        in_specs=[pl.BlockSpec(c16.shape, lambda j: (0, 0)),
                  pl.BlockSpec((D_MODEL, D_MODEL), lambda j: (0, j)),
                  pl.BlockSpec((1, D_MODEL), lambda j: (0, j))],
        out_specs=pl.BlockSpec((c16.shape[0], D_MODEL), lambda j: (0, j)),
        compiler_params=_cparams(("arbitrary",)),
        name="modulation",
    )(c16, w_mod, b_mod)


def _inproj_kernel(x_ref, mod_ref, prew_ref, wqkv_ref, wz_ref, wab_ref, wu_ref, wg_ref, convw_ref,
                   alog_ref, dtb_ref, *out_refs, seg_len, full):
    if full:
        qkvn_ref, gb_ref, u_ref, z_ref, gates_ref = out_refs
    else:
        qkvn_ref, gb_ref, u_ref = out_refs
    tm = x_ref.shape[0]
    mod = mod_ref[0]
    h = _rms(x_ref[...], prew_ref[...]) * (1.0 + mod[1:2, :]) + mod[0:1, :]
    hb = h.astype(BF16)

    qkv = jnp.dot(hb, wqkv_ref[...], preferred_element_type=F32)
    pos = lax.broadcasted_iota(jnp.int32, (tm, 1), 0) % seg_len
    acc = qkv * convw_ref[CONV_K // 2:CONV_K // 2 + 1, :]
    for k in range(CONV_K):
        s = k - CONV_K // 2
        if s == 0:
            continue
        shifted = pltpu.roll(qkv, (-s) % tm, 0)
        ok = (pos + s >= 0) & (pos + s < seg_len)
        acc = acc + jnp.where(ok, shifted, 0.0) * convw_ref[k:k + 1, :]
    act = _silu(acc)
    for j in range(3 * DN_HEADS):
        sl = slice(j * DN_HEAD_DIM, (j + 1) * DN_HEAD_DIM)
        a = act[:, sl]
        if j < 2 * DN_HEADS:
            a = a * lax.rsqrt(jnp.sum(a * a, axis=-1, keepdims=True) + L2_EPS)
        qkvn_ref[:, sl] = a

    ab = jnp.dot(hb, wab_ref[...], preferred_element_type=F32)
    xa = ab + dtb_ref[...]
    softplus = jnp.maximum(xa, 0.0) + jnp.log1p(jnp.exp(-jnp.abs(xa)))
    g = -jnp.exp(alog_ref[...]) * softplus
    lane = lax.broadcasted_iota(jnp.int32, ab.shape, 1)
    gb_ref[...] = jnp.where(lane < 2 * DN_HEADS, g, jax.nn.sigmoid(ab))

    u_ref[...] = jnp.dot(hb, wu_ref[...], preferred_element_type=F32)
    if full:
        z_ref[...] = jnp.dot(hb, wz_ref[...], preferred_element_type=F32)
        gates_ref[...] = jnp.dot(hb, wg_ref[...], preferred_element_type=F32)


def _input_projection(x2d, mod3, mod_row0, mod_row_stride, prew, wts, convw, alog, dtb, *, bsz, seq, seg_len, full):
    wqkv, wz, wab, wu, wg = wts
    tm = min(PROJ_TILE, seq)
    nt = seq // tm
    rows = bsz * seq
    row_map = lambda b, i: (b * nt + i, 0)
    const = lambda b, i: (0, 0)
    resident = lambda a: pl.BlockSpec(a.shape, const, pipeline_mode=pl.Buffered(1))
    out_shape = [jax.ShapeDtypeStruct((rows, 3 * DN_WIDTH), F32),
                 jax.ShapeDtypeStruct((rows, LANES), F32),
                 jax.ShapeDtypeStruct((rows, S5_WIDTH), F32)]
    out_specs = [pl.BlockSpec((tm, 3 * DN_WIDTH), row_map),
                 pl.BlockSpec((tm, LANES), row_map),
                 pl.BlockSpec((tm, S5_WIDTH), row_map)]
    if full:
        out_shape += [jax.ShapeDtypeStruct((rows, DN_WIDTH), F32),
                      jax.ShapeDtypeStruct((rows, 2 * D_MODEL), F32)]
        out_specs += [pl.BlockSpec((tm, DN_WIDTH), row_map),
                      pl.BlockSpec((tm, 2 * D_MODEL), row_map)]
    return pl.pallas_call(
        functools.partial(_inproj_kernel, seg_len=seg_len, full=full),
        out_shape=out_shape,
        grid=(bsz, nt),
        in_specs=[pl.BlockSpec((tm, D_MODEL), row_map),
                  pl.BlockSpec((1, N_MOD, D_MODEL), lambda b, i: (mod_row0 + mod_row_stride * b, 0, 0)),
                  pl.BlockSpec((1, D_MODEL), const),
                  resident(wqkv), resident(wz), resident(wab), resident(wu), resident(wg),
                  pl.BlockSpec(convw.shape, const),
                  pl.BlockSpec((1, LANES), const),
                  pl.BlockSpec((1, LANES), const)],
        out_specs=out_specs,
        compiler_params=_cparams(("arbitrary", "arbitrary")),
        name="input_projection_full" if full else "input_projection_ctx",
    )(x2d, mod3, prew, wqkv, wz, wab, wu, wg, convw, alog, dtb)


def _delta_kernel(qf_ref, kf_ref, vf_ref, gbf_ref, qb_ref, kb_ref, vb_ref, gbb_ref, s0_ref,
                  of_ref, ob_ref, sout_ref, state_ref, *, n_steps, chunks):
    c = DN_CHUNK
    i = pl.program_id(1)

    @pl.when(i == 0)
    def _():
        state_ref[...] = s0_ref[0]

    row = lax.broadcasted_iota(jnp.int32, (c, c), 0)
    col = lax.broadcasted_iota(jnp.int32, (c, c), 1)
    eye = (row == col).astype(F32)
    scale = DN_HEAD_DIM ** -0.5
    dirs = ((qf_ref, kf_ref, vf_ref, gbf_ref, of_ref), (qb_ref, kb_ref, vb_ref, gbb_ref, ob_ref))
    chains = {}
    for d, (q_ref, k_ref, v_ref, gb_ref, o_ref) in enumerate(dirs):
        incl = (row >= col) if d == 0 else (row <= col)
        strict = (row > col) if d == 0 else (row < col)
        last = c - 1 if d == 0 else 0
        for sub in range(chunks):
            rs = slice(sub * c, (sub + 1) * c)
            gb = gb_ref[rs, :]
            gcum = jnp.dot(incl.astype(F32), gb, precision=HIGHEST, preferred_element_type=F32)
            gcum_t = gcum.T
            for h in range(DN_HEADS):
                lane = d * DN_HEADS + h
                hs = slice(h * DN_HEAD_DIM, (h + 1) * DN_HEAD_DIM)
                g_col = gcum[:, lane:lane + 1]
                g_row = gcum_t[lane:lane + 1, :]
                g_last = gcum[last:last + 1, lane:lane + 1]
                beta = gb[:, 2 * DN_HEADS + lane:2 * DN_HEADS + lane + 1]
                dec = jnp.where(incl, jnp.exp(jnp.where(incl, g_col - g_row, 0.0)), 0.0)
                q = q_ref[rs, hs] * scale
                k = k_ref[rs, hs]
                v = v_ref[rs, hs]
                k_beta = k * beta
                e_g = jnp.exp(g_col)
                chains[d, sub, h] = dict(
                    lane=lane, rs=rs, hs=hs, o_ref=o_ref, strict=strict, dec=dec, q=q, k=k, k_beta=k_beta, e_g=e_g,
                    rhs=jnp.concatenate([v * beta, k_beta * e_g], axis=1),
                    k_tail=k * jnp.exp(g_last - g_col), e_last=jnp.exp(g_last))
    every = list(chains.values())
    for ch in every:
        ch['npow'] = -jnp.where(ch['strict'], _bdot_nt(ch['k_beta'], ch['k']) * ch['dec'], 0.0)
        ch['attn'] = _bdot_nt(ch['q'], ch['k']) * ch['dec']
        ch['t_inv'] = eye + ch['npow']
    for _ in range(5):
        for ch in every:
            ch['npow'] = _bdot(ch['npow'], ch['npow'])
        for ch in every:
            ch['t_inv'] = ch['t_inv'] + _bdot(ch['t_inv'], ch['npow'])
    for ch in every:
        ch['sol'] = _bdot(ch['t_inv'], ch['rhs'])
        ch['q_dec'] = ch['q'] * ch['e_g']

    states = [state_ref[lane] for lane in range(2 * DN_HEADS)]
    for step in range(chunks):
        group = [chains[d, step if d == 0 else chunks - 1 - step, h] for d in range(2) for h in range(DN_HEADS)]
        for ch in group:
            s = states[ch['lane']]
            ch['ws'] = _bdot(ch['sol'][:, DN_HEAD_DIM:], s)
            ch['qs'] = _bdot(ch['q_dec'], s)
        for ch in group:
            ch['v_new'] = ch['sol'][:, :DN_HEAD_DIM] - ch['ws']
            ch['o_ref'][ch['rs'], ch['hs']] = ch['qs'] + _bdot(ch['attn'], ch['v_new'])
        for ch in group:
            states[ch['lane']] = states[ch['lane']] * ch['e_last'] + _bdot_tn(ch['k_tail'], ch['v_new'])
    for lane in range(2 * DN_HEADS):
        state_ref[lane] = states[lane]

    @pl.when(i == n_steps - 1)
    def _():
        sout_ref[0] = state_ref[...]


def _delta_rule(qkvn, gb, s0, *, bsz, seq):
    rows = DN_CHUNK * DN_CHUNKS_PER_STEP
    nb = seq // rows
    fwd = lambda j: (lambda b, i: (b * nb + i, j))
    bwd = lambda j: (lambda b, i: (b * nb + nb - 1 - i, j))
    blk = lambda m: pl.BlockSpec((rows, DN_WIDTH), m)
    gblk = lambda m: pl.BlockSpec((rows, LANES), m)
    st = pl.BlockSpec((1, 2 * DN_HEADS, DN_HEAD_DIM, DN_HEAD_DIM), lambda b, i: (b, 0, 0, 0))
    return pl.pallas_call(
        functools.partial(_delta_kernel, n_steps=nb, chunks=DN_CHUNKS_PER_STEP),
        out_shape=[jax.ShapeDtypeStruct((bsz * seq, DN_WIDTH), F32),
                   jax.ShapeDtypeStruct((bsz * seq, DN_WIDTH), F32),
                   jax.ShapeDtypeStruct(s0.shape, F32)],
        grid=(bsz, nb),
        in_specs=[blk(fwd(0)), blk(fwd(1)), blk(fwd(2)), gblk(fwd(0)),
                  blk(bwd(0)), blk(bwd(1)), blk(bwd(2)), gblk(bwd(0)), st],
        out_specs=[blk(fwd(0)), blk(bwd(0)), st],
        scratch_shapes=[pltpu.VMEM((2 * DN_HEADS, DN_HEAD_DIM, DN_HEAD_DIM), F32)],
        compiler_params=_cparams(("arbitrary", "arbitrary")),
        name="delta_rule",
    )(qkvn, qkvn, qkvn, gb, qkvn, qkvn, qkvn, gb, s0)


def _s5_disc_kernel(lre_ref, lim_ref, ls_ref, bre_ref, bim_ref, are_ref, aim_ref, bbre_ref, bbim_ref):
    lam_re = lre_ref[...]
    lam_im = lim_ref[...]
    dt = jnp.exp(ls_ref[...])
    mag = jnp.exp(lam_re * dt)
    ang = lam_im * dt
    bar_re = mag * jnp.cos(ang)
    bar_im = mag * jnp.sin(ang)
    den = lam_re * lam_re + lam_im * lam_im
    coef_re = ((bar_re - 1.0) * lam_re + bar_im * lam_im) / den
    coef_im = (bar_im * lam_re - (bar_re - 1.0) * lam_im) / den
    are_ref[...] = bar_re
    aim_ref[...] = bar_im
    bbre_ref[...] = coef_re * bre_ref[...] - coef_im * bim_ref[...]
    bbim_ref[...] = coef_re * bim_ref[...] + coef_im * bre_ref[...]


def _s5_discretise(lam_re, lam_im, log_step, b_re, b_im):
    rep = lambda t: jnp.repeat(t.reshape(2 * S5_GROUPS, -1), S5_GROUP, axis=0)
    rows = 2 * S5_GROUPS * S5_GROUP
    tr = lambda t: jnp.swapaxes(t, -1, -2).reshape(rows, S5_STATE)
    outs = pl.pallas_call(
        _s5_disc_kernel,
        out_shape=[jax.ShapeDtypeStruct((rows, S5_STATE), F32)] * 4,
        name="s5_discretise",
    )(rep(lam_re), rep(lam_im), rep(log_step[..., None]), tr(b_re), tr(b_im))
    shp = (2, S5_GROUPS, S5_GROUP, S5_STATE)
    bar_re, bar_im, bb_re, bb_im = (o.reshape(shp) for o in outs)
    return bar_re[:, :, 0], bar_im[:, :, 0], bb_re, bb_im


def _s5_kernel(uf_ref, ub_ref, wb_ref, wcre_ref, wcim_ref, are_ref, aim_ref, x0_ref, *rest, tt, n_tiles, emit_y):
    if emit_y:
        yf_ref, yb_ref, xout_ref, buf_ref, xst_ref, us_ref, ys_ref = rest
    else:
        xout_ref, buf_ref, xst_ref, us_ref = rest
        yf_ref = yb_ref = ys_ref = None
    i = pl.program_id(0)
    bsz = uf_ref.shape[0]
    hs = S5_HALF_STATE
    n_lane_tiles = S5_WIDTH // LANES
    blk = 512

    @pl.when(i == 0)
    def _():
        xst_ref[...] = x0_ref[...]

    for d, (u_ref, y_ref) in enumerate(((uf_ref, yf_ref), (ub_ref, yb_ref))):
        for b in range(bsz):
            for j in range(n_lane_tiles):
                us_ref[j, pl.ds(b, tt, stride=bsz), :] = u_ref[b, :, j * LANES:(j + 1) * LANES]
        for hh in range(2):
            u_h = jnp.concatenate([us_ref[2 * hh], us_ref[2 * hh + 1]], axis=1)
            buf_ref[d, :, hh * 2 * hs:(hh + 1) * 2 * hs] = _bdot(u_h, wb_ref[d, hh])
        for hh in range(2):
            for p in range(hs // blk):
                lr = hh * 2 * hs + p * blk
                li = lr + hs
                a_re = are_ref[d, :, lr:lr + blk]
                a_im = aim_ref[d, :, lr:lr + blk]

                def body(t, carry, d=d, lr=lr, li=li, a_re=a_re, a_im=a_im):
                    x_re, x_im = carry
                    step = t if d == 0 else tt - 1 - t
                    r0 = pl.multiple_of(step * SUBLANES, SUBLANES)
                    n_re = a_re * x_re - a_im * x_im + buf_ref[d, pl.ds(r0, SUBLANES), lr:lr + blk]
                    n_im = a_re * x_im + a_im * x_re + buf_ref[d, pl.ds(r0, SUBLANES), li:li + blk]
                    buf_ref[d, pl.ds(r0, SUBLANES), lr:lr + blk] = n_re
                    buf_ref[d, pl.ds(r0, SUBLANES), li:li + blk] = n_im
                    return n_re, n_im

                x_re, x_im = lax.fori_loop(0, tt, body, (xst_ref[d, :, lr:lr + blk], xst_ref[d, :, li:li + blk]),
                                           unroll=8)
                xst_ref[d, :, lr:lr + blk] = x_re
                xst_ref[d, :, li:li + blk] = x_im
        if emit_y:
            for hh in range(2):
                x_re = buf_ref[d, :, hh * 2 * hs:hh * 2 * hs + hs]
                x_im = buf_ref[d, :, hh * 2 * hs + hs:(hh + 1) * 2 * hs]
                y_h = _bdot(x_re, wcre_ref[d, hh]) - _bdot(x_im, wcim_ref[d, hh])
                ys_ref[2 * hh] = y_h[:, :LANES]
                ys_ref[2 * hh + 1] = y_h[:, LANES:]
            for b in range(bsz):
                for j in range(n_lane_tiles):
                    y_ref[b, :, j * LANES:(j + 1) * LANES] = ys_ref[j, pl.ds(b, tt, stride=bsz), :]

    @pl.when(i == n_tiles - 1)
    def _():
        xout_ref[...] = xst_ref[...]


def _s5_scan(u3, s5w, x0, *, emit_y):
    wb, wc_re, wc_im, a_re, a_im = s5w
    bsz, seq, _ = u3.shape
    assert bsz == SUBLANES, "one time step of all batches must fill one sublane group"
    tt = S5_TILE
    nt = seq // tt
    rows = tt * bsz
    full = lambda a: pl.BlockSpec(a.shape, lambda i: (0,) * a.ndim)
    ublk = lambda m: pl.BlockSpec((bsz, tt, S5_WIDTH), m)
    fwd = lambda i: (0, i, 0)
    bwd = lambda i: (0, nt - 1 - i, 0)
    out_shape = [jax.ShapeDtypeStruct(x0.shape, F32)]
    out_specs = [full(x0)]
    scratch = [pltpu.VMEM((2, rows, S5_LANES), F32), pltpu.VMEM(x0.shape, F32),
               pltpu.VMEM((S5_WIDTH // LANES, rows, LANES), F32)]
    if emit_y:
        out_shape = [jax.ShapeDtypeStruct(u3.shape, F32)] * 2 + out_shape
        out_specs = [ublk(fwd), ublk(bwd)] + out_specs
        scratch.append(pltpu.VMEM((S5_WIDTH // LANES, rows, LANES), F32))
    return pl.pallas_call(
        functools.partial(_s5_kernel, tt=tt, n_tiles=nt, emit_y=emit_y),
        out_shape=out_shape,
        grid=(nt,),
        in_specs=[ublk(fwd), ublk(bwd), full(wb), full(wc_re), full(wc_im), full(a_re), full(a_im), full(x0)],
        out_specs=out_specs,
        scratch_shapes=scratch,
        compiler_params=_cparams(("arbitrary",)),
        name="s5_scan_y" if emit_y else "s5_scan_state",
    )(u3, u3, wb, wc_re, wc_im, a_re, a_im, x0)


def _s5_weights(bar_re, bar_im, bb_re, bb_im, c_re, c_im, bsz):
    hg = S5_HALF_GROUPS
    eye = jnp.eye(hg, dtype=F32)

    def in_mat(bb):
        t = bb.reshape(2, 2, hg, S5_GROUP, S5_STATE)
        t = jnp.einsum('dhicp,ij->dhicjp', t, eye)
        return t.reshape(2, 2, hg * S5_GROUP, hg * S5_STATE)

    def out_mat(cc):
        t = cc.reshape(2, 2, hg, S5_GROUP, S5_STATE)
        t = jnp.einsum('dhicp,ij->dhipjc', t, eye)
        return t.reshape(2, 2, hg * S5_STATE, hg * S5_GROUP)

    wb = jnp.concatenate([in_mat(bb_re), in_mat(bb_im)], axis=-1).astype(BF16)
    wc_re = out_mat(c_re).astype(BF16)
    wc_im = out_mat(c_im).astype(BF16)

    def lanes(a):
        t = a.reshape(2, 2, 1, S5_HALF_STATE)
        t = jnp.broadcast_to(t, (2, 2, 2, S5_HALF_STATE)).reshape(2, 1, S5_LANES)
        return jnp.broadcast_to(t, (2, bsz, S5_LANES))

    return wb, wc_re, wc_im, lanes(bar_re), lanes(bar_im)


def _merge_kernel(x_ref, mod_ref, of_ref, ob_ref, z_ref, yf_ref, yb_ref, u_ref, gates_ref, onw_ref, waup_ref,
                  s5d_ref, wglu_ref, wout_ref, postw_ref, prew_ref, wr_ref, br_ref,
                  x1_ref, h2_ref, route_ref):
    mod = mod_ref[0]
    o_a = of_ref[...] + ob_ref[...]
    z = z_ref[...]
    parts = []
    for h in range(DN_HEADS):
        hs = slice(h * DN_HEAD_DIM, (h + 1) * DN_HEAD_DIM)
        parts.append(_rms(o_a[:, hs], onw_ref[...]) * _silu(z[:, hs]))
    y_a = _bdot(jnp.concatenate(parts, axis=1), waup_ref[...])
    y5 = yf_ref[...] + yb_ref[...] + s5d_ref[...] * u_ref[...]
    glu = _bdot(jax.nn.gelu(y5, approximate=True), wglu_ref[...])
    y_b = glu[:, :D_MODEL] * jax.nn.sigmoid(glu[:, D_MODEL:])
    gates = gates_ref[...]
    y = jax.nn.sigmoid(gates[:, :D_MODEL]) * y_a + jax.nn.sigmoid(gates[:, D_MODEL:]) * y_b
    mixed = _bdot(y, wout_ref[...])
    x1 = x_ref[...] + mod[2:3, :] * _rms(mixed, postw_ref[...])
    x1_ref[...] = x1
    h2 = _rms(x1, prew_ref[...]) * (1.0 + mod[4:5, :]) + mod[3:4, :]
    _store_token_tiles(h2_ref, h2)

    logits = _bdot(h2, wr_ref[...]) + br_ref[...]
    lane = lax.broadcasted_iota(jnp.int32, logits.shape, 1)
    neg = -jnp.inf
    big = jnp.int32(LANES)
    first = lambda hit: jnp.min(jnp.where(hit, lane, big), axis=-1, keepdims=True)
    g_logit = jnp.where(lane < N_EXPERT_GROUPS, logits, neg)
    g_max = jnp.max(g_logit, axis=-1, keepdims=True)
    g_sel = first(g_logit == g_max)
    g_w = 1.0 / jnp.sum(jnp.exp(g_logit - g_max), axis=-1, keepdims=True)
    e_idx = lane - N_EXPERT_GROUPS
    in_group = (e_idx >= 0) & (e_idx < N_EXPERTS) & ((e_idx // EXPERTS_PER_GROUP) == g_sel)
    e_logit = jnp.where(in_group, logits, neg)
    m1 = jnp.max(e_logit, axis=-1, keepdims=True)
    i1 = first(e_logit == m1)
    e_logit2 = jnp.where(lane == i1, neg, e_logit)
    m2 = jnp.max(e_logit2, axis=-1, keepdims=True)
    i2 = first(e_logit2 == m2)
    e2 = jnp.exp(m2 - m1)
    w1 = g_w / (1.0 + e2)
    w2 = g_w * e2 / (1.0 + e2)
    id1 = (i1 - N_EXPERT_GROUPS).astype(F32)
    id2 = (i2 - N_EXPERT_GROUPS).astype(F32)
    route_ref[...] = jnp.where(lane == 0, id1, jnp.where(lane == 1, id2, jnp.where(lane == 2, w1,
                               jnp.where(lane == 3, w2, 0.0))))


def _merge(x2d, mod3, o_f, o_b, z, y_f, y_b, u, gates, wts, *, bsz, seq):
    onw, waup, s5d, wglu, wout, postw, prew, wr, br = wts
    tm = PROJ_TILE
    nt = seq // tm
    rows = bsz * seq
    row_map = lambda b, i: (b * nt + i, 0)
    const = lambda b, i: (0, 0)
    cs = lambda a: pl.BlockSpec(a.shape, const, pipeline_mode=pl.Buffered(1))
    return pl.pallas_call(
        _merge_kernel,
        out_shape=[jax.ShapeDtypeStruct((rows, D_MODEL), F32),
                   jax.ShapeDtypeStruct((rows * TOKEN_TILE_ROWS, LANES), F32),
                   jax.ShapeDtypeStruct((rows, LANES), F32)],
        grid=(bsz, nt),
        in_specs=[pl.BlockSpec((tm, D_MODEL), row_map),
                  pl.BlockSpec((1, N_MOD, D_MODEL), lambda b, i: (b, 0, 0)),
                  pl.BlockSpec((tm, DN_WIDTH), row_map),
                  pl.BlockSpec((tm, DN_WIDTH), row_map),
                  pl.BlockSpec((tm, DN_WIDTH), row_map),
                  pl.BlockSpec((tm, S5_WIDTH), row_map),
                  pl.BlockSpec((tm, S5_WIDTH), row_map),
                  pl.BlockSpec((tm, S5_WIDTH), row_map),
                  pl.BlockSpec((tm, 2 * D_MODEL), row_map),
                  cs(onw), cs(waup), cs(s5d), cs(wglu), cs(wout), cs(postw), cs(prew), cs(wr), cs(br)],
        out_specs=[pl.BlockSpec((tm, D_MODEL), row_map),
                   pl.BlockSpec((tm * TOKEN_TILE_ROWS, LANES), row_map),
                   pl.BlockSpec((tm, LANES), row_map)],
        compiler_params=_cparams(("arbitrary", "arbitrary")),
        name="merge_router",
    )(x2d, mod3, o_f, o_b, z, y_f, y_b, u, gates, onw, waup, s5d, wglu, wout, postw, prew, wr, br)


def _rank_kernel(route_ref, rank_ref, count_ref, carry_ref):
    i = pl.program_id(0)

    @pl.when(i == 0)
    def _():
        carry_ref[...] = jnp.zeros_like(carry_ref)

    route = route_ref[...]
    r = route.shape[0]
    lane = lax.broadcasted_iota(jnp.int32, route.shape, 1).astype(F32)
    oh0 = (lane == route[:, 0:1]).astype(F32)
    oh1 = (lane == route[:, 1:2]).astype(F32)
    both = oh0 + oh1
    row = lax.broadcasted_iota(jnp.int32, (r, r), 0)
    col = lax.broadcasted_iota(jnp.int32, (r, r), 1)
    before = _bdot((row > col).astype(F32), both) + carry_ref[...]
    rank0 = jnp.sum(oh0 * before, axis=-1, keepdims=True)
    rank1 = jnp.sum(oh1 * before, axis=-1, keepdims=True)
    lane_i = lax.broadcasted_iota(jnp.int32, route.shape, 1)
    rank_ref[...] = jnp.where(lane_i == 0, rank0, jnp.where(lane_i == 1, rank1, 0.0))
    carry_ref[...] = carry_ref[...] + jnp.sum(both, axis=0, keepdims=True)
    count_ref[...] = carry_ref[...]


def _expert_ranks(route):
    rows = route.shape[0]
    r = RANK_TILE
    return pl.pallas_call(
        _rank_kernel,
        out_shape=[jax.ShapeDtypeStruct((rows, LANES), F32), jax.ShapeDtypeStruct((1, LANES), F32)],
        grid=(rows // r,),
        in_specs=[pl.BlockSpec((r, LANES), lambda i: (i, 0))],
        out_specs=[pl.BlockSpec((r, LANES), lambda i: (i, 0)), pl.BlockSpec((1, LANES), lambda i: (0, 0))],
        scratch_shapes=[pltpu.VMEM((1, LANES), F32)],
        compiler_params=_cparams(("arbitrary",)),
        name="expert_ranks",
    )(route)


def _token_copy(src_ref, src_token, dst_ref, dst_token, sem):
    r = TOKEN_TILE_ROWS
    return pltpu.make_async_copy(src_ref.at[pl.ds(pl.multiple_of(src_token * r, r), r), :],
                                 dst_ref.at[pl.ds(dst_token * r, r), :], sem)


def _row_token_kernel(pos_ref, zeros_ref, inv_ref, sem, *, lane_bits):
    i = pl.program_id(0)
    tm = pos_ref.shape[-1] // 2
    base = i * tm

    @pl.when(i == 0)
    def _():
        clear = pltpu.make_async_copy(zeros_ref, inv_ref, sem)
        clear.start()
        clear.wait()

    def body(t, _):
        for s in range(2):
            p = pos_ref[0, 0, s * tm + t]
            inv_ref[lax.shift_right_logical(p, lane_bits), p & (LANES - 1)] = base + t
        return 0

    lax.fori_loop(0, tm, body, 0, unroll=16)


def _row_tokens(pos3, n_rows):
    n_tiles, _, two_tm = pos3.shape
    lane_bits = LANES.bit_length() - 1
    assert 1 << lane_bits == LANES
    shape = (n_rows // LANES, LANES)
    return pl.pallas_call(
        functools.partial(_row_token_kernel, lane_bits=lane_bits),
        out_shape=jax.ShapeDtypeStruct(shape, jnp.int32),
        grid=(n_tiles,),
        in_specs=[pl.BlockSpec((1, 1, two_tm), lambda i: (i, 0, 0), memory_space=pltpu.SMEM),
                  pl.BlockSpec(memory_space=pl.ANY)],
        out_specs=pl.BlockSpec(memory_space=pltpu.SMEM),
        scratch_shapes=[pltpu.SemaphoreType.DMA],
        compiler_params=_cparams(("arbitrary",)),
        name="row_tokens",
    )(pos3, jnp.zeros(shape, jnp.int32))


def _gather_rows(src_ref, idx_ref, n, dst_ref, sem):
    for t in range(n):
        _token_copy(src_ref, idx_ref[0, 0, t], dst_ref, t, sem).start()


def _wait_rows(src_ref, dst_ref, sem):
    pltpu.make_async_copy(src_ref.at[pl.ds(0, dst_ref.shape[0]), :], dst_ref, sem).wait()


def _expert_kernel(te_ref, inv_ref, invn_ref, h2_ref, wg_ref, wu_ref, wd_ref, ys_ref, xbuf_ref, sems, *, n_tiles):
    del te_ref
    i = pl.program_id(0)
    te = ys_ref.shape[0] // TOKEN_TILE_ROWS
    slot = i % 2

    @pl.when(i == 0)
    def _():
        _gather_rows(h2_ref, inv_ref, te, xbuf_ref.at[0], sems.at[0])

    _wait_rows(h2_ref, xbuf_ref.at[slot], sems.at[slot])
    _gather_rows(h2_ref, invn_ref, te, xbuf_ref.at[1 - slot], sems.at[1 - slot])
    x = _load_token_tiles(xbuf_ref.at[slot], 0, te).astype(BF16)
    hg = jnp.dot(x, wg_ref[0], preferred_element_type=F32)
    hu = jnp.dot(x, wu_ref[0], preferred_element_type=F32)
    _store_token_tiles(ys_ref, _bdot(_silu(hg) * hu, wd_ref[0]))

    @pl.when(i == n_tiles - 1)
    def _():
        _wait_rows(h2_ref, xbuf_ref.at[1 - slot], sems.at[1 - slot])


def _expert_mlp(tile_expert, inv3, h2, wg, wu, wd):
    n_tiles, _, te = inv3.shape
    smem_blk = lambda m: pl.BlockSpec((1, 1, te), m, memory_space=pltpu.SMEM)
    return pl.pallas_call(
        functools.partial(_expert_kernel, n_tiles=n_tiles),
        out_shape=jax.ShapeDtypeStruct((n_tiles * te * TOKEN_TILE_ROWS, LANES), F32),
        grid_spec=pltpu.PrefetchScalarGridSpec(
            num_scalar_prefetch=1,
            grid=(n_tiles,),
            in_specs=[smem_blk(lambda i, e: (i, 0, 0)),
                      smem_blk(lambda i, e: (jnp.minimum(i + 1, n_tiles - 1), 0, 0)),
                      pl.BlockSpec(memory_space=pl.ANY),
                      pl.BlockSpec((1, D_MODEL, EXPERT_FF), lambda i, e: (e[i], 0, 0)),
                      pl.BlockSpec((1, D_MODEL, EXPERT_FF), lambda i, e: (e[i], 0, 0)),
                      pl.BlockSpec((1, EXPERT_FF, D_MODEL), lambda i, e: (e[i], 0, 0))],
            out_specs=pl.BlockSpec((te * TOKEN_TILE_ROWS, LANES), lambda i, e: (i, 0)),
            scratch_shapes=[pltpu.VMEM((2, te * TOKEN_TILE_ROWS, LANES), F32), pltpu.SemaphoreType.DMA((2,))]),
        compiler_params=_cparams(("arbitrary",)),
        name="expert_mlp",
    )(tile_expert, inv3, inv3, h2, wg, wu, wd)


def _combine_kernel(pos_ref, posn_ref, x1_ref, mod_ref, route_ref, postw_ref, ys_ref, out_ref, buf_ref, sems,
                    *, n_tiles):
    tm = x1_ref.shape[0]
    i = pl.program_id(0)
    slot = i % 2

    @pl.when(i == 0)
    def _():
        _gather_rows(ys_ref, pos_ref, 2 * tm, buf_ref.at[0], sems.at[0])

    _wait_rows(ys_ref, buf_ref.at[slot], sems.at[slot])
    _gather_rows(ys_ref, posn_ref, 2 * tm, buf_ref.at[1 - slot], sems.at[1 - slot])
    route = route_ref[...]
    moe = (route[:, 2:3] * _load_token_tiles(buf_ref.at[slot], 0, tm)
           + route[:, 3:4] * _load_token_tiles(buf_ref.at[slot], tm, tm))
    out_ref[...] = x1_ref[...] + mod_ref[0][5:6, :] * _rms(moe, postw_ref[...])

    @pl.when(i == n_tiles - 1)
    def _():
        _wait_rows(ys_ref, buf_ref.at[1 - slot], sems.at[1 - slot])


def _combine(pos3, x1, mod3, route, postw, ys, *, bsz, seq):
    tm = TOK_TILE
    nt = seq // tm
    n_tiles = bsz * nt
    row_map = lambda i: (i, 0)
    return pl.pallas_call(
        functools.partial(_combine_kernel, n_tiles=n_tiles),
        out_shape=jax.ShapeDtypeStruct(x1.shape, F32),
        grid=(n_tiles,),
        in_specs=[pl.BlockSpec((1, 1, 2 * tm), lambda i: (i, 0, 0), memory_space=pltpu.SMEM),
                  pl.BlockSpec((1, 1, 2 * tm), lambda i: (jnp.minimum(i + 1, n_tiles - 1), 0, 0),
                               memory_space=pltpu.SMEM),
                  pl.BlockSpec((tm, D_MODEL), row_map),
                  pl.BlockSpec((1, N_MOD, D_MODEL), lambda i: (i // nt, 0, 0)),
                  pl.BlockSpec((tm, LANES), row_map),
                  pl.BlockSpec((1, D_MODEL), lambda i: (0, 0)),
                  pl.BlockSpec(memory_space=pl.ANY)],
        out_specs=pl.BlockSpec((tm, D_MODEL), row_map),
        scratch_shapes=[pltpu.VMEM((2, 2 * tm * TOKEN_TILE_ROWS, LANES), F32), pltpu.SemaphoreType.DMA((2,))],
        compiler_params=_cparams(("arbitrary",)),
        name="combine",
    )(pos3, pos3, x1, mod3, route, postw, ys)


def _lane_pad(v, n=LANES):
    v = v.reshape(1, -1)
    return jnp.pad(v, ((0, 0), (0, n - v.shape[1])))


def kernel(x, c, ctx, c_ctx, w_mod, b_mod, pre_norm_w, post_norm_w, w_in, conv_w, a_log, dt_bias, o_norm_w,
           w_a_up, s5_lam_re, s5_lam_im, s5_log_step, s5_b_re, s5_b_im, s5_c_re, s5_c_im, s5_d, w_glu, w_out,
           w_router_group, b_router_group, w_router_expert, b_router_expert, w_gate_e, w_up_e, w_down_e):
    assert w_mod.shape[0] == 1, "single-layer block"
    bsz, seq, _ = x.shape
    ctx_len = ctx.shape[1]
    row = lambda v: v.reshape(1, -1)

    c16 = jnp.concatenate([c, c_ctx[None, :], jnp.zeros((2 * SUBLANES - bsz - 1, D_MODEL), F32)], axis=0)
    mod3 = _modulation(c16, w_mod[0], row(b_mod[0])).reshape(2 * SUBLANES, N_MOD, D_MODEL)

    wi = w_in[0]
    o_z, o_a, o_u, o_g = 3 * DN_WIDTH, 4 * DN_WIDTH, 4 * DN_WIDTH + 4 * DN_HEADS, 4 * DN_WIDTH + 4 * DN_HEADS + S5_WIDTH
    wts_in = (wi[:, :o_z].astype(BF16), wi[:, o_z:o_a].astype(BF16),
              jnp.pad(wi[:, o_a:o_u], ((0, 0), (0, LANES - 4 * DN_HEADS))).astype(BF16),
              wi[:, o_u:o_g].astype(BF16), wi[:, o_g:].astype(BF16))
    convw = jnp.pad(conv_w[0], ((0, SUBLANES - CONV_K), (0, 0)))
    alog = _lane_pad(a_log[0])
    dtb = _lane_pad(dt_bias[0])
    prew0, prew1 = row(pre_norm_w[0, 0]), row(pre_norm_w[0, 1])
    postw0, postw1 = row(post_norm_w[0, 0]), row(post_norm_w[0, 1])

    bar_re, bar_im, bb_re, bb_im = _s5_discretise(s5_lam_re[0], s5_lam_im[0], s5_log_step[0], s5_b_re[0], s5_b_im[0])
    s5w = _s5_weights(bar_re, bar_im, bb_re, bb_im, s5_c_re[0], s5_c_im[0], bsz)

    qkvn_c, gb_c, u_c = _input_projection(ctx.reshape(bsz * ctx_len, D_MODEL), mod3, bsz, 0, prew0, wts_in, convw,
                                          alog, dtb, bsz=bsz, seq=ctx_len, seg_len=ctx_len, full=False)
    s_zero = jnp.zeros((bsz, 2 * DN_HEADS, DN_HEAD_DIM, DN_HEAD_DIM), F32)
    _, _, s_ctx = _delta_rule(qkvn_c, gb_c, s_zero, bsz=bsz, seq=ctx_len)
    x_zero = jnp.zeros((2, bsz, S5_LANES), F32)
    (x_ctx,) = _s5_scan(u_c.reshape(bsz, ctx_len, S5_WIDTH), s5w, x_zero, emit_y=False)

    x2d = x.reshape(bsz * seq, D_MODEL)
    qkvn, gb, u, z, gates = _input_projection(x2d, mod3, 0, 1, prew0, wts_in, convw, alog, dtb,
                                                 bsz=bsz, seq=seq, seg_len=GRID_W, full=True)
    o_f, o_b, _ = _delta_rule(qkvn, gb, s_ctx, bsz=bsz, seq=seq)
    y_f, y_b, _ = _s5_scan(u.reshape(bsz, seq, S5_WIDTH), s5w, x_ctx, emit_y=True)
    rows2d = lambda a: a.reshape(bsz * seq, S5_WIDTH)

    w_router = jnp.pad(jnp.concatenate([w_router_group[0], w_router_expert[0]], axis=1),
                       ((0, 0), (0, LANES - N_EXPERT_GROUPS - N_EXPERTS))).astype(BF16)
    b_router = _lane_pad(jnp.concatenate([b_router_group[0], b_router_expert[0]]))
    wts_merge = (row(o_norm_w[0]), w_a_up[0].astype(BF16), row(s5_d[0]), w_glu[0].astype(BF16),
                 w_out[0].astype(BF16), postw0, prew1, w_router, b_router)
    x1, h2, route = _merge(x2d, mod3, o_f, o_b, z, rows2d(y_f), rows2d(y_b), u, gates, wts_merge, bsz=bsz, seq=seq)

    rank, count = _expert_ranks(route)
    n_tok = bsz * seq
    te = EXPERT_TILE
    counts = count[0, :N_EXPERTS].astype(jnp.int32)
    padded = ((counts + te - 1) // te) * te
    ends = jnp.cumsum(padded)
    starts = ends - padded
    ids = route[:, :2].astype(jnp.int32)
    one_hot = ids[:, :, None] == jnp.arange(N_EXPERTS, dtype=jnp.int32)
    pos = jnp.sum(jnp.where(one_hot, starts, 0), axis=-1) + rank[:, :2].astype(jnp.int32)
    pos3 = jnp.swapaxes(pos.reshape(n_tok // TOK_TILE, TOK_TILE, 2), 1, 2).reshape(n_tok // TOK_TILE, 1, 2 * TOK_TILE)
    n_sorted = 2 * n_tok + N_EXPERTS * te
    tile_start = jnp.arange(n_sorted // te, dtype=jnp.int32) * te
    tile_expert = jnp.minimum(jnp.sum((ends[None, :] <= tile_start[:, None]).astype(jnp.int32), axis=1), N_EXPERTS - 1)

    inv3 = _row_tokens(pos3, n_sorted).reshape(n_sorted // te, 1, te)
    ys = _expert_mlp(tile_expert, inv3, h2, w_gate_e[0].astype(BF16), w_up_e[0].astype(BF16),
                     w_down_e[0].astype(BF16))
    out = _combine(pos3, x1, mod3, route, postw1, ys, bsz=bsz, seq=seq)
    return out.reshape(bsz, seq, D_MODEL)
```

```python
import functools

import jax
import jax.numpy as jnp
from jax import lax
from jax.experimental import pallas as pl
from jax.experimental.pallas import tpu as pltpu

F32 = jnp.float32
BF16 = jnp.bfloat16
HIGHEST = lax.Precision.HIGHEST

D_MODEL = 1024
GRID_W = 64
DN_HEADS = 4
DN_HEAD_DIM = 128
DN_WIDTH = DN_HEADS * DN_HEAD_DIM
DN_CHUNK = 64
CONV_K = 5
S5_WIDTH = D_MODEL - DN_WIDTH
S5_GROUP = 16
S5_GROUPS = S5_WIDTH // S5_GROUP
S5_STATE = 64
S5_HALF_GROUPS = S5_GROUPS // 2
S5_HALF_STATE = S5_HALF_GROUPS * S5_STATE
S5_LANES = 2 * 2 * S5_HALF_STATE
N_EXPERT_GROUPS = 4
EXPERTS_PER_GROUP = 8
N_EXPERTS = N_EXPERT_GROUPS * EXPERTS_PER_GROUP
EXPERT_FF = 512
NORM_EPS = 1e-6
L2_EPS = 1e-6
N_MOD = 6
LANES = 128
SUBLANES = 8
TOKEN_TILE_ROWS = D_MODEL // LANES
VMEM_LIMIT = 56 * 1024 * 1024

TOK_TILE = 256
PROJ_TILE = 512
S5_TILE = 64
DN_CHUNKS_PER_STEP = 4
EXPERT_TILE = 512
RANK_TILE = 512
GATHER_BUFFERS = 3


def _cparams(sem):
    return pltpu.CompilerParams(dimension_semantics=sem, vmem_limit_bytes=VMEM_LIMIT)


def _bdot(a, b):
    return jnp.dot(a.astype(BF16), b.astype(BF16), preferred_element_type=F32)


def _bdot_nt(a, b):
    return lax.dot_general(a.astype(BF16), b.astype(BF16), (((1,), (1,)), ((), ())),
                           preferred_element_type=F32)


def _bdot_tn(a, b):
    return lax.dot_general(a.astype(BF16), b.astype(BF16), (((0,), (0,)), ((), ())),
                           preferred_element_type=F32)


def _silu(x):
    return x * jax.nn.sigmoid(x)


def _rms(x, w):
    return x * lax.rsqrt(jnp.mean(x * x, axis=-1, keepdims=True) + NORM_EPS) * w


def _store_token_tiles(ref, val):
    n = val.shape[0]
    for j in range(TOKEN_TILE_ROWS):
        ref[pl.ds(j, n, stride=TOKEN_TILE_ROWS), :] = val[:, j * LANES:(j + 1) * LANES]


def _load_token_tiles(ref, first_token, n):
    return jnp.concatenate([ref[pl.ds(first_token * TOKEN_TILE_ROWS + j, n, stride=TOKEN_TILE_ROWS), :]
                            for j in range(TOKEN_TILE_ROWS)], axis=1)


def _mod_kernel(c_ref, w_ref, b_ref, o_ref):
    o_ref[...] = jnp.dot(_silu(c_ref[...]), w_ref[...], precision=HIGHEST,
                         preferred_element_type=F32) + b_ref[...]


def _modulation(c16, w_mod, b_mod):
    n = w_mod.shape[1]
    return pl.pallas_call(
        _mod_kernel,
        out_shape=jax.ShapeDtypeStruct((c16.shape[0], n), F32),
        grid=(n // D_MODEL,),
        in_specs=[pl.BlockSpec(c16.shape, lambda j: (0, 0)),
                  pl.BlockSpec((D_MODEL, D_MODEL), lambda j: (0, j)),
                  pl.BlockSpec((1, D_MODEL), lambda j: (0, j))],
        out_specs=pl.BlockSpec((c16.shape[0], D_MODEL), lambda j: (0, j)),
        compiler_params=_cparams(("arbitrary",)),
        name="modulation",
    )(c16, w_mod, b_mod)


def _inproj_kernel(x_ref, mod_ref, prew_ref, wqkv_ref, wz_ref, wab_ref, wu_ref, wg_ref, convw_ref,
                   alog_ref, dtb_ref, *out_refs, seg_len, full):
    if full:
        qkvn_ref, gb_ref, u_ref, z_ref, gates_ref = out_refs
    else:
        qkvn_ref, gb_ref, u_ref = out_refs
    tm = x_ref.shape[0]
    mod = mod_ref[0]
    h = _rms(x_ref[...], prew_ref[...]) * (1.0 + mod[1:2, :]) + mod[0:1, :]
    hb = h.astype(BF16)

    qkv = jnp.dot(hb, wqkv_ref[...], preferred_element_type=F32)
    pos = lax.broadcasted_iota(jnp.int32, (tm, 1), 0) % seg_len
    acc = qkv * convw_ref[CONV_K // 2:CONV_K // 2 + 1, :]
    for k in range(CONV_K):
        s = k - CONV_K // 2
        if s == 0:
            continue
        shifted = pltpu.roll(qkv, (-s) % tm, 0)
        ok = (pos + s >= 0) & (pos + s < seg_len)
        acc = acc + jnp.where(ok, shifted, 0.0) * convw_ref[k:k + 1, :]
    act = _silu(acc)
    for j in range(3 * DN_HEADS):
        sl = slice(j * DN_HEAD_DIM, (j + 1) * DN_HEAD_DIM)
        a = act[:, sl]
        if j < 2 * DN_HEADS:
            a = a * lax.rsqrt(jnp.sum(a * a, axis=-1, keepdims=True) + L2_EPS)
        qkvn_ref[:, sl] = a

    ab = jnp.dot(hb, wab_ref[...], preferred_element_type=F32)
    xa = ab + dtb_ref[...]
    softplus = jnp.maximum(xa, 0.0) + jnp.log1p(jnp.exp(-jnp.abs(xa)))
    g = -jnp.exp(alog_ref[...]) * softplus
    lane = lax.broadcasted_iota(jnp.int32, ab.shape, 1)
    gb_ref[...] = jnp.where(lane < 2 * DN_HEADS, g, jax.nn.sigmoid(ab))

    u_ref[...] = jnp.dot(hb, wu_ref[...], preferred_element_type=F32)
    if full:
        z_ref[...] = jnp.dot(hb, wz_ref[...], preferred_element_type=F32)
        gates_ref[...] = jnp.dot(hb, wg_ref[...], preferred_element_type=F32)


def _input_projection(x2d, mod3, mod_row0, mod_row_stride, prew, wts, convw, alog, dtb, *, bsz, seq, seg_len, full):
    wqkv, wz, wab, wu, wg = wts
    tm = min(PROJ_TILE, seq)
    nt = seq // tm
    rows = bsz * seq
    row_map = lambda b, i: (b * nt + i, 0)
    const = lambda b, i: (0, 0)
    resident = lambda a: pl.BlockSpec(a.shape, const, pipeline_mode=pl.Buffered(1))
    out_shape = [jax.ShapeDtypeStruct((rows, 3 * DN_WIDTH), F32),
                 jax.ShapeDtypeStruct((rows, LANES), F32),
                 jax.ShapeDtypeStruct((rows, S5_WIDTH), F32)]
    out_specs = [pl.BlockSpec((tm, 3 * DN_WIDTH), row_map),
                 pl.BlockSpec((tm, LANES), row_map),
                 pl.BlockSpec((tm, S5_WIDTH), row_map)]
    if full:
        out_shape += [jax.ShapeDtypeStruct((rows, DN_WIDTH), F32),
                      jax.ShapeDtypeStruct((rows, 2 * D_MODEL), F32)]
        out_specs += [pl.BlockSpec((tm, DN_WIDTH), row_map),
                      pl.BlockSpec((tm, 2 * D_MODEL), row_map)]
    return pl.pallas_call(
        functools.partial(_inproj_kernel, seg_len=seg_len, full=full),
        out_shape=out_shape,
        grid=(bsz, nt),
        in_specs=[pl.BlockSpec((tm, D_MODEL), row_map),
                  pl.BlockSpec((1, N_MOD, D_MODEL), lambda b, i: (mod_row0 + mod_row_stride * b, 0, 0)),
                  pl.BlockSpec((1, D_MODEL), const),
                  resident(wqkv), resident(wz), resident(wab), resident(wu), resident(wg),
                  pl.BlockSpec(convw.shape, const),
                  pl.BlockSpec((1, LANES), const),
                  pl.BlockSpec((1, LANES), const)],
        out_specs=out_specs,
        compiler_params=_cparams(("arbitrary", "arbitrary")),
        name="input_projection_full" if full else "input_projection_ctx",
    )(x2d, mod3, prew, wqkv, wz, wab, wu, wg, convw, alog, dtb)


def _delta_kernel(qf_ref, kf_ref, vf_ref, gbf_ref, qb_ref, kb_ref, vb_ref, gbb_ref, s0_ref,
                  of_ref, ob_ref, sout_ref, state_ref, *, n_steps, chunks):
    c = DN_CHUNK
    i = pl.program_id(1)

    @pl.when(i == 0)
    def _():
        state_ref[...] = s0_ref[0]

    row = lax.broadcasted_iota(jnp.int32, (c, c), 0)
    col = lax.broadcasted_iota(jnp.int32, (c, c), 1)
    eye = (row == col).astype(F32)
    scale = DN_HEAD_DIM ** -0.5
    dirs = ((qf_ref, kf_ref, vf_ref, gbf_ref, of_ref), (qb_ref, kb_ref, vb_ref, gbb_ref, ob_ref))
    chains = {}
    for d, (q_ref, k_ref, v_ref, gb_ref, o_ref) in enumerate(dirs):
        incl = (row >= col) if d == 0 else (row <= col)
        strict = (row > col) if d == 0 else (row < col)
        last = c - 1 if d == 0 else 0
        for sub in range(chunks):
            rs = slice(sub * c, (sub + 1) * c)
            gb = gb_ref[rs, :]
            gcum = jnp.dot(incl.astype(F32), gb, precision=HIGHEST, preferred_element_type=F32)
            gcum_t = gcum.T
            for h in range(DN_HEADS):
                lane = d * DN_HEADS + h
                hs = slice(h * DN_HEAD_DIM, (h + 1) * DN_HEAD_DIM)
                g_col = gcum[:, lane:lane + 1]
                g_row = gcum_t[lane:lane + 1, :]
                g_last = gcum[last:last + 1, lane:lane + 1]
                beta = gb[:, 2 * DN_HEADS + lane:2 * DN_HEADS + lane + 1]
                dec = jnp.where(incl, jnp.exp(jnp.where(incl, g_col - g_row, 0.0)), 0.0)
                q = q_ref[rs, hs] * scale
                k = k_ref[rs, hs]
                v = v_ref[rs, hs]
                k_beta = k * beta
                e_g = jnp.exp(g_col)
                chains[d, sub, h] = dict(
                    lane=lane, rs=rs, hs=hs, o_ref=o_ref, strict=strict, dec=dec, q=q, k=k, k_beta=k_beta, e_g=e_g,
                    rhs=jnp.concatenate([v * beta, k_beta * e_g], axis=1),
                    k_tail=k * jnp.exp(g_last - g_col), e_last=jnp.exp(g_last))
    every = list(chains.values())
    for ch in every:
        ch['npow'] = -jnp.where(ch['strict'], _bdot_nt(ch['k_beta'], ch['k']) * ch['dec'], 0.0)
        ch['attn'] = _bdot_nt(ch['q'], ch['k']) * ch['dec']
        ch['t_inv'] = eye + ch['npow']
    for _ in range(5):
        for ch in every:
            ch['npow'] = _bdot(ch['npow'], ch['npow'])
        for ch in every:
            ch['t_inv'] = ch['t_inv'] + _bdot(ch['t_inv'], ch['npow'])
    for ch in every:
        ch['sol'] = _bdot(ch['t_inv'], ch['rhs'])
        ch['q_dec'] = ch['q'] * ch['e_g']

    states = [state_ref[lane] for lane in range(2 * DN_HEADS)]
    for step in range(chunks):
        group = [chains[d, step if d == 0 else chunks - 1 - step, h] for d in range(2) for h in range(DN_HEADS)]
        for ch in group:
            s = states[ch['lane']]
            ch['ws'] = _bdot(ch['sol'][:, DN_HEAD_DIM:], s)
            ch['qs'] = _bdot(ch['q_dec'], s)
        for ch in group:
            ch['v_new'] = ch['sol'][:, :DN_HEAD_DIM] - ch['ws']
            ch['o_ref'][ch['rs'], ch['hs']] = ch['qs'] + _bdot(ch['attn'], ch['v_new'])
        for ch in group:
            states[ch['lane']] = states[ch['lane']] * ch['e_last'] + _bdot_tn(ch['k_tail'], ch['v_new'])
    for lane in range(2 * DN_HEADS):
        state_ref[lane] = states[lane]

    @pl.when(i == n_steps - 1)
    def _():
        sout_ref[0] = state_ref[...]


def _delta_rule(qkvn, gb, s0, *, bsz, seq):
    rows = DN_CHUNK * DN_CHUNKS_PER_STEP
    nb = seq // rows
    fwd = lambda j: (lambda b, i: (b * nb + i, j))
    bwd = lambda j: (lambda b, i: (b * nb + nb - 1 - i, j))
    blk = lambda m: pl.BlockSpec((rows, DN_WIDTH), m)
    gblk = lambda m: pl.BlockSpec((rows, LANES), m)
    st = pl.BlockSpec((1, 2 * DN_HEADS, DN_HEAD_DIM, DN_HEAD_DIM), lambda b, i: (b, 0, 0, 0))
    return pl.pallas_call(
        functools.partial(_delta_kernel, n_steps=nb, chunks=DN_CHUNKS_PER_STEP),
        out_shape=[jax.ShapeDtypeStruct((bsz * seq, DN_WIDTH), F32),
                   jax.ShapeDtypeStruct((bsz * seq, DN_WIDTH), F32),
                   jax.ShapeDtypeStruct(s0.shape, F32)],
        grid=(bsz, nb),
        in_specs=[blk(fwd(0)), blk(fwd(1)), blk(fwd(2)), gblk(fwd(0)),
                  blk(bwd(0)), blk(bwd(1)), blk(bwd(2)), gblk(bwd(0)), st],
        out_specs=[blk(fwd(0)), blk(bwd(0)), st],
        scratch_shapes=[pltpu.VMEM((2 * DN_HEADS, DN_HEAD_DIM, DN_HEAD_DIM), F32)],
        compiler_params=_cparams(("arbitrary", "arbitrary")),
        name="delta_rule",
    )(qkvn, qkvn, qkvn, gb, qkvn, qkvn, qkvn, gb, s0)


def _s5_disc_kernel(lre_ref, lim_ref, ls_ref, bre_ref, bim_ref, are_ref, aim_ref, bbre_ref, bbim_ref):
    lam_re = lre_ref[...]
    lam_im = lim_ref[...]
    dt = jnp.exp(ls_ref[...])
    mag = jnp.exp(lam_re * dt)
    ang = lam_im * dt
    bar_re = mag * jnp.cos(ang)
    bar_im = mag * jnp.sin(ang)
    den = lam_re * lam_re + lam_im * lam_im
    coef_re = ((bar_re - 1.0) * lam_re + bar_im * lam_im) / den
    coef_im = (bar_im * lam_re - (bar_re - 1.0) * lam_im) / den
    are_ref[...] = bar_re
    aim_ref[...] = bar_im
    bbre_ref[...] = coef_re * bre_ref[...] - coef_im * bim_ref[...]
    bbim_ref[...] = coef_re * bim_ref[...] + coef_im * bre_ref[...]


def _s5_discretise(lam_re, lam_im, log_step, b_re, b_im):
    rep = lambda t: jnp.repeat(t.reshape(2 * S5_GROUPS, -1), S5_GROUP, axis=0)
    rows = 2 * S5_GROUPS * S5_GROUP
    tr = lambda t: jnp.swapaxes(t, -1, -2).reshape(rows, S5_STATE)
    outs = pl.pallas_call(
        _s5_disc_kernel,
        out_shape=[jax.ShapeDtypeStruct((rows, S5_STATE), F32)] * 4,
        name="s5_discretise",
    )(rep(lam_re), rep(lam_im), rep(log_step[..., None]), tr(b_re), tr(b_im))
    shp = (2, S5_GROUPS, S5_GROUP, S5_STATE)
    bar_re, bar_im, bb_re, bb_im = (o.reshape(shp) for o in outs)
    return bar_re[:, :, 0], bar_im[:, :, 0], bb_re, bb_im


def _s5_kernel(uf_ref, ub_ref, wb_ref, wcre_ref, wcim_ref, are_ref, aim_ref, x0_ref, *rest, tt, n_tiles, emit_y):
    if emit_y:
        yf_ref, yb_ref, xout_ref, buf_ref, xst_ref, us_ref, ys_ref = rest
    else:
        xout_ref, buf_ref, xst_ref, us_ref = rest
        yf_ref = yb_ref = ys_ref = None
    i = pl.program_id(0)
    bsz = uf_ref.shape[0]
    hs = S5_HALF_STATE
    n_lane_tiles = S5_WIDTH // LANES
    blk = 512
    u_refs = (uf_ref, ub_ref)
    y_refs = (yf_ref, yb_ref)

    @pl.when(i == 0)
    def _():
        xst_ref[...] = x0_ref[...]

    def input_matmul(hh):
        for d in range(2):
            u_h = jnp.concatenate([us_ref[d, 2 * hh], us_ref[d, 2 * hh + 1]], axis=1)
            buf_ref[d, :, hh * 2 * hs:(hh + 1) * 2 * hs] = _bdot(u_h, wb_ref[d, hh])

    def scan(hh):
        for p in range(hs // blk):
            lr = hh * 2 * hs + p * blk
            li = lr + hs
            a = [(are_ref[d, :, lr:lr + blk], aim_ref[d, :, lr:lr + blk]) for d in range(2)]
            x = [(xst_ref[d, :, lr:lr + blk], xst_ref[d, :, li:li + blk]) for d in range(2)]
            for t in range(tt):
                for d in range(2):
                    r0 = (t if d == 0 else tt - 1 - t) * SUBLANES
                    a_re, a_im = a[d]
                    x_re, x_im = x[d]
                    n_re = a_re * x_re - a_im * x_im + buf_ref[d, r0:r0 + SUBLANES, lr:lr + blk]
                    n_im = a_re * x_im + a_im * x_re + buf_ref[d, r0:r0 + SUBLANES, li:li + blk]
                    buf_ref[d, r0:r0 + SUBLANES, lr:lr + blk] = n_re
                    buf_ref[d, r0:r0 + SUBLANES, li:li + blk] = n_im
                    x[d] = (n_re, n_im)
            for d in range(2):
                xst_ref[d, :, lr:lr + blk] = x[d][0]
                xst_ref[d, :, li:li + blk] = x[d][1]

    def output_matmul(hh):
        for d in range(2):
            x_re = buf_ref[d, :, hh * 2 * hs:hh * 2 * hs + hs]
            x_im = buf_ref[d, :, hh * 2 * hs + hs:(hh + 1) * 2 * hs]
            y_h = _bdot(x_re, wcre_ref[d, hh]) - _bdot(x_im, wcim_ref[d, hh])
            ys_ref[d, 2 * hh] = y_h[:, :LANES]
            ys_ref[d, 2 * hh + 1] = y_h[:, LANES:]

    for d in range(2):
        for b in range(bsz):
            for j in range(n_lane_tiles):
                us_ref[d, j, pl.ds(b, tt, stride=bsz), :] = u_refs[d][b, :, j * LANES:(j + 1) * LANES]
    input_matmul(0)
    input_matmul(1)
    scan(0)
    if emit_y:
        output_matmul(0)
    scan(1)
    if emit_y:
        output_matmul(1)
        for d in range(2):
            for b in range(bsz):
                for j in range(n_lane_tiles):
                    y_refs[d][b, :, j * LANES:(j + 1) * LANES] = ys_ref[d, j, pl.ds(b, tt, stride=bsz), :]

    @pl.when(i == n_tiles - 1)
    def _():
        xout_ref[...] = xst_ref[...]


def _s5_scan(u3, s5w, x0, *, emit_y):
    wb, wc_re, wc_im, a_re, a_im = s5w
    bsz, seq, _ = u3.shape
    assert bsz == SUBLANES, "one time step of all batches must fill one sublane group"
    tt = S5_TILE
    nt = seq // tt
    rows = tt * bsz
    full = lambda a: pl.BlockSpec(a.shape, lambda i: (0,) * a.ndim)
    ublk = lambda m: pl.BlockSpec((bsz, tt, S5_WIDTH), m)
    fwd = lambda i: (0, i, 0)
    bwd = lambda i: (0, nt - 1 - i, 0)
    out_shape = [jax.ShapeDtypeStruct(x0.shape, F32)]
    out_specs = [full(x0)]
    scratch = [pltpu.VMEM((2, rows, S5_LANES), F32), pltpu.VMEM(x0.shape, F32),
               pltpu.VMEM((2, S5_WIDTH // LANES, rows, LANES), F32)]
    if emit_y:
        out_shape = [jax.ShapeDtypeStruct(u3.shape, F32)] * 2 + out_shape
        out_specs = [ublk(fwd), ublk(bwd)] + out_specs
        scratch.append(pltpu.VMEM((2, S5_WIDTH // LANES, rows, LANES), F32))
    return pl.pallas_call(
        functools.partial(_s5_kernel, tt=tt, n_tiles=nt, emit_y=emit_y),
        out_shape=out_shape,
        grid=(nt,),
        in_specs=[ublk(fwd), ublk(bwd), full(wb), full(wc_re), full(wc_im), full(a_re), full(a_im), full(x0)],
        out_specs=out_specs,
        scratch_shapes=scratch,
        compiler_params=_cparams(("arbitrary",)),
        name="s5_scan_y" if emit_y else "s5_scan_state",
    )(u3, u3, wb, wc_re, wc_im, a_re, a_im, x0)


def _s5_weights(bar_re, bar_im, bb_re, bb_im, c_re, c_im, bsz):
    hg = S5_HALF_GROUPS
    eye = jnp.eye(hg, dtype=F32)

    def in_mat(bb):
        t = bb.reshape(2, 2, hg, S5_GROUP, S5_STATE)
        t = jnp.einsum('dhicp,ij->dhicjp', t, eye)
        return t.reshape(2, 2, hg * S5_GROUP, hg * S5_STATE)

    def out_mat(cc):
        t = cc.reshape(2, 2, hg, S5_GROUP, S5_STATE)
        t = jnp.einsum('dhicp,ij->dhipjc', t, eye)
        return t.reshape(2, 2, hg * S5_STATE, hg * S5_GROUP)

    wb = jnp.concatenate([in_mat(bb_re), in_mat(bb_im)], axis=-1).astype(BF16)
    wc_re = out_mat(c_re).astype(BF16)
    wc_im = out_mat(c_im).astype(BF16)

    def lanes(a):
        t = a.reshape(2, 2, 1, S5_HALF_STATE)
        t = jnp.broadcast_to(t, (2, 2, 2, S5_HALF_STATE)).reshape(2, 1, S5_LANES)
        return jnp.broadcast_to(t, (2, bsz, S5_LANES))

    return wb, wc_re, wc_im, lanes(bar_re), lanes(bar_im)


def _merge_kernel(x_ref, mod_ref, of_ref, ob_ref, z_ref, yf_ref, yb_ref, u_ref, gates_ref, onw_ref, waup_ref,
                  s5d_ref, wglu_ref, wout_ref, postw_ref, prew_ref, wr_ref, br_ref,
                  x1_ref, h2_ref, route_ref):
    mod = mod_ref[0]
    o_a = of_ref[...] + ob_ref[...]
    z = z_ref[...]
    parts = []
    for h in range(DN_HEADS):
        hs = slice(h * DN_HEAD_DIM, (h + 1) * DN_HEAD_DIM)
        parts.append(_rms(o_a[:, hs], onw_ref[...]) * _silu(z[:, hs]))
    y_a = _bdot(jnp.concatenate(parts, axis=1), waup_ref[...])
    y5 = yf_ref[...] + yb_ref[...] + s5d_ref[...] * u_ref[...]
    glu = _bdot(jax.nn.gelu(y5, approximate=True), wglu_ref[...])
    y_b = glu[:, :D_MODEL] * jax.nn.sigmoid(glu[:, D_MODEL:])
    gates = gates_ref[...]
    y = jax.nn.sigmoid(gates[:, :D_MODEL]) * y_a + jax.nn.sigmoid(gates[:, D_MODEL:]) * y_b
    mixed = _bdot(y, wout_ref[...])
    x1 = x_ref[...] + mod[2:3, :] * _rms(mixed, postw_ref[...])
    x1_ref[...] = x1
    h2 = _rms(x1, prew_ref[...]) * (1.0 + mod[4:5, :]) + mod[3:4, :]
    _store_token_tiles(h2_ref, h2)

    logits = _bdot(h2, wr_ref[...]) + br_ref[...]
    lane = lax.broadcasted_iota(jnp.int32, logits.shape, 1)
    neg = -jnp.inf
    big = jnp.int32(LANES)
    first = lambda hit: jnp.min(jnp.where(hit, lane, big), axis=-1, keepdims=True)
    g_logit = jnp.where(lane < N_EXPERT_GROUPS, logits, neg)
    g_max = jnp.max(g_logit, axis=-1, keepdims=True)
    g_sel = first(g_logit == g_max)
    g_w = 1.0 / jnp.sum(jnp.exp(g_logit - g_max), axis=-1, keepdims=True)
    e_idx = lane - N_EXPERT_GROUPS
    in_group = (e_idx >= 0) & (e_idx < N_EXPERTS) & ((e_idx // EXPERTS_PER_GROUP) == g_sel)
    e_logit = jnp.where(in_group, logits, neg)
    m1 = jnp.max(e_logit, axis=-1, keepdims=True)
    i1 = first(e_logit == m1)
    e_logit2 = jnp.where(lane == i1, neg, e_logit)
    m2 = jnp.max(e_logit2, axis=-1, keepdims=True)
    i2 = first(e_logit2 == m2)
    e2 = jnp.exp(m2 - m1)
    w1 = g_w / (1.0 + e2)
    w2 = g_w * e2 / (1.0 + e2)
    id1 = (i1 - N_EXPERT_GROUPS).astype(F32)
    id2 = (i2 - N_EXPERT_GROUPS).astype(F32)
    route_ref[...] = jnp.where(lane == 0, id1, jnp.where(lane == 1, id2, jnp.where(lane == 2, w1,
                               jnp.where(lane == 3, w2, 0.0))))


def _merge(x2d, mod3, o_f, o_b, z, y_f, y_b, u, gates, wts, *, bsz, seq):
    onw, waup, s5d, wglu, wout, postw, prew, wr, br = wts
    tm = PROJ_TILE
    nt = seq // tm
    rows = bsz * seq
    row_map = lambda b, i: (b * nt + i, 0)
    const = lambda b, i: (0, 0)
    cs = lambda a: pl.BlockSpec(a.shape, const, pipeline_mode=pl.Buffered(1))
    return pl.pallas_call(
        _merge_kernel,
        out_shape=[jax.ShapeDtypeStruct((rows, D_MODEL), F32),
                   jax.ShapeDtypeStruct((rows * TOKEN_TILE_ROWS, LANES), F32),
                   jax.ShapeDtypeStruct((rows, LANES), F32)],
        grid=(bsz, nt),
        in_specs=[pl.BlockSpec((tm, D_MODEL), row_map),
                  pl.BlockSpec((1, N_MOD, D_MODEL), lambda b, i: (b, 0, 0)),
                  pl.BlockSpec((tm, DN_WIDTH), row_map),
                  pl.BlockSpec((tm, DN_WIDTH), row_map),
                  pl.BlockSpec((tm, DN_WIDTH), row_map),
                  pl.BlockSpec((tm, S5_WIDTH), row_map),
                  pl.BlockSpec((tm, S5_WIDTH), row_map),
                  pl.BlockSpec((tm, S5_WIDTH), row_map),
                  pl.BlockSpec((tm, 2 * D_MODEL), row_map),
                  cs(onw), cs(waup), cs(s5d), cs(wglu), cs(wout), cs(postw), cs(prew), cs(wr), cs(br)],
        out_specs=[pl.BlockSpec((tm, D_MODEL), row_map),
                   pl.BlockSpec((tm * TOKEN_TILE_ROWS, LANES), row_map),
                   pl.BlockSpec((tm, LANES), row_map)],
        compiler_params=_cparams(("arbitrary", "arbitrary")),
        name="merge_router",
    )(x2d, mod3, o_f, o_b, z, y_f, y_b, u, gates, onw, waup, s5d, wglu, wout, postw, prew, wr, br)


def _rank_kernel(route_ref, rank_ref, count_ref, carry_ref):
    i = pl.program_id(0)

    @pl.when(i == 0)
    def _():
        carry_ref[...] = jnp.zeros_like(carry_ref)

    route = route_ref[...]
    r = route.shape[0]
    lane = lax.broadcasted_iota(jnp.int32, route.shape, 1).astype(F32)
    oh0 = (lane == route[:, 0:1]).astype(F32)
    oh1 = (lane == route[:, 1:2]).astype(F32)
    both = oh0 + oh1
    row = lax.broadcasted_iota(jnp.int32, (r, r), 0)
    col = lax.broadcasted_iota(jnp.int32, (r, r), 1)
    before = _bdot((row > col).astype(F32), both) + carry_ref[...]
    rank0 = jnp.sum(oh0 * before, axis=-1, keepdims=True)
    rank1 = jnp.sum(oh1 * before, axis=-1, keepdims=True)
    lane_i = lax.broadcasted_iota(jnp.int32, route.shape, 1)
    rank_ref[...] = jnp.where(lane_i == 0, rank0, jnp.where(lane_i == 1, rank1, 0.0))
    carry_ref[...] = carry_ref[...] + jnp.sum(both, axis=0, keepdims=True)
    count_ref[...] = carry_ref[...]


def _expert_ranks(route):
    rows = route.shape[0]
    r = RANK_TILE
    return pl.pallas_call(
        _rank_kernel,
        out_shape=[jax.ShapeDtypeStruct((rows, LANES), F32), jax.ShapeDtypeStruct((1, LANES), F32)],
        grid=(rows // r,),
        in_specs=[pl.BlockSpec((r, LANES), lambda i: (i, 0))],
        out_specs=[pl.BlockSpec((r, LANES), lambda i: (i, 0)), pl.BlockSpec((1, LANES), lambda i: (0, 0))],
        scratch_shapes=[pltpu.VMEM((1, LANES), F32)],
        compiler_params=_cparams(("arbitrary",)),
        name="expert_ranks",
    )(route)


def _token_copy(src_ref, src_token, dst_ref, dst_token, sem):
    r = TOKEN_TILE_ROWS
    return pltpu.make_async_copy(src_ref.at[pl.ds(pl.multiple_of(src_token * r, r), r), :],
                                 dst_ref.at[pl.ds(dst_token * r, r), :], sem)


def _row_token_kernel(pos_ref, zeros_ref, inv_ref, sem, *, lane_bits):
    i = pl.program_id(0)
    tm = pos_ref.shape[-1] // 2
    base = i * tm

    @pl.when(i == 0)
    def _():
        clear = pltpu.make_async_copy(zeros_ref, inv_ref, sem)
        clear.start()
        clear.wait()

    def body(t, _):
        for s in range(2):
            p = pos_ref[0, 0, s * tm + t]
            inv_ref[lax.shift_right_logical(p, lane_bits), p & (LANES - 1)] = base + t
        return 0

    lax.fori_loop(0, tm, body, 0, unroll=16)


def _row_tokens(pos3, n_rows):
    n_tiles, _, two_tm = pos3.shape
    lane_bits = LANES.bit_length() - 1
    assert 1 << lane_bits == LANES
    shape = (n_rows // LANES, LANES)
    return pl.pallas_call(
        functools.partial(_row_token_kernel, lane_bits=lane_bits),
        out_shape=jax.ShapeDtypeStruct(shape, jnp.int32),
        grid=(n_tiles,),
        in_specs=[pl.BlockSpec((1, 1, two_tm), lambda i: (i, 0, 0), memory_space=pltpu.SMEM),
                  pl.BlockSpec(memory_space=pl.ANY)],
        out_specs=pl.BlockSpec(memory_space=pltpu.SMEM),
        scratch_shapes=[pltpu.SemaphoreType.DMA],
        compiler_params=_cparams(("arbitrary",)),
        name="row_tokens",
    )(pos3, jnp.zeros(shape, jnp.int32))


def _gather_rows(src_ref, idx_ref, n, dst_ref, sem):
    for t in range(n):
        _token_copy(src_ref, idx_ref[0, 0, t], dst_ref, t, sem).start()


def _wait_rows(src_ref, dst_ref, sem):
    pltpu.make_async_copy(src_ref.at[pl.ds(0, dst_ref.shape[0]), :], dst_ref, sem).wait()


def _gather_ring(i, idx_refs, src_ref, n, buf_ref, sems):
    nb = GATHER_BUFFERS

    @pl.when(i == 0)
    def _():
        for k in range(nb - 1):
            _gather_rows(src_ref, idx_refs[k], n, buf_ref.at[k], sems.at[k])

    slot = i % nb
    _wait_rows(src_ref, buf_ref.at[slot], sems.at[slot])
    ahead = (i + nb - 1) % nb
    _gather_rows(src_ref, idx_refs[nb - 1], n, buf_ref.at[ahead], sems.at[ahead])
    return slot


def _gather_drain(i, n_tiles, src_ref, buf_ref, sems):
    @pl.when(i == n_tiles - 1)
    def _():
        for k in range(1, GATHER_BUFFERS):
            s = (i + k) % GATHER_BUFFERS
            _wait_rows(src_ref, buf_ref.at[s], sems.at[s])


def _ring_index_specs(block, n_tiles):
    def spec(k):
        return pl.BlockSpec(block, lambda i, *_: (jnp.minimum(i + k, n_tiles - 1), 0, 0), memory_space=pltpu.SMEM)

    return [spec(k) for k in range(GATHER_BUFFERS)]


def _expert_kernel(te_ref, inv0_ref, inv1_ref, inv2_ref, h2_ref, wg_ref, wu_ref, wd_ref, ys_ref, xbuf_ref, sems,
                   *, n_tiles):
    del te_ref
    i = pl.program_id(0)
    te = ys_ref.shape[0] // TOKEN_TILE_ROWS
    slot = _gather_ring(i, (inv0_ref, inv1_ref, inv2_ref), h2_ref, te, xbuf_ref, sems)
    x = _load_token_tiles(xbuf_ref.at[slot], 0, te).astype(BF16)
    hg = jnp.dot(x, wg_ref[0], preferred_element_type=F32)
    hu = jnp.dot(x, wu_ref[0], preferred_element_type=F32)
    _store_token_tiles(ys_ref, _bdot(_silu(hg) * hu, wd_ref[0]))
    _gather_drain(i, n_tiles, h2_ref, xbuf_ref, sems)


def _expert_mlp(tile_expert, inv3, h2, wg, wu, wd):
    n_tiles, _, te = inv3.shape
    return pl.pallas_call(
        functools.partial(_expert_kernel, n_tiles=n_tiles),
        out_shape=jax.ShapeDtypeStruct((n_tiles * te * TOKEN_TILE_ROWS, LANES), F32),
        grid_spec=pltpu.PrefetchScalarGridSpec(
            num_scalar_prefetch=1,
            grid=(n_tiles,),
            in_specs=_ring_index_specs((1, 1, te), n_tiles) + [
                      pl.BlockSpec(memory_space=pl.ANY),
                      pl.BlockSpec((1, D_MODEL, EXPERT_FF), lambda i, e: (e[i], 0, 0)),
                      pl.BlockSpec((1, D_MODEL, EXPERT_FF), lambda i, e: (e[i], 0, 0)),
                      pl.BlockSpec((1, EXPERT_FF, D_MODEL), lambda i, e: (e[i], 0, 0))],
            out_specs=pl.BlockSpec((te * TOKEN_TILE_ROWS, LANES), lambda i, e: (i, 0)),
            scratch_shapes=[pltpu.VMEM((GATHER_BUFFERS, te * TOKEN_TILE_ROWS, LANES), F32),
                            pltpu.SemaphoreType.DMA((GATHER_BUFFERS,))]),
        compiler_params=_cparams(("arbitrary",)),
        name="expert_mlp",
    )(tile_expert, inv3, inv3, inv3, h2, wg, wu, wd)


def _combine_kernel(pos0_ref, pos1_ref, pos2_ref, x1_ref, mod_ref, route_ref, postw_ref, ys_ref, out_ref, buf_ref,
                    sems, *, n_tiles):
    tm = x1_ref.shape[0]
    i = pl.program_id(0)
    slot = _gather_ring(i, (pos0_ref, pos1_ref, pos2_ref), ys_ref, 2 * tm, buf_ref, sems)
    route = route_ref[...]
    moe = (route[:, 2:3] * _load_token_tiles(buf_ref.at[slot], 0, tm)
           + route[:, 3:4] * _load_token_tiles(buf_ref.at[slot], tm, tm))
    out_ref[...] = x1_ref[...] + mod_ref[0][5:6, :] * _rms(moe, postw_ref[...])
    _gather_drain(i, n_tiles, ys_ref, buf_ref, sems)


def _combine(pos3, x1, mod3, route, postw, ys, *, bsz, seq):
    tm = TOK_TILE
    nt = seq // tm
    n_tiles = bsz * nt
    row_map = lambda i: (i, 0)
    return pl.pallas_call(
        functools.partial(_combine_kernel, n_tiles=n_tiles),
        out_shape=jax.ShapeDtypeStruct(x1.shape, F32),
        grid=(n_tiles,),
        in_specs=_ring_index_specs((1, 1, 2 * tm), n_tiles) + [
                  pl.BlockSpec((tm, D_MODEL), row_map),
                  pl.BlockSpec((1, N_MOD, D_MODEL), lambda i: (i // nt, 0, 0)),
                  pl.BlockSpec((tm, LANES), row_map),
                  pl.BlockSpec((1, D_MODEL), lambda i: (0, 0)),
                  pl.BlockSpec(memory_space=pl.ANY)],
        out_specs=pl.BlockSpec((tm, D_MODEL), row_map),
        scratch_shapes=[pltpu.VMEM((GATHER_BUFFERS, 2 * tm * TOKEN_TILE_ROWS, LANES), F32),
                        pltpu.SemaphoreType.DMA((GATHER_BUFFERS,))],
        compiler_params=_cparams(("arbitrary",)),
        name="combine",
    )(pos3, pos3, pos3, x1, mod3, route, postw, ys)


def _lane_pad(v, n=LANES):
    v = v.reshape(1, -1)
    return jnp.pad(v, ((0, 0), (0, n - v.shape[1])))


def kernel(x, c, ctx, c_ctx, w_mod, b_mod, pre_norm_w, post_norm_w, w_in, conv_w, a_log, dt_bias, o_norm_w,
           w_a_up, s5_lam_re, s5_lam_im, s5_log_step, s5_b_re, s5_b_im, s5_c_re, s5_c_im, s5_d, w_glu, w_out,
           w_router_group, b_router_group, w_router_expert, b_router_expert, w_gate_e, w_up_e, w_down_e):
    assert w_mod.shape[0] == 1, "single-layer block"
    bsz, seq, _ = x.shape
    ctx_len = ctx.shape[1]
    row = lambda v: v.reshape(1, -1)

    c16 = jnp.concatenate([c, c_ctx[None, :], jnp.zeros((2 * SUBLANES - bsz - 1, D_MODEL), F32)], axis=0)
    mod3 = _modulation(c16, w_mod[0], row(b_mod[0])).reshape(2 * SUBLANES, N_MOD, D_MODEL)

    wi = w_in[0]
    o_z, o_a, o_u, o_g = 3 * DN_WIDTH, 4 * DN_WIDTH, 4 * DN_WIDTH + 4 * DN_HEADS, 4 * DN_WIDTH + 4 * DN_HEADS + S5_WIDTH
    wts_in = (wi[:, :o_z].astype(BF16), wi[:, o_z:o_a].astype(BF16),
              jnp.pad(wi[:, o_a:o_u], ((0, 0), (0, LANES - 4 * DN_HEADS))).astype(BF16),
              wi[:, o_u:o_g].astype(BF16), wi[:, o_g:].astype(BF16))
    convw = jnp.pad(conv_w[0], ((0, SUBLANES - CONV_K), (0, 0)))
    alog = _lane_pad(a_log[0])
    dtb = _lane_pad(dt_bias[0])
    prew0, prew1 = row(pre_norm_w[0, 0]), row(pre_norm_w[0, 1])
    postw0, postw1 = row(post_norm_w[0, 0]), row(post_norm_w[0, 1])

    bar_re, bar_im, bb_re, bb_im = _s5_discretise(s5_lam_re[0], s5_lam_im[0], s5_log_step[0], s5_b_re[0], s5_b_im[0])
    s5w = _s5_weights(bar_re, bar_im, bb_re, bb_im, s5_c_re[0], s5_c_im[0], bsz)

    qkvn_c, gb_c, u_c = _input_projection(ctx.reshape(bsz * ctx_len, D_MODEL), mod3, bsz, 0, prew0, wts_in, convw,
                                          alog, dtb, bsz=bsz, seq=ctx_len, seg_len=ctx_len, full=False)
    s_zero = jnp.zeros((bsz, 2 * DN_HEADS, DN_HEAD_DIM, DN_HEAD_DIM), F32)
    _, _, s_ctx = _delta_rule(qkvn_c, gb_c, s_zero, bsz=bsz, seq=ctx_len)
    x_zero = jnp.zeros((2, bsz, S5_LANES), F32)
    (x_ctx,) = _s5_scan(u_c.reshape(bsz, ctx_len, S5_WIDTH), s5w, x_zero, emit_y=False)

    x2d = x.reshape(bsz * seq, D_MODEL)
    qkvn, gb, u, z, gates = _input_projection(x2d, mod3, 0, 1, prew0, wts_in, convw, alog, dtb,
                                                 bsz=bsz, seq=seq, seg_len=GRID_W, full=True)
    o_f, o_b, _ = _delta_rule(qkvn, gb, s_ctx, bsz=bsz, seq=seq)
    y_f, y_b, _ = _s5_scan(u.reshape(bsz, seq, S5_WIDTH), s5w, x_ctx, emit_y=True)
    rows2d = lambda a: a.reshape(bsz * seq, S5_WIDTH)

    w_router = jnp.pad(jnp.concatenate([w_router_group[0], w_router_expert[0]], axis=1),
                       ((0, 0), (0, LANES - N_EXPERT_GROUPS - N_EXPERTS))).astype(BF16)
    b_router = _lane_pad(jnp.concatenate([b_router_group[0], b_router_expert[0]]))
    wts_merge = (row(o_norm_w[0]), w_a_up[0].astype(BF16), row(s5_d[0]), w_glu[0].astype(BF16),
                 w_out[0].astype(BF16), postw0, prew1, w_router, b_router)
    x1, h2, route = _merge(x2d, mod3, o_f, o_b, z, rows2d(y_f), rows2d(y_b), u, gates, wts_merge, bsz=bsz, seq=seq)

    rank, count = _expert_ranks(route)
    n_tok = bsz * seq
    te = EXPERT_TILE
    counts = count[0, :N_EXPERTS].astype(jnp.int32)
    padded = ((counts + te - 1) // te) * te
    ends = jnp.cumsum(padded)
    starts = ends - padded
    ids = route[:, :2].astype(jnp.int32)
    one_hot = ids[:, :, None] == jnp.arange(N_EXPERTS, dtype=jnp.int32)
    pos = jnp.sum(jnp.where(one_hot, starts, 0), axis=-1) + rank[:, :2].astype(jnp.int32)
    pos3 = jnp.swapaxes(pos.reshape(n_tok // TOK_TILE, TOK_TILE, 2), 1, 2).reshape(n_tok // TOK_TILE, 1, 2 * TOK_TILE)
    n_sorted = 2 * n_tok + N_EXPERTS * te
    tile_start = jnp.arange(n_sorted // te, dtype=jnp.int32) * te
    tile_expert = jnp.minimum(jnp.sum((ends[None, :] <= tile_start[:, None]).astype(jnp.int32), axis=1), N_EXPERTS - 1)

    inv3 = _row_tokens(pos3, n_sorted).reshape(n_sorted // te, 1, te)
    ys = _expert_mlp(tile_expert, inv3, h2, w_gate_e[0].astype(BF16), w_up_e[0].astype(BF16),
                     w_down_e[0].astype(BF16))
    out = _combine(pos3, x1, mod3, route, postw1, ys, bsz=bsz, seq=seq)
    return out.reshape(bsz, seq, D_MODEL)
```

```python
import functools

import jax
import jax.numpy as jnp
from jax import lax
from jax.experimental import pallas as pl
from jax.experimental.pallas import tpu as pltpu

F32 = jnp.float32
BF16 = jnp.bfloat16
HIGHEST = lax.Precision.HIGHEST

D_MODEL = 1024
GRID_W = 64
DN_HEADS = 4
DN_HEAD_DIM = 128
DN_WIDTH = DN_HEADS * DN_HEAD_DIM
DN_CHUNK = 64
CONV_K = 5
S5_WIDTH = D_MODEL - DN_WIDTH
S5_GROUP = 16
S5_GROUPS = S5_WIDTH // S5_GROUP
S5_STATE = 64
S5_HALF_GROUPS = S5_GROUPS // 2
S5_HALF_STATE = S5_HALF_GROUPS * S5_STATE
S5_LANES = 2 * 2 * S5_HALF_STATE
N_EXPERT_GROUPS = 4
EXPERTS_PER_GROUP = 8
N_EXPERTS = N_EXPERT_GROUPS * EXPERTS_PER_GROUP
EXPERT_FF = 512
NORM_EPS = 1e-6
L2_EPS = 1e-6
N_MOD = 6
LANES = 128
SUBLANES = 8
TOKEN_TILE_ROWS = D_MODEL // LANES
VMEM_LIMIT = 56 * 1024 * 1024

TOK_TILE = 256
PROJ_TILE = 512
S5_TILE = 64
DN_CHUNKS_PER_STEP = 4
EXPERT_TILE = 512
RANK_TILE = 512
GATHER_BUFFERS = 3


def _cparams(sem):
    return pltpu.CompilerParams(dimension_semantics=sem, vmem_limit_bytes=VMEM_LIMIT)


def _bdot(a, b):
    return jnp.dot(a.astype(BF16), b.astype(BF16), preferred_element_type=F32)


def _bdot_nt(a, b):
    return lax.dot_general(a.astype(BF16), b.astype(BF16), (((1,), (1,)), ((), ())),
                           preferred_element_type=F32)


def _bdot_tn(a, b):
    return lax.dot_general(a.astype(BF16), b.astype(BF16), (((0,), (0,)), ((), ())),
                           preferred_element_type=F32)


def _silu(x):
    return x * jax.nn.sigmoid(x)


def _rms(x, w):
    return x * lax.rsqrt(jnp.mean(x * x, axis=-1, keepdims=True) + NORM_EPS) * w


def _store_token_tiles(ref, val):
    n = val.shape[0]
    for j in range(TOKEN_TILE_ROWS):
        ref[pl.ds(j, n, stride=TOKEN_TILE_ROWS), :] = val[:, j * LANES:(j + 1) * LANES]


def _load_token_tiles(ref, first_token, n):
    return jnp.concatenate([ref[pl.ds(first_token * TOKEN_TILE_ROWS + j, n, stride=TOKEN_TILE_ROWS), :]
                            for j in range(TOKEN_TILE_ROWS)], axis=1)


def _mod_kernel(c_ref, w_ref, b_ref, o_ref):
    o_ref[...] = jnp.dot(_silu(c_ref[...]), w_ref[...], precision=HIGHEST,
                         preferred_element_type=F32) + b_ref[...]


def _modulation(c16, w_mod, b_mod):
    n = w_mod.shape[1]
    return pl.pallas_call(
        _mod_kernel,
        out_shape=jax.ShapeDtypeStruct((c16.shape[0], n), F32),
        grid=(n // D_MODEL,),
        in_specs=[pl.BlockSpec(c16.shape, lambda j: (0, 0)),
                  pl.BlockSpec((D_MODEL, D_MODEL), lambda j: (0, j)),
                  pl.BlockSpec((1, D_MODEL), lambda j: (0, j))],
        out_specs=pl.BlockSpec((c16.shape[0], D_MODEL), lambda j: (0, j)),
        compiler_params=_cparams(("arbitrary",)),
        name="modulation",
    )(c16, w_mod, b_mod)


def _inproj_kernel(x_ref, mod_ref, prew_ref, wqkv_ref, wz_ref, wab_ref, wu_ref, wg_ref, convw_ref,
                   alog_ref, dtb_ref, *out_refs, seg_len, full):
    if full:
        qkvn_ref, gb_ref, u_ref, z_ref, gates_ref = out_refs
    else:
        qkvn_ref, gb_ref, u_ref = out_refs
    tm = x_ref.shape[0]
    mod = mod_ref[0]
    h = _rms(x_ref[...], prew_ref[...]) * (1.0 + mod[1:2, :]) + mod[0:1, :]
    hb = h.astype(BF16)

    qkv = jnp.dot(hb, wqkv_ref[...], preferred_element_type=F32)
    pos = lax.broadcasted_iota(jnp.int32, (tm, 1), 0) % seg_len
    acc = qkv * convw_ref[CONV_K // 2:CONV_K // 2 + 1, :]
    for k in range(CONV_K):
        s = k - CONV_K // 2
        if s == 0:
            continue
        shifted = pltpu.roll(qkv, (-s) % tm, 0)
        ok = (pos + s >= 0) & (pos + s < seg_len)
        acc = acc + jnp.where(ok, shifted, 0.0) * convw_ref[k:k + 1, :]
    act = _silu(acc)
    for j in range(3 * DN_HEADS):
        sl = slice(j * DN_HEAD_DIM, (j + 1) * DN_HEAD_DIM)
        a = act[:, sl]
        if j < 2 * DN_HEADS:
            a = a * lax.rsqrt(jnp.sum(a * a, axis=-1, keepdims=True) + L2_EPS)
        qkvn_ref[:, sl] = a

    ab = jnp.dot(hb, wab_ref[...], preferred_element_type=F32)
    xa = ab + dtb_ref[...]
    softplus = jnp.maximum(xa, 0.0) + jnp.log1p(jnp.exp(-jnp.abs(xa)))
    g = -jnp.exp(alog_ref[...]) * softplus
    lane = lax.broadcasted_iota(jnp.int32, ab.shape, 1)
    gb_ref[...] = jnp.where(lane < 2 * DN_HEADS, g, jax.nn.sigmoid(ab))

    u_ref[...] = jnp.dot(hb, wu_ref[...], preferred_element_type=F32)
    if full:
        z_ref[...] = jnp.dot(hb, wz_ref[...], preferred_element_type=F32)
        gates_ref[...] = jnp.dot(hb, wg_ref[...], preferred_element_type=F32)


def _input_projection(x2d, mod3, mod_row0, mod_row_stride, prew, wts, convw, alog, dtb, *, bsz, seq, seg_len, full):
    wqkv, wz, wab, wu, wg = wts
    tm = min(PROJ_TILE, seq)
    nt = seq // tm
    rows = bsz * seq
    row_map = lambda b, i: (b * nt + i, 0)
    const = lambda b, i: (0, 0)
    resident = lambda a: pl.BlockSpec(a.shape, const, pipeline_mode=pl.Buffered(1))
    out_shape = [jax.ShapeDtypeStruct((rows, 3 * DN_WIDTH), F32),
                 jax.ShapeDtypeStruct((rows, LANES), F32),
                 jax.ShapeDtypeStruct((rows, S5_WIDTH), F32)]
    out_specs = [pl.BlockSpec((tm, 3 * DN_WIDTH), row_map),
                 pl.BlockSpec((tm, LANES), row_map),
                 pl.BlockSpec((tm, S5_WIDTH), row_map)]
    if full:
        out_shape += [jax.ShapeDtypeStruct((rows, DN_WIDTH), F32),
                      jax.ShapeDtypeStruct((rows, 2 * D_MODEL), F32)]
        out_specs += [pl.BlockSpec((tm, DN_WIDTH), row_map),
                      pl.BlockSpec((tm, 2 * D_MODEL), row_map)]
    return pl.pallas_call(
        functools.partial(_inproj_kernel, seg_len=seg_len, full=full),
        out_shape=out_shape,
        grid=(bsz, nt),
        in_specs=[pl.BlockSpec((tm, D_MODEL), row_map),
                  pl.BlockSpec((1, N_MOD, D_MODEL), lambda b, i: (mod_row0 + mod_row_stride * b, 0, 0)),
                  pl.BlockSpec((1, D_MODEL), const),
                  resident(wqkv), resident(wz), resident(wab), resident(wu), resident(wg),
                  pl.BlockSpec(convw.shape, const),
                  pl.BlockSpec((1, LANES), const),
                  pl.BlockSpec((1, LANES), const)],
        out_specs=out_specs,
        compiler_params=_cparams(("arbitrary", "arbitrary")),
        name="input_projection_full" if full else "input_projection_ctx",
    )(x2d, mod3, prew, wqkv, wz, wab, wu, wg, convw, alog, dtb)


def _delta_kernel(qf_ref, kf_ref, vf_ref, gbf_ref, qb_ref, kb_ref, vb_ref, gbb_ref, s0_ref,
                  of_ref, ob_ref, sout_ref, state_ref, *, n_steps, chunks):
    c = DN_CHUNK
    i = pl.program_id(1)

    @pl.when(i == 0)
    def _():
        state_ref[...] = s0_ref[0]

    row = lax.broadcasted_iota(jnp.int32, (c, c), 0)
    col = lax.broadcasted_iota(jnp.int32, (c, c), 1)
    eye = (row == col).astype(F32)
    scale = DN_HEAD_DIM ** -0.5
    dirs = ((qf_ref, kf_ref, vf_ref, gbf_ref, of_ref), (qb_ref, kb_ref, vb_ref, gbb_ref, ob_ref))
    chains = {}
    for d, (q_ref, k_ref, v_ref, gb_ref, o_ref) in enumerate(dirs):
        incl = (row >= col) if d == 0 else (row <= col)
        strict = (row > col) if d == 0 else (row < col)
        last = c - 1 if d == 0 else 0
        for sub in range(chunks):
            rs = slice(sub * c, (sub + 1) * c)
            gb = gb_ref[rs, :]
            gcum = jnp.dot(incl.astype(F32), gb, precision=HIGHEST, preferred_element_type=F32)
            gcum_t = gcum.T
            for h in range(DN_HEADS):
                lane = d * DN_HEADS + h
                hs = slice(h * DN_HEAD_DIM, (h + 1) * DN_HEAD_DIM)
                g_col = gcum[:, lane:lane + 1]
                g_row = gcum_t[lane:lane + 1, :]
                g_last = gcum[last:last + 1, lane:lane + 1]
                beta = gb[:, 2 * DN_HEADS + lane:2 * DN_HEADS + lane + 1]
                dec = jnp.where(incl, jnp.exp(jnp.where(incl, g_col - g_row, 0.0)), 0.0)
                q = q_ref[rs, hs] * scale
                k = k_ref[rs, hs]
                v = v_ref[rs, hs]
                k_beta = k * beta
                e_g = jnp.exp(g_col)
                chains[d, sub, h] = dict(
                    lane=lane, rs=rs, hs=hs, o_ref=o_ref, strict=strict, dec=dec, q=q, k=k, k_beta=k_beta, e_g=e_g,
                    rhs=jnp.concatenate([v * beta, k_beta * e_g], axis=1),
                    k_tail=k * jnp.exp(g_last - g_col), e_last=jnp.exp(g_last))
    every = list(chains.values())
    for ch in every:
        kq = _bdot_nt(jnp.concatenate([ch['k_beta'], ch['q']], axis=0), ch['k'])
        ch['npow'] = -jnp.where(ch['strict'], kq[:c] * ch['dec'], 0.0)
        ch['attn'] = kq[c:] * ch['dec']
        ch['t_inv'] = eye + ch['npow']
    for _ in range(5):
        for ch in every:
            ch['npow'] = _bdot(ch['npow'], ch['npow'])
        for ch in every:
            ch['t_inv'] = ch['t_inv'] + _bdot(ch['t_inv'], ch['npow'])
    for ch in every:
        ch['sol'] = _bdot(ch['t_inv'], ch['rhs'])
        ch['q_dec'] = ch['q'] * ch['e_g']

    states = [state_ref[lane] for lane in range(2 * DN_HEADS)]
    for step in range(chunks):
        group = [chains[d, step if d == 0 else chunks - 1 - step, h] for d in range(2) for h in range(DN_HEADS)]
        for ch in group:
            ch['wq'] = _bdot(jnp.concatenate([ch['sol'][:, DN_HEAD_DIM:], ch['q_dec']], axis=0), states[ch['lane']])
        for ch in group:
            ch['v_new'] = ch['sol'][:, :DN_HEAD_DIM] - ch['wq'][:c]
            ch['o_ref'][ch['rs'], ch['hs']] = ch['wq'][c:] + _bdot(ch['attn'], ch['v_new'])
        for ch in group:
            states[ch['lane']] = states[ch['lane']] * ch['e_last'] + _bdot_tn(ch['k_tail'], ch['v_new'])
    for lane in range(2 * DN_HEADS):
        state_ref[lane] = states[lane]

    @pl.when(i == n_steps - 1)
    def _():
        sout_ref[0] = state_ref[...]


def _delta_rule(qkvn, gb, s0, *, bsz, seq):
    rows = DN_CHUNK * DN_CHUNKS_PER_STEP
    nb = seq // rows
    fwd = lambda j: (lambda b, i: (b * nb + i, j))
    bwd = lambda j: (lambda b, i: (b * nb + nb - 1 - i, j))
    blk = lambda m: pl.BlockSpec((rows, DN_WIDTH), m)
    gblk = lambda m: pl.BlockSpec((rows, LANES), m)
    st = pl.BlockSpec((1, 2 * DN_HEADS, DN_HEAD_DIM, DN_HEAD_DIM), lambda b, i: (b, 0, 0, 0))
    return pl.pallas_call(
        functools.partial(_delta_kernel, n_steps=nb, chunks=DN_CHUNKS_PER_STEP),
        out_shape=[jax.ShapeDtypeStruct((bsz * seq, DN_WIDTH), F32),
                   jax.ShapeDtypeStruct((bsz * seq, DN_WIDTH), F32),
                   jax.ShapeDtypeStruct(s0.shape, F32)],
        grid=(bsz, nb),
        in_specs=[blk(fwd(0)), blk(fwd(1)), blk(fwd(2)), gblk(fwd(0)),
                  blk(bwd(0)), blk(bwd(1)), blk(bwd(2)), gblk(bwd(0)), st],
        out_specs=[blk(fwd(0)), blk(bwd(0)), st],
        scratch_shapes=[pltpu.VMEM((2 * DN_HEADS, DN_HEAD_DIM, DN_HEAD_DIM), F32)],
        compiler_params=_cparams(("arbitrary", "arbitrary")),
        name="delta_rule",
    )(qkvn, qkvn, qkvn, gb, qkvn, qkvn, qkvn, gb, s0)


def _s5_disc_kernel(lre_ref, lim_ref, ls_ref, bre_ref, bim_ref, are_ref, aim_ref, bbre_ref, bbim_ref):
    lam_re = lre_ref[...]
    lam_im = lim_ref[...]
    dt = jnp.exp(ls_ref[...])
    mag = jnp.exp(lam_re * dt)
    ang = lam_im * dt
    bar_re = mag * jnp.cos(ang)
    bar_im = mag * jnp.sin(ang)
    den = lam_re * lam_re + lam_im * lam_im
    coef_re = ((bar_re - 1.0) * lam_re + bar_im * lam_im) / den
    coef_im = (bar_im * lam_re - (bar_re - 1.0) * lam_im) / den
    are_ref[...] = bar_re
    aim_ref[...] = bar_im
    bbre_ref[...] = coef_re * bre_ref[...] - coef_im * bim_ref[...]
    bbim_ref[...] = coef_re * bim_ref[...] + coef_im * bre_ref[...]


def _s5_discretise(lam_re, lam_im, log_step, b_re, b_im):
    rep = lambda t: jnp.repeat(t.reshape(2 * S5_GROUPS, -1), S5_GROUP, axis=0)
    rows = 2 * S5_GROUPS * S5_GROUP
    tr = lambda t: jnp.swapaxes(t, -1, -2).reshape(rows, S5_STATE)
    outs = pl.pallas_call(
        _s5_disc_kernel,
        out_shape=[jax.ShapeDtypeStruct((rows, S5_STATE), F32)] * 4,
        name="s5_discretise",
    )(rep(lam_re), rep(lam_im), rep(log_step[..., None]), tr(b_re), tr(b_im))
    shp = (2, S5_GROUPS, S5_GROUP, S5_STATE)
    bar_re, bar_im, bb_re, bb_im = (o.reshape(shp) for o in outs)
    return bar_re[:, :, 0], bar_im[:, :, 0], bb_re, bb_im


def _s5_kernel(uf_ref, ub_ref, wb_ref, wcre_ref, wcim_ref, are_ref, aim_ref, x0_ref, *rest, tt, n_tiles, emit_y):
    if emit_y:
        yf_ref, yb_ref, xout_ref, buf_ref, xst_ref, us_ref, ys_ref = rest
    else:
        xout_ref, buf_ref, xst_ref, us_ref = rest
        yf_ref = yb_ref = ys_ref = None
    i = pl.program_id(0)
    bsz = uf_ref.shape[0]
    hs = S5_HALF_STATE
    n_lane_tiles = S5_WIDTH // LANES
    blk = 512
    u_refs = (uf_ref, ub_ref)
    y_refs = (yf_ref, yb_ref)

    @pl.when(i == 0)
    def _():
        xst_ref[...] = x0_ref[...]

    def input_matmul(hh):
        for d in range(2):
            u_h = jnp.concatenate([us_ref[d, 2 * hh], us_ref[d, 2 * hh + 1]], axis=1)
            buf_ref[d, :, hh * 2 * hs:(hh + 1) * 2 * hs] = _bdot(u_h, wb_ref[d, hh])

    def scan(hh):
        for p in range(hs // blk):
            lr = hh * 2 * hs + p * blk
            li = lr + hs
            a = [(are_ref[d, :, lr:lr + blk], aim_ref[d, :, lr:lr + blk]) for d in range(2)]
            x = [(xst_ref[d, :, lr:lr + blk], xst_ref[d, :, li:li + blk]) for d in range(2)]
            for t in range(tt):
                for d in range(2):
                    r0 = (t if d == 0 else tt - 1 - t) * SUBLANES
                    a_re, a_im = a[d]
                    x_re, x_im = x[d]
                    n_re = a_re * x_re - a_im * x_im + buf_ref[d, r0:r0 + SUBLANES, lr:lr + blk]
                    n_im = a_re * x_im + a_im * x_re + buf_ref[d, r0:r0 + SUBLANES, li:li + blk]
                    buf_ref[d, r0:r0 + SUBLANES, lr:lr + blk] = n_re
                    buf_ref[d, r0:r0 + SUBLANES, li:li + blk] = n_im
                    x[d] = (n_re, n_im)
            for d in range(2):
                xst_ref[d, :, lr:lr + blk] = x[d][0]
                xst_ref[d, :, li:li + blk] = x[d][1]

    def output_matmul(hh):
        for d in range(2):
            x_re = buf_ref[d, :, hh * 2 * hs:hh * 2 * hs + hs]
            x_im = buf_ref[d, :, hh * 2 * hs + hs:(hh + 1) * 2 * hs]
            y_h = _bdot(x_re, wcre_ref[d, hh]) - _bdot(x_im, wcim_ref[d, hh])
            ys_ref[d, 2 * hh] = y_h[:, :LANES]
            ys_ref[d, 2 * hh + 1] = y_h[:, LANES:]

    for d in range(2):
        for b in range(bsz):
            for j in range(n_lane_tiles):
                us_ref[d, j, pl.ds(b, tt, stride=bsz), :] = u_refs[d][b, :, j * LANES:(j + 1) * LANES]
    input_matmul(0)
    input_matmul(1)
    scan(0)
    if emit_y:
        output_matmul(0)
    scan(1)
    if emit_y:
        output_matmul(1)
        for d in range(2):
            for b in range(bsz):
                for j in range(n_lane_tiles):
                    y_refs[d][b, :, j * LANES:(j + 1) * LANES] = ys_ref[d, j, pl.ds(b, tt, stride=bsz), :]

    @pl.when(i == n_tiles - 1)
    def _():
        xout_ref[...] = xst_ref[...]


def _s5_scan(u3, s5w, x0, *, emit_y):
    wb, wc_re, wc_im, a_re, a_im = s5w
    bsz, seq, _ = u3.shape
    assert bsz == SUBLANES, "one time step of all batches must fill one sublane group"
    tt = S5_TILE
    nt = seq // tt
    rows = tt * bsz
    full = lambda a: pl.BlockSpec(a.shape, lambda i: (0,) * a.ndim)
    ublk = lambda m: pl.BlockSpec((bsz, tt, S5_WIDTH), m)
    fwd = lambda i: (0, i, 0)
    bwd = lambda i: (0, nt - 1 - i, 0)
    out_shape = [jax.ShapeDtypeStruct(x0.shape, F32)]
    out_specs = [full(x0)]
    scratch = [pltpu.VMEM((2, rows, S5_LANES), F32), pltpu.VMEM(x0.shape, F32),
               pltpu.VMEM((2, S5_WIDTH // LANES, rows, LANES), F32)]
    if emit_y:
        out_shape = [jax.ShapeDtypeStruct(u3.shape, F32)] * 2 + out_shape
        out_specs = [ublk(fwd), ublk(bwd)] + out_specs
        scratch.append(pltpu.VMEM((2, S5_WIDTH // LANES, rows, LANES), F32))
    return pl.pallas_call(
        functools.partial(_s5_kernel, tt=tt, n_tiles=nt, emit_y=emit_y),
        out_shape=out_shape,
        grid=(nt,),
        in_specs=[ublk(fwd), ublk(bwd), full(wb), full(wc_re), full(wc_im), full(a_re), full(a_im), full(x0)],
        out_specs=out_specs,
        scratch_shapes=scratch,
        compiler_params=_cparams(("arbitrary",)),
        name="s5_scan_y" if emit_y else "s5_scan_state",
    )(u3, u3, wb, wc_re, wc_im, a_re, a_im, x0)


def _s5_weights(bar_re, bar_im, bb_re, bb_im, c_re, c_im, bsz):
    hg = S5_HALF_GROUPS
    eye = jnp.eye(hg, dtype=F32)

    def in_mat(bb):
        t = bb.reshape(2, 2, hg, S5_GROUP, S5_STATE)
        t = jnp.einsum('dhicp,ij->dhicjp', t, eye)
        return t.reshape(2, 2, hg * S5_GROUP, hg * S5_STATE)

    def out_mat(cc):
        t = cc.reshape(2, 2, hg, S5_GROUP, S5_STATE)
        t = jnp.einsum('dhicp,ij->dhipjc', t, eye)
        return t.reshape(2, 2, hg * S5_STATE, hg * S5_GROUP)

    wb = jnp.concatenate([in_mat(bb_re), in_mat(bb_im)], axis=-1).astype(BF16)
    wc_re = out_mat(c_re).astype(BF16)
    wc_im = out_mat(c_im).astype(BF16)

    def lanes(a):
        t = a.reshape(2, 2, 1, S5_HALF_STATE)
        t = jnp.broadcast_to(t, (2, 2, 2, S5_HALF_STATE)).reshape(2, 1, S5_LANES)
        return jnp.broadcast_to(t, (2, bsz, S5_LANES))

    return wb, wc_re, wc_im, lanes(bar_re), lanes(bar_im)


def _merge_kernel(x_ref, mod_ref, of_ref, ob_ref, z_ref, yf_ref, yb_ref, u_ref, gates_ref, onw_ref, waup_ref,
                  s5d_ref, wglu_ref, wout_ref, postw_ref, prew_ref, wr_ref, br_ref,
                  x1_ref, h2_ref, route_ref):
    mod = mod_ref[0]
    o_a = of_ref[...] + ob_ref[...]
    z = z_ref[...]
    parts = []
    for h in range(DN_HEADS):
        hs = slice(h * DN_HEAD_DIM, (h + 1) * DN_HEAD_DIM)
        parts.append(_rms(o_a[:, hs], onw_ref[...]) * _silu(z[:, hs]))
    y_a = _bdot(jnp.concatenate(parts, axis=1), waup_ref[...])
    y5 = yf_ref[...] + yb_ref[...] + s5d_ref[...] * u_ref[...]
    glu = _bdot(jax.nn.gelu(y5, approximate=True), wglu_ref[...])
    y_b = glu[:, :D_MODEL] * jax.nn.sigmoid(glu[:, D_MODEL:])
    gates = gates_ref[...]
    y = jax.nn.sigmoid(gates[:, :D_MODEL]) * y_a + jax.nn.sigmoid(gates[:, D_MODEL:]) * y_b
    mixed = _bdot(y, wout_ref[...])
    x1 = x_ref[...] + mod[2:3, :] * _rms(mixed, postw_ref[...])
    x1_ref[...] = x1
    h2 = _rms(x1, prew_ref[...]) * (1.0 + mod[4:5, :]) + mod[3:4, :]
    _store_token_tiles(h2_ref, h2)

    logits = _bdot(h2, wr_ref[...]) + br_ref[...]
    lane = lax.broadcasted_iota(jnp.int32, logits.shape, 1)
    neg = -jnp.inf
    big = jnp.int32(LANES)
    first = lambda hit: jnp.min(jnp.where(hit, lane, big), axis=-1, keepdims=True)
    g_logit = jnp.where(lane < N_EXPERT_GROUPS, logits, neg)
    g_max = jnp.max(g_logit, axis=-1, keepdims=True)
    g_sel = first(g_logit == g_max)
    g_w = 1.0 / jnp.sum(jnp.exp(g_logit - g_max), axis=-1, keepdims=True)
    e_idx = lane - N_EXPERT_GROUPS
    in_group = (e_idx >= 0) & (e_idx < N_EXPERTS) & ((e_idx // EXPERTS_PER_GROUP) == g_sel)
    e_logit = jnp.where(in_group, logits, neg)
    m1 = jnp.max(e_logit, axis=-1, keepdims=True)
    i1 = first(e_logit == m1)
    e_logit2 = jnp.where(lane == i1, neg, e_logit)
    m2 = jnp.max(e_logit2, axis=-1, keepdims=True)
    i2 = first(e_logit2 == m2)
    e2 = jnp.exp(m2 - m1)
    w1 = g_w / (1.0 + e2)
    w2 = g_w * e2 / (1.0 + e2)
    id1 = (i1 - N_EXPERT_GROUPS).astype(F32)
    id2 = (i2 - N_EXPERT_GROUPS).astype(F32)
    route_ref[...] = jnp.where(lane == 0, id1, jnp.where(lane == 1, id2, jnp.where(lane == 2, w1,
                               jnp.where(lane == 3, w2, 0.0))))


def _merge(x2d, mod3, o_f, o_b, z, y_f, y_b, u, gates, wts, *, bsz, seq):
    onw, waup, s5d, wglu, wout, postw, prew, wr, br = wts
    tm = PROJ_TILE
    nt = seq // tm
    rows = bsz * seq
    row_map = lambda b, i: (b * nt + i, 0)
    const = lambda b, i: (0, 0)
    cs = lambda a: pl.BlockSpec(a.shape, const, pipeline_mode=pl.Buffered(1))
    return pl.pallas_call(
        _merge_kernel,
        out_shape=[jax.ShapeDtypeStruct((rows, D_MODEL), F32),
                   jax.ShapeDtypeStruct((rows * TOKEN_TILE_ROWS, LANES), F32),
                   jax.ShapeDtypeStruct((rows, LANES), F32)],
        grid=(bsz, nt),
        in_specs=[pl.BlockSpec((tm, D_MODEL), row_map),
                  pl.BlockSpec((1, N_MOD, D_MODEL), lambda b, i: (b, 0, 0)),
                  pl.BlockSpec((tm, DN_WIDTH), row_map),
                  pl.BlockSpec((tm, DN_WIDTH), row_map),
                  pl.BlockSpec((tm, DN_WIDTH), row_map),
                  pl.BlockSpec((tm, S5_WIDTH), row_map),
                  pl.BlockSpec((tm, S5_WIDTH), row_map),
                  pl.BlockSpec((tm, S5_WIDTH), row_map),
                  pl.BlockSpec((tm, 2 * D_MODEL), row_map),
                  cs(onw), cs(waup), cs(s5d), cs(wglu), cs(wout), cs(postw), cs(prew), cs(wr), cs(br)],
        out_specs=[pl.BlockSpec((tm, D_MODEL), row_map),
                   pl.BlockSpec((tm * TOKEN_TILE_ROWS, LANES), row_map),
                   pl.BlockSpec((tm, LANES), row_map)],
        compiler_params=_cparams(("arbitrary", "arbitrary")),
        name="merge_router",
    )(x2d, mod3, o_f, o_b, z, y_f, y_b, u, gates, onw, waup, s5d, wglu, wout, postw, prew, wr, br)


def _rank_kernel(route_ref, rank_ref, count_ref, carry_ref):
    i = pl.program_id(0)

    @pl.when(i == 0)
    def _():
        carry_ref[...] = jnp.zeros_like(carry_ref)

    route = route_ref[...]
    r = route.shape[0]
    lane = lax.broadcasted_iota(jnp.int32, route.shape, 1).astype(F32)
    oh0 = (lane == route[:, 0:1]).astype(F32)
    oh1 = (lane == route[:, 1:2]).astype(F32)
    both = oh0 + oh1
    row = lax.broadcasted_iota(jnp.int32, (r, r), 0)
    col = lax.broadcasted_iota(jnp.int32, (r, r), 1)
    before = _bdot((row > col).astype(F32), both) + carry_ref[...]
    rank0 = jnp.sum(oh0 * before, axis=-1, keepdims=True)
    rank1 = jnp.sum(oh1 * before, axis=-1, keepdims=True)
    lane_i = lax.broadcasted_iota(jnp.int32, route.shape, 1)
    rank_ref[...] = jnp.where(lane_i == 0, rank0, jnp.where(lane_i == 1, rank1, 0.0))
    carry_ref[...] = carry_ref[...] + jnp.sum(both, axis=0, keepdims=True)
    count_ref[...] = carry_ref[...]


def _expert_ranks(route):
    rows = route.shape[0]
    r = RANK_TILE
    return pl.pallas_call(
        _rank_kernel,
        out_shape=[jax.ShapeDtypeStruct((rows, LANES), F32), jax.ShapeDtypeStruct((1, LANES), F32)],
        grid=(rows // r,),
        in_specs=[pl.BlockSpec((r, LANES), lambda i: (i, 0))],
        out_specs=[pl.BlockSpec((r, LANES), lambda i: (i, 0)), pl.BlockSpec((1, LANES), lambda i: (0, 0))],
        scratch_shapes=[pltpu.VMEM((1, LANES), F32)],
        compiler_params=_cparams(("arbitrary",)),
        name="expert_ranks",
    )(route)


def _token_copy(src_ref, src_token, dst_ref, dst_token, sem):
    r = TOKEN_TILE_ROWS
    return pltpu.make_async_copy(src_ref.at[pl.ds(pl.multiple_of(src_token * r, r), r), :],
                                 dst_ref.at[pl.ds(dst_token * r, r), :], sem)


def _row_token_kernel(pos_ref, zeros_ref, inv_ref, sem, *, lane_bits):
    i = pl.program_id(0)
    tm = pos_ref.shape[-1] // 2
    base = i * tm

    @pl.when(i == 0)
    def _():
        clear = pltpu.make_async_copy(zeros_ref, inv_ref, sem)
        clear.start()
        clear.wait()

    def body(t, _):
        for s in range(2):
            p = pos_ref[0, 0, s * tm + t]
            inv_ref[lax.shift_right_logical(p, lane_bits), p & (LANES - 1)] = base + t
        return 0

    lax.fori_loop(0, tm, body, 0, unroll=16)


def _row_tokens(pos3, n_rows):
    n_tiles, _, two_tm = pos3.shape
    lane_bits = LANES.bit_length() - 1
    assert 1 << lane_bits == LANES
    shape = (n_rows // LANES, LANES)
    return pl.pallas_call(
        functools.partial(_row_token_kernel, lane_bits=lane_bits),
        out_shape=jax.ShapeDtypeStruct(shape, jnp.int32),
        grid=(n_tiles,),
        in_specs=[pl.BlockSpec((1, 1, two_tm), lambda i: (i, 0, 0), memory_space=pltpu.SMEM),
                  pl.BlockSpec(memory_space=pl.ANY)],
        out_specs=pl.BlockSpec(memory_space=pltpu.SMEM),
        scratch_shapes=[pltpu.SemaphoreType.DMA],
        compiler_params=_cparams(("arbitrary",)),
        name="row_tokens",
    )(pos3, jnp.zeros(shape, jnp.int32))


def _gather_rows(src_ref, idx_ref, n, dst_ref, sem):
    for t in range(n):
        _token_copy(src_ref, idx_ref[0, 0, t], dst_ref, t, sem).start()


def _wait_rows(src_ref, dst_ref, sem):
    pltpu.make_async_copy(src_ref.at[pl.ds(0, dst_ref.shape[0]), :], dst_ref, sem).wait()


def _gather_ring(i, idx_refs, src_ref, n, buf_ref, sems):
    nb = GATHER_BUFFERS

    @pl.when(i == 0)
    def _():
        for k in range(nb - 1):
            _gather_rows(src_ref, idx_refs[k], n, buf_ref.at[k], sems.at[k])

    slot = i % nb
    _wait_rows(src_ref, buf_ref.at[slot], sems.at[slot])
    ahead = (i + nb - 1) % nb
    _gather_rows(src_ref, idx_refs[nb - 1], n, buf_ref.at[ahead], sems.at[ahead])
    return slot


def _gather_drain(i, n_tiles, src_ref, buf_ref, sems):
    @pl.when(i == n_tiles - 1)
    def _():
        for k in range(1, GATHER_BUFFERS):
            s = (i + k) % GATHER_BUFFERS
            _wait_rows(src_ref, buf_ref.at[s], sems.at[s])


def _ring_index_specs(block, n_tiles):
    def spec(k):
        return pl.BlockSpec(block, lambda i, *_: (jnp.minimum(i + k, n_tiles - 1), 0, 0), memory_space=pltpu.SMEM)

    return [spec(k) for k in range(GATHER_BUFFERS)]


def _sorted_rows_kernel(inv0_ref, inv1_ref, inv2_ref, h2_ref, xs_ref, buf_ref, sems, *, n_tiles):
    i = pl.program_id(0)
    te = xs_ref.shape[0] // TOKEN_TILE_ROWS
    slot = _gather_ring(i, (inv0_ref, inv1_ref, inv2_ref), h2_ref, te, buf_ref, sems)
    xs_ref[...] = buf_ref[slot]
    _gather_drain(i, n_tiles, h2_ref, buf_ref, sems)


def _sorted_rows(inv3, h2):
    n_tiles, _, te = inv3.shape
    rows = te * TOKEN_TILE_ROWS
    return pl.pallas_call(
        functools.partial(_sorted_rows_kernel, n_tiles=n_tiles),
        out_shape=jax.ShapeDtypeStruct((n_tiles * rows, LANES), F32),
        grid=(n_tiles,),
        in_specs=_ring_index_specs((1, 1, te), n_tiles) + [pl.BlockSpec(memory_space=pl.ANY)],
        out_specs=pl.BlockSpec((rows, LANES), lambda i: (i, 0)),
        scratch_shapes=[pltpu.VMEM((GATHER_BUFFERS, rows, LANES), F32), pltpu.SemaphoreType.DMA((GATHER_BUFFERS,))],
        compiler_params=_cparams(("arbitrary",)),
        name="sorted_rows",
    )(inv3, inv3, inv3, h2)


def _expert_kernel(te_ref, xs_ref, wg_ref, wu_ref, wd_ref, ys_ref, wgb_ref, wub_ref, wdb_ref):
    i = pl.program_id(0)
    te = ys_ref.shape[0] // TOKEN_TILE_ROWS

    @pl.when(jnp.logical_or(i == 0, te_ref[i] != te_ref[jnp.maximum(i - 1, 0)]))
    def _():
        wgb_ref[...] = wg_ref[0].astype(BF16)
        wub_ref[...] = wu_ref[0].astype(BF16)
        wdb_ref[...] = wd_ref[0].astype(BF16)

    x = _load_token_tiles(xs_ref, 0, te).astype(BF16)
    hg = jnp.dot(x, wgb_ref[...], preferred_element_type=F32)
    hu = jnp.dot(x, wub_ref[...], preferred_element_type=F32)
    _store_token_tiles(ys_ref, _bdot(_silu(hg) * hu, wdb_ref[...]))


def _expert_mlp(tile_expert, xs, wg, wu, wd):
    rows = EXPERT_TILE * TOKEN_TILE_ROWS
    n_tiles = xs.shape[0] // rows
    return pl.pallas_call(
        _expert_kernel,
        out_shape=jax.ShapeDtypeStruct(xs.shape, F32),
        grid_spec=pltpu.PrefetchScalarGridSpec(
            num_scalar_prefetch=1,
            grid=(n_tiles,),
            in_specs=[pl.BlockSpec((rows, LANES), lambda i, e: (i, 0)),
                      pl.BlockSpec((1, D_MODEL, EXPERT_FF), lambda i, e: (e[i], 0, 0)),
                      pl.BlockSpec((1, D_MODEL, EXPERT_FF), lambda i, e: (e[i], 0, 0)),
                      pl.BlockSpec((1, EXPERT_FF, D_MODEL), lambda i, e: (e[i], 0, 0))],
            out_specs=pl.BlockSpec((rows, LANES), lambda i, e: (i, 0)),
            scratch_shapes=[pltpu.VMEM((D_MODEL, EXPERT_FF), BF16), pltpu.VMEM((D_MODEL, EXPERT_FF), BF16),
                            pltpu.VMEM((EXPERT_FF, D_MODEL), BF16)]),
        compiler_params=_cparams(("arbitrary",)),
        name="expert_mlp",
    )(tile_expert, xs, wg, wu, wd)


def _combine_kernel(pos0_ref, pos1_ref, pos2_ref, x1_ref, mod_ref, route_ref, postw_ref, ys_ref, out_ref, buf_ref,
                    sems, *, n_tiles):
    tm = x1_ref.shape[0]
    i = pl.program_id(0)
    slot = _gather_ring(i, (pos0_ref, pos1_ref, pos2_ref), ys_ref, 2 * tm, buf_ref, sems)
    route = route_ref[...]
    moe = (route[:, 2:3] * _load_token_tiles(buf_ref.at[slot], 0, tm)
           + route[:, 3:4] * _load_token_tiles(buf_ref.at[slot], tm, tm))
    out_ref[...] = x1_ref[...] + mod_ref[0][5:6, :] * _rms(moe, postw_ref[...])
    _gather_drain(i, n_tiles, ys_ref, buf_ref, sems)


def _combine(pos3, x1, mod3, route, postw, ys, *, bsz, seq):
    tm = TOK_TILE
    nt = seq // tm
    n_tiles = bsz * nt
    row_map = lambda i: (i, 0)
    return pl.pallas_call(
        functools.partial(_combine_kernel, n_tiles=n_tiles),
        out_shape=jax.ShapeDtypeStruct(x1.shape, F32),
        grid=(n_tiles,),
        in_specs=_ring_index_specs((1, 1, 2 * tm), n_tiles) + [
                  pl.BlockSpec((tm, D_MODEL), row_map),
                  pl.BlockSpec((1, N_MOD, D_MODEL), lambda i: (i // nt, 0, 0)),
                  pl.BlockSpec((tm, LANES), row_map),
                  pl.BlockSpec((1, D_MODEL), lambda i: (0, 0)),
                  pl.BlockSpec(memory_space=pl.ANY)],
        out_specs=pl.BlockSpec((tm, D_MODEL), row_map),
        scratch_shapes=[pltpu.VMEM((GATHER_BUFFERS, 2 * tm * TOKEN_TILE_ROWS, LANES), F32),
                        pltpu.SemaphoreType.DMA((GATHER_BUFFERS,))],
        compiler_params=_cparams(("arbitrary",)),
        name="combine",
    )(pos3, pos3, pos3, x1, mod3, route, postw, ys)


def _lane_pad(v, n=LANES):
    v = v.reshape(1, -1)
    return jnp.pad(v, ((0, 0), (0, n - v.shape[1])))


def kernel(x, c, ctx, c_ctx, w_mod, b_mod, pre_norm_w, post_norm_w, w_in, conv_w, a_log, dt_bias, o_norm_w,
           w_a_up, s5_lam_re, s5_lam_im, s5_log_step, s5_b_re, s5_b_im, s5_c_re, s5_c_im, s5_d, w_glu, w_out,
           w_router_group, b_router_group, w_router_expert, b_router_expert, w_gate_e, w_up_e, w_down_e):
    assert w_mod.shape[0] == 1, "single-layer block"
    bsz, seq, _ = x.shape
    ctx_len = ctx.shape[1]
    row = lambda v: v.reshape(1, -1)

    c16 = jnp.concatenate([c, c_ctx[None, :], jnp.zeros((2 * SUBLANES - bsz - 1, D_MODEL), F32)], axis=0)
    mod3 = _modulation(c16, w_mod[0], row(b_mod[0])).reshape(2 * SUBLANES, N_MOD, D_MODEL)

    wi = w_in[0]
    o_z, o_a, o_u, o_g = 3 * DN_WIDTH, 4 * DN_WIDTH, 4 * DN_WIDTH + 4 * DN_HEADS, 4 * DN_WIDTH + 4 * DN_HEADS + S5_WIDTH
    wts_in = (wi[:, :o_z].astype(BF16), wi[:, o_z:o_a].astype(BF16),
              jnp.pad(wi[:, o_a:o_u], ((0, 0), (0, LANES - 4 * DN_HEADS))).astype(BF16),
              wi[:, o_u:o_g].astype(BF16), wi[:, o_g:].astype(BF16))
    convw = jnp.pad(conv_w[0], ((0, SUBLANES - CONV_K), (0, 0)))
    alog = _lane_pad(a_log[0])
    dtb = _lane_pad(dt_bias[0])
    prew0, prew1 = row(pre_norm_w[0, 0]), row(pre_norm_w[0, 1])
    postw0, postw1 = row(post_norm_w[0, 0]), row(post_norm_w[0, 1])

    bar_re, bar_im, bb_re, bb_im = _s5_discretise(s5_lam_re[0], s5_lam_im[0], s5_log_step[0], s5_b_re[0], s5_b_im[0])
    s5w = _s5_weights(bar_re, bar_im, bb_re, bb_im, s5_c_re[0], s5_c_im[0], bsz)

    qkvn_c, gb_c, u_c = _input_projection(ctx.reshape(bsz * ctx_len, D_MODEL), mod3, bsz, 0, prew0, wts_in, convw,
                                          alog, dtb, bsz=bsz, seq=ctx_len, seg_len=ctx_len, full=False)
    s_zero = jnp.zeros((bsz, 2 * DN_HEADS, DN_HEAD_DIM, DN_HEAD_DIM), F32)
    _, _, s_ctx = _delta_rule(qkvn_c, gb_c, s_zero, bsz=bsz, seq=ctx_len)
    x_zero = jnp.zeros((2, bsz, S5_LANES), F32)
    (x_ctx,) = _s5_scan(u_c.reshape(bsz, ctx_len, S5_WIDTH), s5w, x_zero, emit_y=False)

    x2d = x.reshape(bsz * seq, D_MODEL)
    qkvn, gb, u, z, gates = _input_projection(x2d, mod3, 0, 1, prew0, wts_in, convw, alog, dtb,
                                                 bsz=bsz, seq=seq, seg_len=GRID_W, full=True)
    o_f, o_b, _ = _delta_rule(qkvn, gb, s_ctx, bsz=bsz, seq=seq)
    y_f, y_b, _ = _s5_scan(u.reshape(bsz, seq, S5_WIDTH), s5w, x_ctx, emit_y=True)
    rows2d = lambda a: a.reshape(bsz * seq, S5_WIDTH)

    w_router = jnp.pad(jnp.concatenate([w_router_group[0], w_router_expert[0]], axis=1),
                       ((0, 0), (0, LANES - N_EXPERT_GROUPS - N_EXPERTS))).astype(BF16)
    b_router = _lane_pad(jnp.concatenate([b_router_group[0], b_router_expert[0]]))
    wts_merge = (row(o_norm_w[0]), w_a_up[0].astype(BF16), row(s5_d[0]), w_glu[0].astype(BF16),
                 w_out[0].astype(BF16), postw0, prew1, w_router, b_router)
    x1, h2, route = _merge(x2d, mod3, o_f, o_b, z, rows2d(y_f), rows2d(y_b), u, gates, wts_merge, bsz=bsz, seq=seq)

    rank, count = _expert_ranks(route)
    n_tok = bsz * seq
    te = EXPERT_TILE
    counts = count[0, :N_EXPERTS].astype(jnp.int32)
    padded = ((counts + te - 1) // te) * te
    ends = jnp.cumsum(padded)
    starts = ends - padded
    ids = route[:, :2].astype(jnp.int32)
    one_hot = ids[:, :, None] == jnp.arange(N_EXPERTS, dtype=jnp.int32)
    pos = jnp.sum(jnp.where(one_hot, starts, 0), axis=-1) + rank[:, :2].astype(jnp.int32)
    pos3 = jnp.swapaxes(pos.reshape(n_tok // TOK_TILE, TOK_TILE, 2), 1, 2).reshape(n_tok // TOK_TILE, 1, 2 * TOK_TILE)
    n_sorted = 2 * n_tok + N_EXPERTS * te
    tile_start = jnp.arange(n_sorted // te, dtype=jnp.int32) * te
    tile_expert = jnp.minimum(jnp.sum((ends[None, :] <= tile_start[:, None]).astype(jnp.int32), axis=1), N_EXPERTS - 1)

    inv3 = _row_tokens(pos3, n_sorted).reshape(n_sorted // te, 1, te)
    xs = _sorted_rows(inv3, h2)
    ys = _expert_mlp(tile_expert, xs, w_gate_e[0], w_up_e[0], w_down_e[0])
    out = _combine(pos3, x1, mod3, route, postw1, ys, bsz=bsz, seq=seq)
    return out.reshape(bsz, seq, D_MODEL)
```

```python
import functools

import jax
import jax.numpy as jnp
from jax import lax
from jax.experimental import pallas as pl
from jax.experimental.pallas import tpu as pltpu

F32 = jnp.float32
BF16 = jnp.bfloat16
HIGHEST = lax.Precision.HIGHEST

D_MODEL = 1024
GRID_W = 64
DN_HEADS = 4
DN_HEAD_DIM = 128
DN_WIDTH = DN_HEADS * DN_HEAD_DIM
DN_CHUNK = 64
CONV_K = 5
S5_WIDTH = D_MODEL - DN_WIDTH
S5_GROUP = 16
S5_GROUPS = S5_WIDTH // S5_GROUP
S5_STATE = 64
S5_HALF_GROUPS = S5_GROUPS // 2
S5_HALF_STATE = S5_HALF_GROUPS * S5_STATE
S5_LANES = 2 * 2 * S5_HALF_STATE
N_EXPERT_GROUPS = 4
EXPERTS_PER_GROUP = 8
N_EXPERTS = N_EXPERT_GROUPS * EXPERTS_PER_GROUP
EXPERT_FF = 512
NORM_EPS = 1e-6
L2_EPS = 1e-6
N_MOD = 6
LANES = 128
SUBLANES = 8
TOKEN_TILE_ROWS = D_MODEL // LANES
VMEM_LIMIT = 56 * 1024 * 1024

TOK_TILE = 256
PROJ_TILE = 512
S5_TILE = 64
DN_CHUNKS_PER_STEP = 4
EXPERT_TILE = 512
RANK_TILE = 512
GATHER_BUFFERS = 3


def _cparams(sem):
    return pltpu.CompilerParams(dimension_semantics=sem, vmem_limit_bytes=VMEM_LIMIT)


def _bdot(a, b):
    return jnp.dot(a.astype(BF16), b.astype(BF16), preferred_element_type=F32)


def _bdot_nt(a, b):
    return lax.dot_general(a.astype(BF16), b.astype(BF16), (((1,), (1,)), ((), ())),
                           preferred_element_type=F32)


def _bdot_tn(a, b):
    return lax.dot_general(a.astype(BF16), b.astype(BF16), (((0,), (0,)), ((), ())),
                           preferred_element_type=F32)


def _silu(x):
    return x * jax.nn.sigmoid(x)


def _rms(x, w):
    return x * lax.rsqrt(jnp.mean(x * x, axis=-1, keepdims=True) + NORM_EPS) * w


def _store_token_tiles(ref, val):
    n = val.shape[0]
    for j in range(TOKEN_TILE_ROWS):
        ref[pl.ds(j, n, stride=TOKEN_TILE_ROWS), :] = val[:, j * LANES:(j + 1) * LANES]


def _load_token_tiles(ref, first_token, n):
    return jnp.concatenate([ref[pl.ds(first_token * TOKEN_TILE_ROWS + j, n, stride=TOKEN_TILE_ROWS), :]
                            for j in range(TOKEN_TILE_ROWS)], axis=1)


def _mod_kernel(c_ref, w_ref, b_ref, o_ref):
    o_ref[...] = jnp.dot(_silu(c_ref[...]), w_ref[...], precision=HIGHEST,
                         preferred_element_type=F32) + b_ref[...]


def _modulation(c16, w_mod, b_mod):
    n = w_mod.shape[1]
    return pl.pallas_call(
        _mod_kernel,
        out_shape=jax.ShapeDtypeStruct((c16.shape[0], n), F32),
        grid=(n // D_MODEL,),
        in_specs=[pl.BlockSpec(c16.shape, lambda j: (0, 0)),
                  pl.BlockSpec((D_MODEL, D_MODEL), lambda j: (0, j)),
                  pl.BlockSpec((1, D_MODEL), lambda j: (0, j))],
        out_specs=pl.BlockSpec((c16.shape[0], D_MODEL), lambda j: (0, j)),
        compiler_params=_cparams(("arbitrary",)),
        name="modulation",
    )(c16, w_mod, b_mod)


def _inproj_kernel(x_ref, mod_ref, prew_ref, wqkv_ref, wz_ref, wab_ref, wu_ref, wg_ref, convw_ref,
                   alog_ref, dtb_ref, *out_refs, seg_len, full):
    if full:
        qkvn_ref, gb_ref, u_ref, z_ref, gates_ref = out_refs
    else:
        qkvn_ref, gb_ref, u_ref = out_refs
    tm = x_ref.shape[0]
    mod = mod_ref[0]
    h = _rms(x_ref[...], prew_ref[...]) * (1.0 + mod[1:2, :]) + mod[0:1, :]
    hb = h.astype(BF16)

    qkv = jnp.dot(hb, wqkv_ref[...], preferred_element_type=F32)
    pos = lax.broadcasted_iota(jnp.int32, (tm, 1), 0) % seg_len
    acc = qkv * convw_ref[CONV_K // 2:CONV_K // 2 + 1, :]
    for k in range(CONV_K):
        s = k - CONV_K // 2
        if s == 0:
            continue
        shifted = pltpu.roll(qkv, (-s) % tm, 0)
        ok = (pos + s >= 0) & (pos + s < seg_len)
        acc = acc + jnp.where(ok, shifted, 0.0) * convw_ref[k:k + 1, :]
    act = _silu(acc)
    for j in range(3 * DN_HEADS):
        sl = slice(j * DN_HEAD_DIM, (j + 1) * DN_HEAD_DIM)
        a = act[:, sl]
        if j < 2 * DN_HEADS:
            a = a * lax.rsqrt(jnp.sum(a * a, axis=-1, keepdims=True) + L2_EPS)
        qkvn_ref[:, sl] = a

    ab = jnp.dot(hb, wab_ref[...], preferred_element_type=F32)
    xa = ab + dtb_ref[...]
    softplus = jnp.maximum(xa, 0.0) + jnp.log1p(jnp.exp(-jnp.abs(xa)))
    g = -jnp.exp(alog_ref[...]) * softplus
    lane = lax.broadcasted_iota(jnp.int32, ab.shape, 1)
    gb_ref[...] = jnp.where(lane < 2 * DN_HEADS, g, jax.nn.sigmoid(ab))

    u_ref[...] = jnp.dot(hb, wu_ref[...], preferred_element_type=F32)
    if full:
        z_ref[...] = jnp.dot(hb, wz_ref[...], preferred_element_type=F32)
        gates_ref[...] = jnp.dot(hb, wg_ref[...], preferred_element_type=F32)


def _input_projection(x2d, mod3, mod_row0, mod_row_stride, prew, wts, convw, alog, dtb, *, bsz, seq, seg_len, full):
    wqkv, wz, wab, wu, wg = wts
    tm = min(PROJ_TILE, seq)
    nt = seq // tm
    rows = bsz * seq
    row_map = lambda b, i: (b * nt + i, 0)
    const = lambda b, i: (0, 0)
    resident = lambda a: pl.BlockSpec(a.shape, const, pipeline_mode=pl.Buffered(1))
    out_shape = [jax.ShapeDtypeStruct((rows, 3 * DN_WIDTH), F32),
                 jax.ShapeDtypeStruct((rows, LANES), F32),
                 jax.ShapeDtypeStruct((rows, S5_WIDTH), F32)]
    out_specs = [pl.BlockSpec((tm, 3 * DN_WIDTH), row_map),
                 pl.BlockSpec((tm, LANES), row_map),
                 pl.BlockSpec((tm, S5_WIDTH), row_map)]
    if full:
        out_shape += [jax.ShapeDtypeStruct((rows, DN_WIDTH), F32),
                      jax.ShapeDtypeStruct((rows, 2 * D_MODEL), F32)]
        out_specs += [pl.BlockSpec((tm, DN_WIDTH), row_map),
                      pl.BlockSpec((tm, 2 * D_MODEL), row_map)]
    return pl.pallas_call(
        functools.partial(_inproj_kernel, seg_len=seg_len, full=full),
        out_shape=out_shape,
        grid=(bsz, nt),
        in_specs=[pl.BlockSpec((tm, D_MODEL), row_map),
                  pl.BlockSpec((1, N_MOD, D_MODEL), lambda b, i: (mod_row0 + mod_row_stride * b, 0, 0)),
                  pl.BlockSpec((1, D_MODEL), const),
                  resident(wqkv), resident(wz), resident(wab), resident(wu), resident(wg),
                  pl.BlockSpec(convw.shape, const),
                  pl.BlockSpec((1, LANES), const),
                  pl.BlockSpec((1, LANES), const)],
        out_specs=out_specs,
        compiler_params=_cparams(("arbitrary", "arbitrary")),
        name="input_projection_full" if full else "input_projection_ctx",
    )(x2d, mod3, prew, wqkv, wz, wab, wu, wg, convw, alog, dtb)


def _delta_kernel(qf_ref, kf_ref, vf_ref, gbf_ref, qb_ref, kb_ref, vb_ref, gbb_ref, s0_ref,
                  of_ref, ob_ref, sout_ref, state_ref, *, n_steps, chunks):
    c = DN_CHUNK
    i = pl.program_id(1)

    @pl.when(i == 0)
    def _():
        state_ref[...] = s0_ref[0]

    row = lax.broadcasted_iota(jnp.int32, (c, c), 0)
    col = lax.broadcasted_iota(jnp.int32, (c, c), 1)
    eye = (row == col).astype(F32)
    scale = DN_HEAD_DIM ** -0.5
    dirs = ((qf_ref, kf_ref, vf_ref, gbf_ref, of_ref), (qb_ref, kb_ref, vb_ref, gbb_ref, ob_ref))
    chains = {}
    for d, (q_ref, k_ref, v_ref, gb_ref, o_ref) in enumerate(dirs):
        incl = (row >= col) if d == 0 else (row <= col)
        strict = (row > col) if d == 0 else (row < col)
        last = c - 1 if d == 0 else 0
        for sub in range(chunks):
            rs = slice(sub * c, (sub + 1) * c)
            gb = gb_ref[rs, :]
            gcum = jnp.dot(incl.astype(F32), gb, precision=HIGHEST, preferred_element_type=F32)
            gcum_t = gcum.T
            for h in range(DN_HEADS):
                lane = d * DN_HEADS + h
                hs = slice(h * DN_HEAD_DIM, (h + 1) * DN_HEAD_DIM)
                g_col = gcum[:, lane:lane + 1]
                g_row = gcum_t[lane:lane + 1, :]
                g_last = gcum[last:last + 1, lane:lane + 1]
                beta = gb[:, 2 * DN_HEADS + lane:2 * DN_HEADS + lane + 1]
                dec = jnp.where(incl, jnp.exp(jnp.where(incl, g_col - g_row, 0.0)), 0.0)
                q = q_ref[rs, hs] * scale
                k = k_ref[rs, hs]
                v = v_ref[rs, hs]
                k_beta = k * beta
                e_g = jnp.exp(g_col)
                chains[d, sub, h] = dict(
                    lane=lane, rs=rs, hs=hs, o_ref=o_ref, strict=strict, dec=dec, q=q, k=k, k_beta=k_beta, e_g=e_g,
                    rhs=jnp.concatenate([v * beta, k_beta * e_g], axis=1),
                    k_tail=k * jnp.exp(g_last - g_col), e_last=jnp.exp(g_last))
    every = list(chains.values())
    for ch in every:
        kq = _bdot_nt(jnp.concatenate([ch['k_beta'], ch['q']], axis=0), ch['k'])
        ch['npow'] = -jnp.where(ch['strict'], kq[:c] * ch['dec'], 0.0)
        ch['attn'] = kq[c:] * ch['dec']
        ch['t_inv'] = eye + ch['npow']
    for _ in range(5):
        for ch in every:
            ch['npow'] = _bdot(ch['npow'], ch['npow'])
        for ch in every:
            ch['t_inv'] = ch['t_inv'] + _bdot(ch['t_inv'], ch['npow'])
    for ch in every:
        ch['sol'] = _bdot(ch['t_inv'], ch['rhs'])
        ch['q_dec'] = ch['q'] * ch['e_g']

    states = [state_ref[lane] for lane in range(2 * DN_HEADS)]
    for step in range(chunks):
        group = [chains[d, step if d == 0 else chunks - 1 - step, h] for d in range(2) for h in range(DN_HEADS)]
        for ch in group:
            ch['wq'] = _bdot(jnp.concatenate([ch['sol'][:, DN_HEAD_DIM:], ch['q_dec']], axis=0), states[ch['lane']])
        for ch in group:
            ch['v_new'] = ch['sol'][:, :DN_HEAD_DIM] - ch['wq'][:c]
            ch['o_ref'][ch['rs'], ch['hs']] = ch['wq'][c:] + _bdot(ch['attn'], ch['v_new'])
        for ch in group:
            states[ch['lane']] = states[ch['lane']] * ch['e_last'] + _bdot_tn(ch['k_tail'], ch['v_new'])
    for lane in range(2 * DN_HEADS):
        state_ref[lane] = states[lane]

    @pl.when(i == n_steps - 1)
    def _():
        sout_ref[0] = state_ref[...]


def _delta_rule(qkvn, gb, s0, *, bsz, seq):
    rows = DN_CHUNK * DN_CHUNKS_PER_STEP
    nb = seq // rows
    fwd = lambda j: (lambda b, i: (b * nb + i, j))
    bwd = lambda j: (lambda b, i: (b * nb + nb - 1 - i, j))
    blk = lambda m: pl.BlockSpec((rows, DN_WIDTH), m)
    gblk = lambda m: pl.BlockSpec((rows, LANES), m)
    st = pl.BlockSpec((1, 2 * DN_HEADS, DN_HEAD_DIM, DN_HEAD_DIM), lambda b, i: (b, 0, 0, 0))
    return pl.pallas_call(
        functools.partial(_delta_kernel, n_steps=nb, chunks=DN_CHUNKS_PER_STEP),
        out_shape=[jax.ShapeDtypeStruct((bsz * seq, DN_WIDTH), F32),
                   jax.ShapeDtypeStruct((bsz * seq, DN_WIDTH), F32),
                   jax.ShapeDtypeStruct(s0.shape, F32)],
        grid=(bsz, nb),
        in_specs=[blk(fwd(0)), blk(fwd(1)), blk(fwd(2)), gblk(fwd(0)),
                  blk(bwd(0)), blk(bwd(1)), blk(bwd(2)), gblk(bwd(0)), st],
        out_specs=[blk(fwd(0)), blk(bwd(0)), st],
        scratch_shapes=[pltpu.VMEM((2 * DN_HEADS, DN_HEAD_DIM, DN_HEAD_DIM), F32)],
        compiler_params=_cparams(("arbitrary", "arbitrary")),
        name="delta_rule",
    )(qkvn, qkvn, qkvn, gb, qkvn, qkvn, qkvn, gb, s0)


def _s5_disc_kernel(lre_ref, lim_ref, ls_ref, bre_ref, bim_ref, are_ref, aim_ref, bbre_ref, bbim_ref):
    lam_re = lre_ref[...]
    lam_im = lim_ref[...]
    dt = jnp.exp(ls_ref[...])
    mag = jnp.exp(lam_re * dt)
    ang = lam_im * dt
    bar_re = mag * jnp.cos(ang)
    bar_im = mag * jnp.sin(ang)
    den = lam_re * lam_re + lam_im * lam_im
    coef_re = ((bar_re - 1.0) * lam_re + bar_im * lam_im) / den
    coef_im = (bar_im * lam_re - (bar_re - 1.0) * lam_im) / den
    are_ref[...] = bar_re
    aim_ref[...] = bar_im
    bbre_ref[...] = coef_re * bre_ref[...] - coef_im * bim_ref[...]
    bbim_ref[...] = coef_re * bim_ref[...] + coef_im * bre_ref[...]


def _s5_discretise(lam_re, lam_im, log_step, b_re, b_im):
    rep = lambda t: jnp.repeat(t.reshape(2 * S5_GROUPS, -1), S5_GROUP, axis=0)
    rows = 2 * S5_GROUPS * S5_GROUP
    tr = lambda t: jnp.swapaxes(t, -1, -2).reshape(rows, S5_STATE)
    outs = pl.pallas_call(
        _s5_disc_kernel,
        out_shape=[jax.ShapeDtypeStruct((rows, S5_STATE), F32)] * 4,
        name="s5_discretise",
    )(rep(lam_re), rep(lam_im), rep(log_step[..., None]), tr(b_re), tr(b_im))
    shp = (2, S5_GROUPS, S5_GROUP, S5_STATE)
    bar_re, bar_im, bb_re, bb_im = (o.reshape(shp) for o in outs)
    return bar_re[:, :, 0], bar_im[:, :, 0], bb_re, bb_im


def _s5_kernel(uf_ref, ub_ref, wb_ref, wcre_ref, wcim_ref, are_ref, aim_ref, x0_ref, *rest, tt, n_tiles, emit_y):
    if emit_y:
        yf_ref, yb_ref, xout_ref, buf_ref, xst_ref, us_ref, ys_ref = rest
    else:
        xout_ref, buf_ref, xst_ref, us_ref = rest
        yf_ref = yb_ref = ys_ref = None
    i = pl.program_id(0)
    bsz = uf_ref.shape[0]
    hs = S5_HALF_STATE
    n_lane_tiles = S5_WIDTH // LANES
    blk = 512
    u_refs = (uf_ref, ub_ref)
    y_refs = (yf_ref, yb_ref)

    @pl.when(i == 0)
    def _():
        xst_ref[...] = x0_ref[...]

    def input_matmul(hh):
        for d in range(2):
            u_h = jnp.concatenate([us_ref[d, 2 * hh], us_ref[d, 2 * hh + 1]], axis=1)
            buf_ref[d, :, hh * 2 * hs:(hh + 1) * 2 * hs] = _bdot(u_h, wb_ref[d, hh])

    def scan(hh):
        for p in range(hs // blk):
            lr = hh * 2 * hs + p * blk
            li = lr + hs
            a = [(are_ref[d, :, lr:lr + blk], aim_ref[d, :, lr:lr + blk]) for d in range(2)]
            x = [(xst_ref[d, :, lr:lr + blk], xst_ref[d, :, li:li + blk]) for d in range(2)]
            for t in range(tt):
                for d in range(2):
                    r0 = (t if d == 0 else tt - 1 - t) * SUBLANES
                    a_re, a_im = a[d]
                    x_re, x_im = x[d]
                    n_re = a_re * x_re - a_im * x_im + buf_ref[d, r0:r0 + SUBLANES, lr:lr + blk]
                    n_im = a_re * x_im + a_im * x_re + buf_ref[d, r0:r0 + SUBLANES, li:li + blk]
                    buf_ref[d, r0:r0 + SUBLANES, lr:lr + blk] = n_re
                    buf_ref[d, r0:r0 + SUBLANES, li:li + blk] = n_im
                    x[d] = (n_re, n_im)
            for d in range(2):
                xst_ref[d, :, lr:lr + blk] = x[d][0]
                xst_ref[d, :, li:li + blk] = x[d][1]

    def output_matmul(hh):
        for d in range(2):
            x_re = buf_ref[d, :, hh * 2 * hs:hh * 2 * hs + hs]
            x_im = buf_ref[d, :, hh * 2 * hs + hs:(hh + 1) * 2 * hs]
            y_h = _bdot(x_re, wcre_ref[d, hh]) - _bdot(x_im, wcim_ref[d, hh])
            ys_ref[d, 2 * hh] = y_h[:, :LANES]
            ys_ref[d, 2 * hh + 1] = y_h[:, LANES:]

    for d in range(2):
        for b in range(bsz):
            for j in range(n_lane_tiles):
                us_ref[d, j, pl.ds(b, tt, stride=bsz), :] = u_refs[d][b, :, j * LANES:(j + 1) * LANES]
    input_matmul(0)
    input_matmul(1)
    scan(0)
    if emit_y:
        output_matmul(0)
    scan(1)
    if emit_y:
        output_matmul(1)
        for d in range(2):
            for b in range(bsz):
                for j in range(n_lane_tiles):
                    y_refs[d][b, :, j * LANES:(j + 1) * LANES] = ys_ref[d, j, pl.ds(b, tt, stride=bsz), :]

    @pl.when(i == n_tiles - 1)
    def _():
        xout_ref[...] = xst_ref[...]


def _s5_scan(u3, s5w, x0, *, emit_y):
    wb, wc_re, wc_im, a_re, a_im = s5w
    bsz, seq, _ = u3.shape
    assert bsz == SUBLANES, "one time step of all batches must fill one sublane group"
    tt = S5_TILE
    nt = seq // tt
    rows = tt * bsz
    full = lambda a: pl.BlockSpec(a.shape, lambda i: (0,) * a.ndim)
    ublk = lambda m: pl.BlockSpec((bsz, tt, S5_WIDTH), m)
    fwd = lambda i: (0, i, 0)
    bwd = lambda i: (0, nt - 1 - i, 0)
    out_shape = [jax.ShapeDtypeStruct(x0.shape, F32)]
    out_specs = [full(x0)]
    scratch = [pltpu.VMEM((2, rows, S5_LANES), F32), pltpu.VMEM(x0.shape, F32),
               pltpu.VMEM((2, S5_WIDTH // LANES, rows, LANES), F32)]
    if emit_y:
        out_shape = [jax.ShapeDtypeStruct(u3.shape, F32)] * 2 + out_shape
        out_specs = [ublk(fwd), ublk(bwd)] + out_specs
        scratch.append(pltpu.VMEM((2, S5_WIDTH // LANES, rows, LANES), F32))
    return pl.pallas_call(
        functools.partial(_s5_kernel, tt=tt, n_tiles=nt, emit_y=emit_y),
        out_shape=out_shape,
        grid=(nt,),
        in_specs=[ublk(fwd), ublk(bwd), full(wb), full(wc_re), full(wc_im), full(a_re), full(a_im), full(x0)],
        out_specs=out_specs,
        scratch_shapes=scratch,
        compiler_params=_cparams(("arbitrary",)),
        name="s5_scan_y" if emit_y else "s5_scan_state",
    )(u3, u3, wb, wc_re, wc_im, a_re, a_im, x0)


def _s5_weights(bar_re, bar_im, bb_re, bb_im, c_re, c_im, bsz):
    hg = S5_HALF_GROUPS
    eye = jnp.eye(hg, dtype=F32)

    def in_mat(bb):
        t = bb.reshape(2, 2, hg, S5_GROUP, S5_STATE)
        t = jnp.einsum('dhicp,ij->dhicjp', t, eye)
        return t.reshape(2, 2, hg * S5_GROUP, hg * S5_STATE)

    def out_mat(cc):
        t = cc.reshape(2, 2, hg, S5_GROUP, S5_STATE)
        t = jnp.einsum('dhicp,ij->dhipjc', t, eye)
        return t.reshape(2, 2, hg * S5_STATE, hg * S5_GROUP)

    wb = jnp.concatenate([in_mat(bb_re), in_mat(bb_im)], axis=-1).astype(BF16)
    wc_re = out_mat(c_re).astype(BF16)
    wc_im = out_mat(c_im).astype(BF16)

    def lanes(a):
        t = a.reshape(2, 2, 1, S5_HALF_STATE)
        t = jnp.broadcast_to(t, (2, 2, 2, S5_HALF_STATE)).reshape(2, 1, S5_LANES)
        return jnp.broadcast_to(t, (2, bsz, S5_LANES))

    return wb, wc_re, wc_im, lanes(bar_re), lanes(bar_im)


def _merge_kernel(x_ref, mod_ref, of_ref, ob_ref, z_ref, yf_ref, yb_ref, u_ref, gates_ref, onw_ref, waup_ref,
                  s5d_ref, wglu_ref, wout_ref, postw_ref, prew_ref, wr_ref, br_ref,
                  x1_ref, h2_ref, route_ref):
    mod = mod_ref[0]
    o_a = of_ref[...] + ob_ref[...]
    z = z_ref[...]
    parts = []
    for h in range(DN_HEADS):
        hs = slice(h * DN_HEAD_DIM, (h + 1) * DN_HEAD_DIM)
        parts.append(_rms(o_a[:, hs], onw_ref[...]) * _silu(z[:, hs]))
    y_a = _bdot(jnp.concatenate(parts, axis=1), waup_ref[...])
    y5 = yf_ref[...] + yb_ref[...] + s5d_ref[...] * u_ref[...]
    glu = _bdot(jax.nn.gelu(y5, approximate=True), wglu_ref[...])
    y_b = glu[:, :D_MODEL] * jax.nn.sigmoid(glu[:, D_MODEL:])
    gates = gates_ref[...]
    y = jax.nn.sigmoid(gates[:, :D_MODEL]) * y_a + jax.nn.sigmoid(gates[:, D_MODEL:]) * y_b
    mixed = _bdot(y, wout_ref[...])
    x1 = x_ref[...] + mod[2:3, :] * _rms(mixed, postw_ref[...])
    x1_ref[...] = x1
    h2 = _rms(x1, prew_ref[...]) * (1.0 + mod[4:5, :]) + mod[3:4, :]
    _store_token_tiles(h2_ref, h2)

    logits = _bdot(h2, wr_ref[...]) + br_ref[...]
    lane = lax.broadcasted_iota(jnp.int32, logits.shape, 1)
    neg = -jnp.inf
    big = jnp.int32(LANES)
    first = lambda hit: jnp.min(jnp.where(hit, lane, big), axis=-1, keepdims=True)
    g_logit = jnp.where(lane < N_EXPERT_GROUPS, logits, neg)
    g_max = jnp.max(g_logit, axis=-1, keepdims=True)
    g_sel = first(g_logit == g_max)
    g_w = 1.0 / jnp.sum(jnp.exp(g_logit - g_max), axis=-1, keepdims=True)
    e_idx = lane - N_EXPERT_GROUPS
    in_group = (e_idx >= 0) & (e_idx < N_EXPERTS) & ((e_idx // EXPERTS_PER_GROUP) == g_sel)
    e_logit = jnp.where(in_group, logits, neg)
    m1 = jnp.max(e_logit, axis=-1, keepdims=True)
    i1 = first(e_logit == m1)
    e_logit2 = jnp.where(lane == i1, neg, e_logit)
    m2 = jnp.max(e_logit2, axis=-1, keepdims=True)
    i2 = first(e_logit2 == m2)
    e2 = jnp.exp(m2 - m1)
    w1 = g_w / (1.0 + e2)
    w2 = g_w * e2 / (1.0 + e2)
    id1 = (i1 - N_EXPERT_GROUPS).astype(F32)
    id2 = (i2 - N_EXPERT_GROUPS).astype(F32)
    route_ref[...] = jnp.where(lane == 0, id1, jnp.where(lane == 1, id2, jnp.where(lane == 2, w1,
                               jnp.where(lane == 3, w2, 0.0))))


def _merge(x2d, mod3, o_f, o_b, z, y_f, y_b, u, gates, wts, *, bsz, seq):
    onw, waup, s5d, wglu, wout, postw, prew, wr, br = wts
    tm = PROJ_TILE
    nt = seq // tm
    rows = bsz * seq
    row_map = lambda b, i: (b * nt + i, 0)
    const = lambda b, i: (0, 0)
    cs = lambda a: pl.BlockSpec(a.shape, const, pipeline_mode=pl.Buffered(1))
    return pl.pallas_call(
        _merge_kernel,
        out_shape=[jax.ShapeDtypeStruct((rows, D_MODEL), F32),
                   jax.ShapeDtypeStruct((rows * TOKEN_TILE_ROWS, LANES), F32),
                   jax.ShapeDtypeStruct((rows, LANES), F32)],
        grid=(bsz, nt),
        in_specs=[pl.BlockSpec((tm, D_MODEL), row_map),
                  pl.BlockSpec((1, N_MOD, D_MODEL), lambda b, i: (b, 0, 0)),
                  pl.BlockSpec((tm, DN_WIDTH), row_map),
                  pl.BlockSpec((tm, DN_WIDTH), row_map),
                  pl.BlockSpec((tm, DN_WIDTH), row_map),
                  pl.BlockSpec((tm, S5_WIDTH), row_map),
                  pl.BlockSpec((tm, S5_WIDTH), row_map),
                  pl.BlockSpec((tm, S5_WIDTH), row_map),
                  pl.BlockSpec((tm, 2 * D_MODEL), row_map),
                  cs(onw), cs(waup), cs(s5d), cs(wglu), cs(wout), cs(postw), cs(prew), cs(wr), cs(br)],
        out_specs=[pl.BlockSpec((tm, D_MODEL), row_map),
                   pl.BlockSpec((tm * TOKEN_TILE_ROWS, LANES), row_map),
                   pl.BlockSpec((tm, LANES), row_map)],
        compiler_params=_cparams(("arbitrary", "arbitrary")),
        name="merge_router",
    )(x2d, mod3, o_f, o_b, z, y_f, y_b, u, gates, onw, waup, s5d, wglu, wout, postw, prew, wr, br)


def _rank_kernel(route_ref, rank_ref, count_ref, carry_ref):
    i = pl.program_id(0)

    @pl.when(i == 0)
    def _():
        carry_ref[...] = jnp.zeros_like(carry_ref)

    route = route_ref[...]
    r = route.shape[0]
    lane = lax.broadcasted_iota(jnp.int32, route.shape, 1).astype(F32)
    oh0 = (lane == route[:, 0:1]).astype(F32)
    oh1 = (lane == route[:, 1:2]).astype(F32)
    both = oh0 + oh1
    row = lax.broadcasted_iota(jnp.int32, (r, r), 0)
    col = lax.broadcasted_iota(jnp.int32, (r, r), 1)
    before = _bdot((row > col).astype(F32), both) + carry_ref[...]
    rank0 = jnp.sum(oh0 * before, axis=-1, keepdims=True)
    rank1 = jnp.sum(oh1 * before, axis=-1, keepdims=True)
    lane_i = lax.broadcasted_iota(jnp.int32, route.shape, 1)
    rank_ref[...] = jnp.where(lane_i == 0, rank0, jnp.where(lane_i == 1, rank1, 0.0))
    carry_ref[...] = carry_ref[...] + jnp.sum(both, axis=0, keepdims=True)
    count_ref[...] = carry_ref[...]


def _expert_ranks(route):
    rows = route.shape[0]
    r = RANK_TILE
    return pl.pallas_call(
        _rank_kernel,
        out_shape=[jax.ShapeDtypeStruct((rows, LANES), F32), jax.ShapeDtypeStruct((1, LANES), F32)],
        grid=(rows // r,),
        in_specs=[pl.BlockSpec((r, LANES), lambda i: (i, 0))],
        out_specs=[pl.BlockSpec((r, LANES), lambda i: (i, 0)), pl.BlockSpec((1, LANES), lambda i: (0, 0))],
        scratch_shapes=[pltpu.VMEM((1, LANES), F32)],
        compiler_params=_cparams(("arbitrary",)),
        name="expert_ranks",
    )(route)


def _token_copy(src_ref, src_token, dst_ref, dst_token, sem):
    r = TOKEN_TILE_ROWS
    return pltpu.make_async_copy(src_ref.at[pl.ds(pl.multiple_of(src_token * r, r), r), :],
                                 dst_ref.at[pl.ds(dst_token * r, r), :], sem)


def _row_token_kernel(pos_ref, fill_ref, inv_ref, sem, *, lane_bits):
    i = pl.program_id(0)
    tm = pos_ref.shape[-1] // 2
    base = i * tm

    @pl.when(i == 0)
    def _():
        fill = pltpu.make_async_copy(fill_ref, inv_ref, sem)
        fill.start()
        fill.wait()

    def body(t, _):
        for s in range(2):
            p = pos_ref[0, 0, s * tm + t]
            inv_ref[lax.shift_right_logical(p, lane_bits), p & (LANES - 1)] = base + t
        return 0

    lax.fori_loop(0, tm, body, 0, unroll=16)


def _row_tokens(pos3, n_rows):
    n_tiles, _, two_tm = pos3.shape
    lane_bits = LANES.bit_length() - 1
    assert 1 << lane_bits == LANES
    shape = (n_rows // LANES, LANES)
    n_tok = n_tiles * two_tm // 2
    fill = (jnp.arange(n_rows, dtype=jnp.int32) % n_tok).reshape(shape)
    return pl.pallas_call(
        functools.partial(_row_token_kernel, lane_bits=lane_bits),
        out_shape=jax.ShapeDtypeStruct(shape, jnp.int32),
        grid=(n_tiles,),
        in_specs=[pl.BlockSpec((1, 1, two_tm), lambda i: (i, 0, 0), memory_space=pltpu.SMEM),
                  pl.BlockSpec(memory_space=pl.ANY)],
        out_specs=pl.BlockSpec(memory_space=pltpu.SMEM),
        scratch_shapes=[pltpu.SemaphoreType.DMA],
        compiler_params=_cparams(("arbitrary",)),
        name="row_tokens",
    )(pos3, fill)


def _gather_rows(src_ref, idx_ref, n, dst_ref, sem):
    for t in range(n):
        _token_copy(src_ref, idx_ref[0, 0, t], dst_ref, t, sem).start()


def _wait_rows(src_ref, dst_ref, sem):
    pltpu.make_async_copy(src_ref.at[pl.ds(0, dst_ref.shape[0]), :], dst_ref, sem).wait()


def _gather_ring(i, idx_refs, src_ref, n, buf_ref, sems):
    nb = GATHER_BUFFERS

    @pl.when(i == 0)
    def _():
        for k in range(nb - 1):
            _gather_rows(src_ref, idx_refs[k], n, buf_ref.at[k], sems.at[k])

    slot = i % nb
    _wait_rows(src_ref, buf_ref.at[slot], sems.at[slot])
    ahead = (i + nb - 1) % nb
    _gather_rows(src_ref, idx_refs[nb - 1], n, buf_ref.at[ahead], sems.at[ahead])
    return slot


def _gather_drain(i, n_tiles, src_ref, buf_ref, sems):
    @pl.when(i == n_tiles - 1)
    def _():
        for k in range(1, GATHER_BUFFERS):
            s = (i + k) % GATHER_BUFFERS
            _wait_rows(src_ref, buf_ref.at[s], sems.at[s])


def _ring_index_specs(block, n_tiles):
    def spec(k):
        return pl.BlockSpec(block, lambda i, *_: (jnp.minimum(i + k, n_tiles - 1), 0, 0), memory_space=pltpu.SMEM)

    return [spec(k) for k in range(GATHER_BUFFERS)]


def _sorted_rows_kernel(inv0_ref, inv1_ref, inv2_ref, h2_ref, xs_ref, buf_ref, sems, *, n_tiles):
    i = pl.program_id(0)
    te = xs_ref.shape[0] // TOKEN_TILE_ROWS
    slot = _gather_ring(i, (inv0_ref, inv1_ref, inv2_ref), h2_ref, te, buf_ref, sems)
    xs_ref[...] = buf_ref[slot]
    _gather_drain(i, n_tiles, h2_ref, buf_ref, sems)


def _sorted_rows(inv3, h2):
    n_tiles, _, te = inv3.shape
    rows = te * TOKEN_TILE_ROWS
    return pl.pallas_call(
        functools.partial(_sorted_rows_kernel, n_tiles=n_tiles),
        out_shape=jax.ShapeDtypeStruct((n_tiles * rows, LANES), F32),
        grid=(n_tiles,),
        in_specs=_ring_index_specs((1, 1, te), n_tiles) + [pl.BlockSpec(memory_space=pl.ANY)],
        out_specs=pl.BlockSpec((rows, LANES), lambda i: (i, 0)),
        scratch_shapes=[pltpu.VMEM((GATHER_BUFFERS, rows, LANES), F32), pltpu.SemaphoreType.DMA((GATHER_BUFFERS,))],
        compiler_params=_cparams(("arbitrary",)),
        name="sorted_rows",
    )(inv3, inv3, inv3, h2)


def _expert_kernel(te_ref, xs_ref, wg_ref, wu_ref, wd_ref, ys_ref, wgb_ref, wub_ref, wdb_ref):
    i = pl.program_id(0)
    te = ys_ref.shape[0] // TOKEN_TILE_ROWS

    @pl.when(jnp.logical_or(i == 0, te_ref[i] != te_ref[jnp.maximum(i - 1, 0)]))
    def _():
        wgb_ref[...] = wg_ref[0].astype(BF16)
        wub_ref[...] = wu_ref[0].astype(BF16)
        wdb_ref[...] = wd_ref[0].astype(BF16)

    x = _load_token_tiles(xs_ref, 0, te).astype(BF16)
    hg = jnp.dot(x, wgb_ref[...], preferred_element_type=F32)
    hu = jnp.dot(x, wub_ref[...], preferred_element_type=F32)
    _store_token_tiles(ys_ref, _bdot(_silu(hg) * hu, wdb_ref[...]))


def _expert_mlp(tile_expert, xs, wg, wu, wd):
    rows = EXPERT_TILE * TOKEN_TILE_ROWS
    n_tiles = xs.shape[0] // rows
    return pl.pallas_call(
        _expert_kernel,
        out_shape=jax.ShapeDtypeStruct(xs.shape, F32),
        grid_spec=pltpu.PrefetchScalarGridSpec(
            num_scalar_prefetch=1,
            grid=(n_tiles,),
            in_specs=[pl.BlockSpec((rows, LANES), lambda i, e: (i, 0)),
                      pl.BlockSpec((1, D_MODEL, EXPERT_FF), lambda i, e: (e[i], 0, 0)),
                      pl.BlockSpec((1, D_MODEL, EXPERT_FF), lambda i, e: (e[i], 0, 0)),
                      pl.BlockSpec((1, EXPERT_FF, D_MODEL), lambda i, e: (e[i], 0, 0))],
            out_specs=pl.BlockSpec((rows, LANES), lambda i, e: (i, 0)),
            scratch_shapes=[pltpu.VMEM((D_MODEL, EXPERT_FF), BF16), pltpu.VMEM((D_MODEL, EXPERT_FF), BF16),
                            pltpu.VMEM((EXPERT_FF, D_MODEL), BF16)]),
        compiler_params=_cparams(("arbitrary",)),
        name="expert_mlp",
    )(tile_expert, xs, wg, wu, wd)


def _combine_kernel(pos0_ref, pos1_ref, pos2_ref, x1_ref, mod_ref, route_ref, postw_ref, ys_ref, out_ref, buf_ref,
                    sems, *, n_tiles):
    tm = x1_ref.shape[0]
    i = pl.program_id(0)
    slot = _gather_ring(i, (pos0_ref, pos1_ref, pos2_ref), ys_ref, 2 * tm, buf_ref, sems)
    route = route_ref[...]
    moe = (route[:, 2:3] * _load_token_tiles(buf_ref.at[slot], 0, tm)
           + route[:, 3:4] * _load_token_tiles(buf_ref.at[slot], tm, tm))
    out_ref[...] = x1_ref[...] + mod_ref[0][5:6, :] * _rms(moe, postw_ref[...])
    _gather_drain(i, n_tiles, ys_ref, buf_ref, sems)


def _combine(pos3, x1, mod3, route, postw, ys, *, bsz, seq):
    tm = TOK_TILE
    nt = seq // tm
    n_tiles = bsz * nt
    row_map = lambda i: (i, 0)
    return pl.pallas_call(
        functools.partial(_combine_kernel, n_tiles=n_tiles),
        out_shape=jax.ShapeDtypeStruct(x1.shape, F32),
        grid=(n_tiles,),
        in_specs=_ring_index_specs((1, 1, 2 * tm), n_tiles) + [
                  pl.BlockSpec((tm, D_MODEL), row_map),
                  pl.BlockSpec((1, N_MOD, D_MODEL), lambda i: (i // nt, 0, 0)),
                  pl.BlockSpec((tm, LANES), row_map),
                  pl.BlockSpec((1, D_MODEL), lambda i: (0, 0)),
                  pl.BlockSpec(memory_space=pl.ANY)],
        out_specs=pl.BlockSpec((tm, D_MODEL), row_map),
        scratch_shapes=[pltpu.VMEM((GATHER_BUFFERS, 2 * tm * TOKEN_TILE_ROWS, LANES), F32),
                        pltpu.SemaphoreType.DMA((GATHER_BUFFERS,))],
        compiler_params=_cparams(("arbitrary",)),
        name="combine",
    )(pos3, pos3, pos3, x1, mod3, route, postw, ys)


def _lane_pad(v, n=LANES):
    v = v.reshape(1, -1)
    return jnp.pad(v, ((0, 0), (0, n - v.shape[1])))


def kernel(x, c, ctx, c_ctx, w_mod, b_mod, pre_norm_w, post_norm_w, w_in, conv_w, a_log, dt_bias, o_norm_w,
           w_a_up, s5_lam_re, s5_lam_im, s5_log_step, s5_b_re, s5_b_im, s5_c_re, s5_c_im, s5_d, w_glu, w_out,
           w_router_group, b_router_group, w_router_expert, b_router_expert, w_gate_e, w_up_e, w_down_e):
    assert w_mod.shape[0] == 1, "single-layer block"
    bsz, seq, _ = x.shape
    ctx_len = ctx.shape[1]
    row = lambda v: v.reshape(1, -1)

    c16 = jnp.concatenate([c, c_ctx[None, :], jnp.zeros((2 * SUBLANES - bsz - 1, D_MODEL), F32)], axis=0)
    mod3 = _modulation(c16, w_mod[0], row(b_mod[0])).reshape(2 * SUBLANES, N_MOD, D_MODEL)

    wi = w_in[0]
    o_z, o_a, o_u, o_g = 3 * DN_WIDTH, 4 * DN_WIDTH, 4 * DN_WIDTH + 4 * DN_HEADS, 4 * DN_WIDTH + 4 * DN_HEADS + S5_WIDTH
    wts_in = (wi[:, :o_z].astype(BF16), wi[:, o_z:o_a].astype(BF16),
              jnp.pad(wi[:, o_a:o_u], ((0, 0), (0, LANES - 4 * DN_HEADS))).astype(BF16),
              wi[:, o_u:o_g].astype(BF16), wi[:, o_g:].astype(BF16))
    convw = jnp.pad(conv_w[0], ((0, SUBLANES - CONV_K), (0, 0)))
    alog = _lane_pad(a_log[0])
    dtb = _lane_pad(dt_bias[0])
    prew0, prew1 = row(pre_norm_w[0, 0]), row(pre_norm_w[0, 1])
    postw0, postw1 = row(post_norm_w[0, 0]), row(post_norm_w[0, 1])

    bar_re, bar_im, bb_re, bb_im = _s5_discretise(s5_lam_re[0], s5_lam_im[0], s5_log_step[0], s5_b_re[0], s5_b_im[0])
    s5w = _s5_weights(bar_re, bar_im, bb_re, bb_im, s5_c_re[0], s5_c_im[0], bsz)

    qkvn_c, gb_c, u_c = _input_projection(ctx.reshape(bsz * ctx_len, D_MODEL), mod3, bsz, 0, prew0, wts_in, convw,
                                          alog, dtb, bsz=bsz, seq=ctx_len, seg_len=ctx_len, full=False)
    s_zero = jnp.zeros((bsz, 2 * DN_HEADS, DN_HEAD_DIM, DN_HEAD_DIM), F32)
    _, _, s_ctx = _delta_rule(qkvn_c, gb_c, s_zero, bsz=bsz, seq=ctx_len)
    x_zero = jnp.zeros((2, bsz, S5_LANES), F32)
    (x_ctx,) = _s5_scan(u_c.reshape(bsz, ctx_len, S5_WIDTH), s5w, x_zero, emit_y=False)

    x2d = x.reshape(bsz * seq, D_MODEL)
    qkvn, gb, u, z, gates = _input_projection(x2d, mod3, 0, 1, prew0, wts_in, convw, alog, dtb,
                                                 bsz=bsz, seq=seq, seg_len=GRID_W, full=True)
    o_f, o_b, _ = _delta_rule(qkvn, gb, s_ctx, bsz=bsz, seq=seq)
    y_f, y_b, _ = _s5_scan(u.reshape(bsz, seq, S5_WIDTH), s5w, x_ctx, emit_y=True)
    rows2d = lambda a: a.reshape(bsz * seq, S5_WIDTH)

    w_router = jnp.pad(jnp.concatenate([w_router_group[0], w_router_expert[0]], axis=1),
                       ((0, 0), (0, LANES - N_EXPERT_GROUPS - N_EXPERTS))).astype(BF16)
    b_router = _lane_pad(jnp.concatenate([b_router_group[0], b_router_expert[0]]))
    wts_merge = (row(o_norm_w[0]), w_a_up[0].astype(BF16), row(s5_d[0]), w_glu[0].astype(BF16),
                 w_out[0].astype(BF16), postw0, prew1, w_router, b_router)
    x1, h2, route = _merge(x2d, mod3, o_f, o_b, z, rows2d(y_f), rows2d(y_b), u, gates, wts_merge, bsz=bsz, seq=seq)

    rank, count = _expert_ranks(route)
    n_tok = bsz * seq
    te = EXPERT_TILE
    counts = count[0, :N_EXPERTS].astype(jnp.int32)
    padded = ((counts + te - 1) // te) * te
    ends = jnp.cumsum(padded)
    starts = ends - padded
    ids = route[:, :2].astype(jnp.int32)
    one_hot = ids[:, :, None] == jnp.arange(N_EXPERTS, dtype=jnp.int32)
    pos = jnp.sum(jnp.where(one_hot, starts, 0), axis=-1) + rank[:, :2].astype(jnp.int32)
    pos3 = jnp.swapaxes(pos.reshape(n_tok // TOK_TILE, TOK_TILE, 2), 1, 2).reshape(n_tok // TOK_TILE, 1, 2 * TOK_TILE)
    n_sorted = 2 * n_tok + N_EXPERTS * te
    tile_start = jnp.arange(n_sorted // te, dtype=jnp.int32) * te
    tile_expert = jnp.minimum(jnp.sum((ends[None, :] <= tile_start[:, None]).astype(jnp.int32), axis=1), N_EXPERTS - 1)

    inv3 = _row_tokens(pos3, n_sorted).reshape(n_sorted // te, 1, te)
    xs = _sorted_rows(inv3, h2)
    ys = _expert_mlp(tile_expert, xs, w_gate_e[0], w_up_e[0], w_down_e[0])
    out = _combine(pos3, x1, mod3, route, postw1, ys, bsz=bsz, seq=seq)
    return out.reshape(bsz, seq, D_MODEL)
```

```python
import functools

import jax
import jax.numpy as jnp
from jax import lax
from jax.experimental import pallas as pl
from jax.experimental.pallas import tpu as pltpu

F32 = jnp.float32
BF16 = jnp.bfloat16
HIGHEST = lax.Precision.HIGHEST

D_MODEL = 1024
GRID_W = 64
DN_HEADS = 4
DN_HEAD_DIM = 128
DN_WIDTH = DN_HEADS * DN_HEAD_DIM
DN_CHUNK = 64
CONV_K = 5
S5_WIDTH = D_MODEL - DN_WIDTH
S5_GROUP = 16
S5_GROUPS = S5_WIDTH // S5_GROUP
S5_STATE = 64
S5_HALF_GROUPS = S5_GROUPS // 2
S5_HALF_STATE = S5_HALF_GROUPS * S5_STATE
S5_LANES = 2 * 2 * S5_HALF_STATE
N_EXPERT_GROUPS = 4
EXPERTS_PER_GROUP = 8
N_EXPERTS = N_EXPERT_GROUPS * EXPERTS_PER_GROUP
EXPERT_FF = 512
NORM_EPS = 1e-6
L2_EPS = 1e-6
N_MOD = 6
LANES = 128
SUBLANES = 8
TOKEN_TILE_ROWS = D_MODEL // LANES
VMEM_LIMIT = 56 * 1024 * 1024

TOK_TILE = 256
PROJ_TILE = 512
S5_TILE = 64
DN_CHUNKS_PER_STEP = 4
EXPERT_TILE = 512
RANK_TILE = 512
GATHER_BUFFERS = 3


def _cparams(sem):
    return pltpu.CompilerParams(dimension_semantics=sem, vmem_limit_bytes=VMEM_LIMIT)


def _bdot(a, b):
    return jnp.dot(a.astype(BF16), b.astype(BF16), preferred_element_type=F32)


def _bdot_nt(a, b):
    return lax.dot_general(a.astype(BF16), b.astype(BF16), (((1,), (1,)), ((), ())),
                           preferred_element_type=F32)


def _bdot_tn(a, b):
    return lax.dot_general(a.astype(BF16), b.astype(BF16), (((0,), (0,)), ((), ())),
                           preferred_element_type=F32)


def _silu(x):
    return x * jax.nn.sigmoid(x)


def _rms(x, w):
    return x * lax.rsqrt(jnp.mean(x * x, axis=-1, keepdims=True) + NORM_EPS) * w


def _store_token_tiles(ref, val):
    n = val.shape[0]
    for j in range(TOKEN_TILE_ROWS):
        ref[pl.ds(j, n, stride=TOKEN_TILE_ROWS), :] = val[:, j * LANES:(j + 1) * LANES]


def _load_token_tiles(ref, first_token, n):
    return jnp.concatenate([ref[pl.ds(first_token * TOKEN_TILE_ROWS + j, n, stride=TOKEN_TILE_ROWS), :]
                            for j in range(TOKEN_TILE_ROWS)], axis=1)


def _mod_kernel(c_ref, w_ref, b_ref, o_ref):
    o_ref[...] = jnp.dot(_silu(c_ref[...]), w_ref[...], precision=HIGHEST,
                         preferred_element_type=F32) + b_ref[...]


def _modulation(c16, w_mod, b_mod):
    n = w_mod.shape[1]
    return pl.pallas_call(
        _mod_kernel,
        out_shape=jax.ShapeDtypeStruct((c16.shape[0], n), F32),
        grid=(n // D_MODEL,),
        in_specs=[pl.BlockSpec(c16.shape, lambda j: (0, 0)),
                  pl.BlockSpec((D_MODEL, D_MODEL), lambda j: (0, j)),
                  pl.BlockSpec((1, D_MODEL), lambda j: (0, j))],
        out_specs=pl.BlockSpec((c16.shape[0], D_MODEL), lambda j: (0, j)),
        compiler_params=_cparams(("arbitrary",)),
        name="modulation",
    )(c16, w_mod, b_mod)


def _inproj_kernel(x_ref, mod_ref, prew_ref, wqkv_ref, wz_ref, wab_ref, wu_ref, wg_ref, convw_ref,
                   alog_ref, dtb_ref, *out_refs, seg_len, full):
    if full:
        qkvn_ref, gb_ref, u_ref, z_ref, gates_ref = out_refs
    else:
        qkvn_ref, gb_ref, u_ref = out_refs
    tm = x_ref.shape[0]
    mod = mod_ref[0]
    h = _rms(x_ref[...], prew_ref[...]) * (1.0 + mod[1:2, :]) + mod[0:1, :]
    hb = h.astype(BF16)

    qkv = jnp.dot(hb, wqkv_ref[...], preferred_element_type=F32)
    pos = lax.broadcasted_iota(jnp.int32, (tm, 1), 0) % seg_len
    acc = qkv * convw_ref[CONV_K // 2:CONV_K // 2 + 1, :]
    for k in range(CONV_K):
        s = k - CONV_K // 2
        if s == 0:
            continue
        shifted = pltpu.roll(qkv, (-s) % tm, 0)
        ok = (pos + s >= 0) & (pos + s < seg_len)
        acc = acc + jnp.where(ok, shifted, 0.0) * convw_ref[k:k + 1, :]
    act = _silu(acc)
    for j in range(3 * DN_HEADS):
        sl = slice(j * DN_HEAD_DIM, (j + 1) * DN_HEAD_DIM)
        a = act[:, sl]
        if j < 2 * DN_HEADS:
            a = a * lax.rsqrt(jnp.sum(a * a, axis=-1, keepdims=True) + L2_EPS)
        qkvn_ref[:, sl] = a

    ab = jnp.dot(hb, wab_ref[...], preferred_element_type=F32)
    xa = ab + dtb_ref[...]
    softplus = jnp.maximum(xa, 0.0) + jnp.log1p(jnp.exp(-jnp.abs(xa)))
    g = -jnp.exp(alog_ref[...]) * softplus
    lane = lax.broadcasted_iota(jnp.int32, ab.shape, 1)
    gb_ref[...] = jnp.where(lane < 2 * DN_HEADS, g, jax.nn.sigmoid(ab))

    u_ref[...] = jnp.dot(hb, wu_ref[...], preferred_element_type=F32)
    if full:
        z_ref[...] = jnp.dot(hb, wz_ref[...], preferred_element_type=F32)
        gates_ref[...] = jnp.dot(hb, wg_ref[...], preferred_element_type=F32)


def _input_projection(x2d, mod3, mod_row0, mod_row_stride, prew, wts, convw, alog, dtb, *, bsz, seq, seg_len, full):
    wqkv, wz, wab, wu, wg = wts
    tm = min(PROJ_TILE, seq)
    nt = seq // tm
    rows = bsz * seq
    row_map = lambda b, i: (b * nt + i, 0)
    const = lambda b, i: (0, 0)
    resident = lambda a: pl.BlockSpec(a.shape, const, pipeline_mode=pl.Buffered(1))
    out_shape = [jax.ShapeDtypeStruct((rows, 3 * DN_WIDTH), F32),
                 jax.ShapeDtypeStruct((rows, LANES), F32),
                 jax.ShapeDtypeStruct((rows, S5_WIDTH), F32)]
    out_specs = [pl.BlockSpec((tm, 3 * DN_WIDTH), row_map),
                 pl.BlockSpec((tm, LANES), row_map),
                 pl.BlockSpec((tm, S5_WIDTH), row_map)]
    if full:
        out_shape += [jax.ShapeDtypeStruct((rows, DN_WIDTH), F32),
                      jax.ShapeDtypeStruct((rows, 2 * D_MODEL), F32)]
        out_specs += [pl.BlockSpec((tm, DN_WIDTH), row_map),
                      pl.BlockSpec((tm, 2 * D_MODEL), row_map)]
    return pl.pallas_call(
        functools.partial(_inproj_kernel, seg_len=seg_len, full=full),
        out_shape=out_shape,
        grid=(bsz, nt),
        in_specs=[pl.BlockSpec((tm, D_MODEL), row_map),
                  pl.BlockSpec((1, N_MOD, D_MODEL), lambda b, i: (mod_row0 + mod_row_stride * b, 0, 0)),
                  pl.BlockSpec((1, D_MODEL), const),
                  resident(wqkv), resident(wz), resident(wab), resident(wu), resident(wg),
                  pl.BlockSpec(convw.shape, const),
                  pl.BlockSpec((1, LANES), const),
                  pl.BlockSpec((1, LANES), const)],
        out_specs=out_specs,
        compiler_params=_cparams(("arbitrary", "arbitrary")),
        name="input_projection_full" if full else "input_projection_ctx",
    )(x2d, mod3, prew, wqkv, wz, wab, wu, wg, convw, alog, dtb)


def _delta_kernel(qf_ref, kf_ref, vf_ref, gbf_ref, qb_ref, kb_ref, vb_ref, gbb_ref, s0_ref,
                  of_ref, ob_ref, sout_ref, state_ref, *, n_steps, chunks):
    c = DN_CHUNK
    i = pl.program_id(1)

    @pl.when(i == 0)
    def _():
        state_ref[...] = s0_ref[0]

    row = lax.broadcasted_iota(jnp.int32, (c, c), 0)
    col = lax.broadcasted_iota(jnp.int32, (c, c), 1)
    eye = (row == col).astype(F32)
    scale = DN_HEAD_DIM ** -0.5
    dirs = ((qf_ref, kf_ref, vf_ref, gbf_ref, of_ref), (qb_ref, kb_ref, vb_ref, gbb_ref, ob_ref))
    chains = {}
    for d, (q_ref, k_ref, v_ref, gb_ref, o_ref) in enumerate(dirs):
        incl = (row >= col) if d == 0 else (row <= col)
        strict = (row > col) if d == 0 else (row < col)
        last = c - 1 if d == 0 else 0
        for sub in range(chunks):
            rs = slice(sub * c, (sub + 1) * c)
            gb = gb_ref[rs, :]
            gcum = jnp.dot(incl.astype(F32), gb, precision=HIGHEST, preferred_element_type=F32)
            gcum_t = gcum.T
            for h in range(DN_HEADS):
                lane = d * DN_HEADS + h
                hs = slice(h * DN_HEAD_DIM, (h + 1) * DN_HEAD_DIM)
                g_col = gcum[:, lane:lane + 1]
                g_row = gcum_t[lane:lane + 1, :]
                g_last = gcum[last:last + 1, lane:lane + 1]
                beta = gb[:, 2 * DN_HEADS + lane:2 * DN_HEADS + lane + 1]
                dec = jnp.where(incl, jnp.exp(jnp.where(incl, g_col - g_row, 0.0)), 0.0)
                q = q_ref[rs, hs] * scale
                k = k_ref[rs, hs]
                v = v_ref[rs, hs]
                k_beta = k * beta
                e_g = jnp.exp(g_col)
                chains[d, sub, h] = dict(
                    lane=lane, rs=rs, hs=hs, o_ref=o_ref, strict=strict, dec=dec, q=q, k=k, k_beta=k_beta, e_g=e_g,
                    rhs=jnp.concatenate([v * beta, k_beta * e_g], axis=1),
                    k_tail=k * jnp.exp(g_last - g_col), e_last=jnp.exp(g_last))
    every = list(chains.values())
    for ch in every:
        kq = _bdot_nt(jnp.concatenate([ch['k_beta'], ch['q']], axis=0), ch['k'])
        ch['npow'] = -jnp.where(ch['strict'], kq[:c] * ch['dec'], 0.0)
        ch['attn'] = kq[c:] * ch['dec']
        ch['t_inv'] = eye + ch['npow']
    for _ in range(5):
        for ch in every:
            ch['npow'] = _bdot(ch['npow'], ch['npow'])
        for ch in every:
            ch['t_inv'] = ch['t_inv'] + _bdot(ch['t_inv'], ch['npow'])
    for ch in every:
        ch['sol'] = _bdot(ch['t_inv'], ch['rhs'])
        ch['q_dec'] = ch['q'] * ch['e_g']

    states = [state_ref[lane] for lane in range(2 * DN_HEADS)]
    for step in range(chunks):
        group = [chains[d, step if d == 0 else chunks - 1 - step, h] for d in range(2) for h in range(DN_HEADS)]
        for ch in group:
            ch['wq'] = _bdot(jnp.concatenate([ch['sol'][:, DN_HEAD_DIM:], ch['q_dec']], axis=0), states[ch['lane']])
        for ch in group:
            ch['v_new'] = ch['sol'][:, :DN_HEAD_DIM] - ch['wq'][:c]
            ch['o_ref'][ch['rs'], ch['hs']] = ch['wq'][c:] + _bdot(ch['attn'], ch['v_new'])
        for ch in group:
            states[ch['lane']] = states[ch['lane']] * ch['e_last'] + _bdot_tn(ch['k_tail'], ch['v_new'])
    for lane in range(2 * DN_HEADS):
        state_ref[lane] = states[lane]

    @pl.when(i == n_steps - 1)
    def _():
        sout_ref[0] = state_ref[...]


def _delta_rule(qkvn, gb, s0, *, bsz, seq):
    rows = DN_CHUNK * DN_CHUNKS_PER_STEP
    nb = seq // rows
    fwd = lambda j: (lambda b, i: (b * nb + i, j))
    bwd = lambda j: (lambda b, i: (b * nb + nb - 1 - i, j))
    blk = lambda m: pl.BlockSpec((rows, DN_WIDTH), m)
    gblk = lambda m: pl.BlockSpec((rows, LANES), m)
    st = pl.BlockSpec((1, 2 * DN_HEADS, DN_HEAD_DIM, DN_HEAD_DIM), lambda b, i: (b, 0, 0, 0))
    return pl.pallas_call(
        functools.partial(_delta_kernel, n_steps=nb, chunks=DN_CHUNKS_PER_STEP),
        out_shape=[jax.ShapeDtypeStruct((bsz * seq, DN_WIDTH), F32),
                   jax.ShapeDtypeStruct((bsz * seq, DN_WIDTH), F32),
                   jax.ShapeDtypeStruct(s0.shape, F32)],
        grid=(bsz, nb),
        in_specs=[blk(fwd(0)), blk(fwd(1)), blk(fwd(2)), gblk(fwd(0)),
                  blk(bwd(0)), blk(bwd(1)), blk(bwd(2)), gblk(bwd(0)), st],
        out_specs=[blk(fwd(0)), blk(bwd(0)), st],
        scratch_shapes=[pltpu.VMEM((2 * DN_HEADS, DN_HEAD_DIM, DN_HEAD_DIM), F32)],
        compiler_params=_cparams(("arbitrary", "arbitrary")),
        name="delta_rule",
    )(qkvn, qkvn, qkvn, gb, qkvn, qkvn, qkvn, gb, s0)


def _s5_disc_kernel(lre_ref, lim_ref, ls_ref, bre_ref, bim_ref, are_ref, aim_ref, bbre_ref, bbim_ref):
    lam_re = lre_ref[...]
    lam_im = lim_ref[...]
    dt = jnp.exp(ls_ref[...])
    mag = jnp.exp(lam_re * dt)
    ang = lam_im * dt
    bar_re = mag * jnp.cos(ang)
    bar_im = mag * jnp.sin(ang)
    den = lam_re * lam_re + lam_im * lam_im
    coef_re = ((bar_re - 1.0) * lam_re + bar_im * lam_im) / den
    coef_im = (bar_im * lam_re - (bar_re - 1.0) * lam_im) / den
    are_ref[...] = bar_re
    aim_ref[...] = bar_im
    bbre_ref[...] = coef_re * bre_ref[...] - coef_im * bim_ref[...]
    bbim_ref[...] = coef_re * bim_ref[...] + coef_im * bre_ref[...]


def _s5_discretise(lam_re, lam_im, log_step, b_re, b_im):
    rep = lambda t: jnp.repeat(t.reshape(2 * S5_GROUPS, -1), S5_GROUP, axis=0)
    rows = 2 * S5_GROUPS * S5_GROUP
    tr = lambda t: jnp.swapaxes(t, -1, -2).reshape(rows, S5_STATE)
    outs = pl.pallas_call(
        _s5_disc_kernel,
        out_shape=[jax.ShapeDtypeStruct((rows, S5_STATE), F32)] * 4,
        name="s5_discretise",
    )(rep(lam_re), rep(lam_im), rep(log_step[..., None]), tr(b_re), tr(b_im))
    shp = (2, S5_GROUPS, S5_GROUP, S5_STATE)
    bar_re, bar_im, bb_re, bb_im = (o.reshape(shp) for o in outs)
    return bar_re[:, :, 0], bar_im[:, :, 0], bb_re, bb_im


def _s5_kernel(uf_ref, ub_ref, wb_ref, wcre_ref, wcim_ref, are_ref, aim_ref, x0_ref, *rest, tt, n_tiles, emit_y):
    if emit_y:
        yf_ref, yb_ref, xout_ref, buf_ref, xst_ref, us_ref, ys_ref = rest
    else:
        xout_ref, buf_ref, xst_ref, us_ref = rest
        yf_ref = yb_ref = ys_ref = None
    i = pl.program_id(0)
    bsz = uf_ref.shape[0]
    hs = S5_HALF_STATE
    n_lane_tiles = S5_WIDTH // LANES
    blk = 512
    u_refs = (uf_ref, ub_ref)
    y_refs = (yf_ref, yb_ref)

    @pl.when(i == 0)
    def _():
        xst_ref[...] = x0_ref[...]

    def input_matmul(hh):
        for d in range(2):
            u_h = jnp.concatenate([us_ref[d, 2 * hh], us_ref[d, 2 * hh + 1]], axis=1)
            buf_ref[d, :, hh * 2 * hs:(hh + 1) * 2 * hs] = _bdot(u_h, wb_ref[d, hh])

    def scan(hh):
        for p in range(hs // blk):
            lr = hh * 2 * hs + p * blk
            li = lr + hs
            a = [(are_ref[d, :, lr:lr + blk], aim_ref[d, :, lr:lr + blk]) for d in range(2)]
            x = [(xst_ref[d, :, lr:lr + blk], xst_ref[d, :, li:li + blk]) for d in range(2)]
            for t in range(tt):
                for d in range(2):
                    r0 = (t if d == 0 else tt - 1 - t) * SUBLANES
                    a_re, a_im = a[d]
                    x_re, x_im = x[d]
                    n_re = a_re * x_re - a_im * x_im + buf_ref[d, r0:r0 + SUBLANES, lr:lr + blk]
                    n_im = a_re * x_im + a_im * x_re + buf_ref[d, r0:r0 + SUBLANES, li:li + blk]
                    buf_ref[d, r0:r0 + SUBLANES, lr:lr + blk] = n_re
                    buf_ref[d, r0:r0 + SUBLANES, li:li + blk] = n_im
                    x[d] = (n_re, n_im)
            for d in range(2):
                xst_ref[d, :, lr:lr + blk] = x[d][0]
                xst_ref[d, :, li:li + blk] = x[d][1]

    def output_matmul(hh):
        for d in range(2):
            x_re = buf_ref[d, :, hh * 2 * hs:hh * 2 * hs + hs]
            x_im = buf_ref[d, :, hh * 2 * hs + hs:(hh + 1) * 2 * hs]
            y_h = _bdot(x_re, wcre_ref[d, hh]) - _bdot(x_im, wcim_ref[d, hh])
            ys_ref[d, 2 * hh] = y_h[:, :LANES]
            ys_ref[d, 2 * hh + 1] = y_h[:, LANES:]

    for d in range(2):
        for b in range(bsz):
            for j in range(n_lane_tiles):
                us_ref[d, j, pl.ds(b, tt, stride=bsz), :] = u_refs[d][b, :, j * LANES:(j + 1) * LANES]
    input_matmul(0)
    input_matmul(1)
    scan(0)
    if emit_y:
        output_matmul(0)
    scan(1)
    if emit_y:
        output_matmul(1)
        for d in range(2):
            for b in range(bsz):
                for j in range(n_lane_tiles):
                    y_refs[d][b, :, j * LANES:(j + 1) * LANES] = ys_ref[d, j, pl.ds(b, tt, stride=bsz), :]

    @pl.when(i == n_tiles - 1)
    def _():
        xout_ref[...] = xst_ref[...]


def _s5_scan(u3, s5w, x0, *, emit_y):
    wb, wc_re, wc_im, a_re, a_im = s5w
    bsz, seq, _ = u3.shape
    assert bsz == SUBLANES, "one time step of all batches must fill one sublane group"
    tt = S5_TILE
    nt = seq // tt
    rows = tt * bsz
    full = lambda a: pl.BlockSpec(a.shape, lambda i: (0,) * a.ndim)
    ublk = lambda m: pl.BlockSpec((bsz, tt, S5_WIDTH), m)
    fwd = lambda i: (0, i, 0)
    bwd = lambda i: (0, nt - 1 - i, 0)
    out_shape = [jax.ShapeDtypeStruct(x0.shape, F32)]
    out_specs = [full(x0)]
    scratch = [pltpu.VMEM((2, rows, S5_LANES), F32), pltpu.VMEM(x0.shape, F32),
               pltpu.VMEM((2, S5_WIDTH // LANES, rows, LANES), F32)]
    if emit_y:
        out_shape = [jax.ShapeDtypeStruct(u3.shape, F32)] * 2 + out_shape
        out_specs = [ublk(fwd), ublk(bwd)] + out_specs
        scratch.append(pltpu.VMEM((2, S5_WIDTH // LANES, rows, LANES), F32))
    return pl.pallas_call(
        functools.partial(_s5_kernel, tt=tt, n_tiles=nt, emit_y=emit_y),
        out_shape=out_shape,
        grid=(nt,),
        in_specs=[ublk(fwd), ublk(bwd), full(wb), full(wc_re), full(wc_im), full(a_re), full(a_im), full(x0)],
        out_specs=out_specs,
        scratch_shapes=scratch,
        compiler_params=_cparams(("arbitrary",)),
        name="s5_scan_y" if emit_y else "s5_scan_state",
    )(u3, u3, wb, wc_re, wc_im, a_re, a_im, x0)


def _s5_weights(bar_re, bar_im, bb_re, bb_im, c_re, c_im, bsz):
    hg = S5_HALF_GROUPS
    eye = jnp.eye(hg, dtype=F32)

    def in_mat(bb):
        t = bb.reshape(2, 2, hg, S5_GROUP, S5_STATE)
        t = jnp.einsum('dhicp,ij->dhicjp', t, eye)
        return t.reshape(2, 2, hg * S5_GROUP, hg * S5_STATE)

    def out_mat(cc):
        t = cc.reshape(2, 2, hg, S5_GROUP, S5_STATE)
        t = jnp.einsum('dhicp,ij->dhipjc', t, eye)
        return t.reshape(2, 2, hg * S5_STATE, hg * S5_GROUP)

    wb = jnp.concatenate([in_mat(bb_re), in_mat(bb_im)], axis=-1).astype(BF16)
    wc_re = out_mat(c_re).astype(BF16)
    wc_im = out_mat(c_im).astype(BF16)

    def lanes(a):
        t = a.reshape(2, 2, 1, S5_HALF_STATE)
        t = jnp.broadcast_to(t, (2, 2, 2, S5_HALF_STATE)).reshape(2, 1, S5_LANES)
        return jnp.broadcast_to(t, (2, bsz, S5_LANES))

    return wb, wc_re, wc_im, lanes(bar_re), lanes(bar_im)


def _merge_kernel(x_ref, mod_ref, of_ref, ob_ref, z_ref, yf_ref, yb_ref, u_ref, gates_ref, onw_ref, waup_ref,
                  s5d_ref, wglu_ref, wout_ref, postw_ref, prew_ref, wr_ref, br_ref,
                  x1_ref, h2_ref, route_ref):
    mod = mod_ref[0]
    o_a = of_ref[...] + ob_ref[...]
    z = z_ref[...]
    parts = []
    for h in range(DN_HEADS):
        hs = slice(h * DN_HEAD_DIM, (h + 1) * DN_HEAD_DIM)
        parts.append(_rms(o_a[:, hs], onw_ref[...]) * _silu(z[:, hs]))
    y_a = _bdot(jnp.concatenate(parts, axis=1), waup_ref[...])
    y5 = yf_ref[...] + yb_ref[...] + s5d_ref[...] * u_ref[...]
    glu = _bdot(jax.nn.gelu(y5, approximate=True), wglu_ref[...])
    y_b = glu[:, :D_MODEL] * jax.nn.sigmoid(glu[:, D_MODEL:])
    gates = gates_ref[...]
    y = jax.nn.sigmoid(gates[:, :D_MODEL]) * y_a + jax.nn.sigmoid(gates[:, D_MODEL:]) * y_b
    mixed = _bdot(y, wout_ref[...])
    x1 = x_ref[...] + mod[2:3, :] * _rms(mixed, postw_ref[...])
    x1_ref[...] = x1
    h2 = _rms(x1, prew_ref[...]) * (1.0 + mod[4:5, :]) + mod[3:4, :]
    _store_token_tiles(h2_ref, h2)

    logits = _bdot(h2, wr_ref[...]) + br_ref[...]
    lane = lax.broadcasted_iota(jnp.int32, logits.shape, 1)
    neg = -jnp.inf
    big = jnp.int32(LANES)
    first = lambda hit: jnp.min(jnp.where(hit, lane, big), axis=-1, keepdims=True)
    g_logit = jnp.where(lane < N_EXPERT_GROUPS, logits, neg)
    g_max = jnp.max(g_logit, axis=-1, keepdims=True)
    g_sel = first(g_logit == g_max)
    g_w = 1.0 / jnp.sum(jnp.exp(g_logit - g_max), axis=-1, keepdims=True)
    e_idx = lane - N_EXPERT_GROUPS
    in_group = (e_idx >= 0) & (e_idx < N_EXPERTS) & ((e_idx // EXPERTS_PER_GROUP) == g_sel)
    e_logit = jnp.where(in_group, logits, neg)
    m1 = jnp.max(e_logit, axis=-1, keepdims=True)
    i1 = first(e_logit == m1)
    e_logit2 = jnp.where(lane == i1, neg, e_logit)
    m2 = jnp.max(e_logit2, axis=-1, keepdims=True)
    i2 = first(e_logit2 == m2)
    e2 = jnp.exp(m2 - m1)
    w1 = g_w / (1.0 + e2)
    w2 = g_w * e2 / (1.0 + e2)
    id1 = (i1 - N_EXPERT_GROUPS).astype(F32)
    id2 = (i2 - N_EXPERT_GROUPS).astype(F32)
    route_ref[...] = jnp.where(lane == 0, id1, jnp.where(lane == 1, id2, jnp.where(lane == 2, w1,
                               jnp.where(lane == 3, w2, 0.0))))


def _merge(x2d, mod3, o_f, o_b, z, y_f, y_b, u, gates, wts, *, bsz, seq):
    onw, waup, s5d, wglu, wout, postw, prew, wr, br = wts
    tm = PROJ_TILE
    nt = seq // tm
    rows = bsz * seq
    row_map = lambda b, i: (b * nt + i, 0)
    const = lambda b, i: (0, 0)
    cs = lambda a: pl.BlockSpec(a.shape, const, pipeline_mode=pl.Buffered(1))
    return pl.pallas_call(
        _merge_kernel,
        out_shape=[jax.ShapeDtypeStruct((rows, D_MODEL), F32),
                   jax.ShapeDtypeStruct((rows * TOKEN_TILE_ROWS, LANES), F32),
                   jax.ShapeDtypeStruct((rows, LANES), F32)],
        grid=(bsz, nt),
        in_specs=[pl.BlockSpec((tm, D_MODEL), row_map),
                  pl.BlockSpec((1, N_MOD, D_MODEL), lambda b, i: (b, 0, 0)),
                  pl.BlockSpec((tm, DN_WIDTH), row_map),
                  pl.BlockSpec((tm, DN_WIDTH), row_map),
                  pl.BlockSpec((tm, DN_WIDTH), row_map),
                  pl.BlockSpec((tm, S5_WIDTH), row_map),
                  pl.BlockSpec((tm, S5_WIDTH), row_map),
                  pl.BlockSpec((tm, S5_WIDTH), row_map),
                  pl.BlockSpec((tm, 2 * D_MODEL), row_map),
                  cs(onw), cs(waup), cs(s5d), cs(wglu), cs(wout), cs(postw), cs(prew), cs(wr), cs(br)],
        out_specs=[pl.BlockSpec((tm, D_MODEL), row_map),
                   pl.BlockSpec((tm * TOKEN_TILE_ROWS, LANES), row_map),
                   pl.BlockSpec((tm, LANES), row_map)],
        compiler_params=_cparams(("arbitrary", "arbitrary")),
        name="merge_router",
    )(x2d, mod3, o_f, o_b, z, y_f, y_b, u, gates, onw, waup, s5d, wglu, wout, postw, prew, wr, br)


def _rank_kernel(route_ref, rank_ref, count_ref, carry_ref):
    i = pl.program_id(0)

    @pl.when(i == 0)
    def _():
        carry_ref[...] = jnp.zeros_like(carry_ref)

    route = route_ref[...]
    r = route.shape[0]
    lane = lax.broadcasted_iota(jnp.int32, route.shape, 1).astype(F32)
    oh0 = (lane == route[:, 0:1]).astype(F32)
    oh1 = (lane == route[:, 1:2]).astype(F32)
    both = oh0 + oh1
    row = lax.broadcasted_iota(jnp.int32, (r, r), 0)
    col = lax.broadcasted_iota(jnp.int32, (r, r), 1)
    before = _bdot((row > col).astype(F32), both) + carry_ref[...]
    rank0 = jnp.sum(oh0 * before, axis=-1, keepdims=True)
    rank1 = jnp.sum(oh1 * before, axis=-1, keepdims=True)
    lane_i = lax.broadcasted_iota(jnp.int32, route.shape, 1)
    rank_ref[...] = jnp.where(lane_i == 0, rank0, jnp.where(lane_i == 1, rank1, 0.0))
    carry_ref[...] = carry_ref[...] + jnp.sum(both, axis=0, keepdims=True)
    count_ref[...] = carry_ref[...]


def _expert_ranks(route):
    rows = route.shape[0]
    r = RANK_TILE
    return pl.pallas_call(
        _rank_kernel,
        out_shape=[jax.ShapeDtypeStruct((rows, LANES), F32), jax.ShapeDtypeStruct((1, LANES), F32)],
        grid=(rows // r,),
        in_specs=[pl.BlockSpec((r, LANES), lambda i: (i, 0))],
        out_specs=[pl.BlockSpec((r, LANES), lambda i: (i, 0)), pl.BlockSpec((1, LANES), lambda i: (0, 0))],
        scratch_shapes=[pltpu.VMEM((1, LANES), F32)],
        compiler_params=_cparams(("arbitrary",)),
        name="expert_ranks",
    )(route)


def _token_copy(src_ref, src_token, dst_ref, dst_token, sem):
    r = TOKEN_TILE_ROWS
    return pltpu.make_async_copy(src_ref.at[pl.ds(pl.multiple_of(src_token * r, r), r), :],
                                 dst_ref.at[pl.ds(dst_token * r, r), :], sem)


def _row_token_kernel(pos_ref, fill_ref, inv_ref, sem):
    i = pl.program_id(0)
    tm = pos_ref.shape[-1] // 2
    base = i * tm

    @pl.when(i == 0)
    def _():
        fill = pltpu.make_async_copy(fill_ref, inv_ref, sem)
        fill.start()
        fill.wait()

    def body(t, _):
        for s in range(2):
            p = pos_ref[0, 0, s * tm + t]
            inv_ref[p] = base + t
        return 0

    lax.fori_loop(0, tm, body, 0, unroll=16)


def _row_tokens(pos3, n_rows):
    n_tiles, _, two_tm = pos3.shape
    n_tok = n_tiles * two_tm // 2
    fill = jnp.arange(n_rows, dtype=jnp.int32) % n_tok
    return pl.pallas_call(
        _row_token_kernel,
        out_shape=jax.ShapeDtypeStruct((n_rows,), jnp.int32),
        grid=(n_tiles,),
        in_specs=[pl.BlockSpec((1, 1, two_tm), lambda i: (i, 0, 0), memory_space=pltpu.SMEM),
                  pl.BlockSpec(memory_space=pl.ANY)],
        out_specs=pl.BlockSpec(memory_space=pltpu.SMEM),
        scratch_shapes=[pltpu.SemaphoreType.DMA],
        compiler_params=_cparams(("arbitrary",)),
        name="row_tokens",
    )(pos3, fill)


def _gather_rows(src_ref, idx_ref, n, dst_ref, sem):
    for t in range(n):
        _token_copy(src_ref, idx_ref[0, 0, t], dst_ref, t, sem).start()


def _wait_rows(src_ref, dst_ref, sem):
    pltpu.make_async_copy(src_ref.at[pl.ds(0, dst_ref.shape[0]), :], dst_ref, sem).wait()


def _gather_ring(i, idx_refs, src_ref, n, buf_ref, sems):
    nb = GATHER_BUFFERS

    @pl.when(i == 0)
    def _():
        for k in range(nb - 1):
            _gather_rows(src_ref, idx_refs[k], n, buf_ref.at[k], sems.at[k])

    slot = i % nb
    _wait_rows(src_ref, buf_ref.at[slot], sems.at[slot])
    ahead = (i + nb - 1) % nb
    _gather_rows(src_ref, idx_refs[nb - 1], n, buf_ref.at[ahead], sems.at[ahead])
    return slot


def _gather_drain(i, n_tiles, src_ref, buf_ref, sems):
    @pl.when(i == n_tiles - 1)
    def _():
        for k in range(1, GATHER_BUFFERS):
            s = (i + k) % GATHER_BUFFERS
            _wait_rows(src_ref, buf_ref.at[s], sems.at[s])


def _ring_index_specs(block, n_tiles):
    def spec(k):
        return pl.BlockSpec(block, lambda i, *_: (jnp.minimum(i + k, n_tiles - 1), 0, 0), memory_space=pltpu.SMEM)

    return [spec(k) for k in range(GATHER_BUFFERS)]


def _expert_kernel(te_ref, inv0_ref, inv1_ref, inv2_ref, h2_ref, wg_ref, wu_ref, wd_ref, ys_ref,
                   xbuf_ref, sems, wgb_ref, wub_ref, wdb_ref, *, n_tiles):
    i = pl.program_id(0)
    te = ys_ref.shape[0] // TOKEN_TILE_ROWS

    @pl.when(jnp.logical_or(i == 0, te_ref[i] != te_ref[jnp.maximum(i - 1, 0)]))
    def _():
        wgb_ref[...] = wg_ref[0].astype(BF16)
        wub_ref[...] = wu_ref[0].astype(BF16)
        wdb_ref[...] = wd_ref[0].astype(BF16)

    slot = _gather_ring(i, (inv0_ref, inv1_ref, inv2_ref), h2_ref, te, xbuf_ref, sems)
    x = _load_token_tiles(xbuf_ref.at[slot], 0, te).astype(BF16)
    hg = jnp.dot(x, wgb_ref[...], preferred_element_type=F32)
    hu = jnp.dot(x, wub_ref[...], preferred_element_type=F32)
    _store_token_tiles(ys_ref, _bdot(_silu(hg) * hu, wdb_ref[...]))
    _gather_drain(i, n_tiles, h2_ref, xbuf_ref, sems)


def _expert_mlp(tile_expert, inv3, h2, wg, wu, wd):
    n_tiles, _, te = inv3.shape
    rows = te * TOKEN_TILE_ROWS
    return pl.pallas_call(
        functools.partial(_expert_kernel, n_tiles=n_tiles),
        out_shape=jax.ShapeDtypeStruct((n_tiles * rows, LANES), F32),
        grid_spec=pltpu.PrefetchScalarGridSpec(
            num_scalar_prefetch=1,
            grid=(n_tiles,),
            in_specs=_ring_index_specs((1, 1, te), n_tiles) + [
                      pl.BlockSpec(memory_space=pl.ANY),
                      pl.BlockSpec((1, D_MODEL, EXPERT_FF), lambda i, e: (e[i], 0, 0)),
                      pl.BlockSpec((1, D_MODEL, EXPERT_FF), lambda i, e: (e[i], 0, 0)),
                      pl.BlockSpec((1, EXPERT_FF, D_MODEL), lambda i, e: (e[i], 0, 0))],
            out_specs=pl.BlockSpec((rows, LANES), lambda i, e: (i, 0)),
            scratch_shapes=[pltpu.VMEM((GATHER_BUFFERS, rows, LANES), F32),
                            pltpu.SemaphoreType.DMA((GATHER_BUFFERS,)),
                            pltpu.VMEM((D_MODEL, EXPERT_FF), BF16), pltpu.VMEM((D_MODEL, EXPERT_FF), BF16),
                            pltpu.VMEM((EXPERT_FF, D_MODEL), BF16)]),
        compiler_params=_cparams(("arbitrary",)),
        name="expert_mlp",
    )(tile_expert, inv3, inv3, inv3, h2, wg, wu, wd)


def _combine_kernel(pos0_ref, pos1_ref, pos2_ref, x1_ref, mod_ref, route_ref, postw_ref, ys_ref, out_ref, buf_ref,
                    sems, *, n_tiles):
    tm = x1_ref.shape[0]
    i = pl.program_id(0)
    slot = _gather_ring(i, (pos0_ref, pos1_ref, pos2_ref), ys_ref, 2 * tm, buf_ref, sems)
    route = route_ref[...]
    moe = (route[:, 2:3] * _load_token_tiles(buf_ref.at[slot], 0, tm)
           + route[:, 3:4] * _load_token_tiles(buf_ref.at[slot], tm, tm))
    out_ref[...] = x1_ref[...] + mod_ref[0][5:6, :] * _rms(moe, postw_ref[...])
    _gather_drain(i, n_tiles, ys_ref, buf_ref, sems)


def _combine(pos3, x1, mod3, route, postw, ys, *, bsz, seq):
    tm = TOK_TILE
    nt = seq // tm
    n_tiles = bsz * nt
    row_map = lambda i: (i, 0)
    return pl.pallas_call(
        functools.partial(_combine_kernel, n_tiles=n_tiles),
        out_shape=jax.ShapeDtypeStruct(x1.shape, F32),
        grid=(n_tiles,),
        in_specs=_ring_index_specs((1, 1, 2 * tm), n_tiles) + [
                  pl.BlockSpec((tm, D_MODEL), row_map),
                  pl.BlockSpec((1, N_MOD, D_MODEL), lambda i: (i // nt, 0, 0)),
                  pl.BlockSpec((tm, LANES), row_map),
                  pl.BlockSpec((1, D_MODEL), lambda i: (0, 0)),
                  pl.BlockSpec(memory_space=pl.ANY)],
        out_specs=pl.BlockSpec((tm, D_MODEL), row_map),
        scratch_shapes=[pltpu.VMEM((GATHER_BUFFERS, 2 * tm * TOKEN_TILE_ROWS, LANES), F32),
                        pltpu.SemaphoreType.DMA((GATHER_BUFFERS,))],
        compiler_params=_cparams(("arbitrary",)),
        name="combine",
    )(pos3, pos3, pos3, x1, mod3, route, postw, ys)


def _lane_pad(v, n=LANES):
    v = v.reshape(1, -1)
    return jnp.pad(v, ((0, 0), (0, n - v.shape[1])))


def kernel(x, c, ctx, c_ctx, w_mod, b_mod, pre_norm_w, post_norm_w, w_in, conv_w, a_log, dt_bias, o_norm_w,
           w_a_up, s5_lam_re, s5_lam_im, s5_log_step, s5_b_re, s5_b_im, s5_c_re, s5_c_im, s5_d, w_glu, w_out,
           w_router_group, b_router_group, w_router_expert, b_router_expert, w_gate_e, w_up_e, w_down_e):
    assert w_mod.shape[0] == 1, "single-layer block"
    bsz, seq, _ = x.shape
    ctx_len = ctx.shape[1]
    row = lambda v: v.reshape(1, -1)

    c16 = jnp.concatenate([c, c_ctx[None, :], jnp.zeros((2 * SUBLANES - bsz - 1, D_MODEL), F32)], axis=0)
    mod3 = _modulation(c16, w_mod[0], row(b_mod[0])).reshape(2 * SUBLANES, N_MOD, D_MODEL)

    wi = w_in[0]
    o_z, o_a, o_u, o_g = 3 * DN_WIDTH, 4 * DN_WIDTH, 4 * DN_WIDTH + 4 * DN_HEADS, 4 * DN_WIDTH + 4 * DN_HEADS + S5_WIDTH
    wts_in = (wi[:, :o_z].astype(BF16), wi[:, o_z:o_a].astype(BF16),
              jnp.pad(wi[:, o_a:o_u], ((0, 0), (0, LANES - 4 * DN_HEADS))).astype(BF16),
              wi[:, o_u:o_g].astype(BF16), wi[:, o_g:].astype(BF16))
    convw = jnp.pad(conv_w[0], ((0, SUBLANES - CONV_K), (0, 0)))
    alog = _lane_pad(a_log[0])
    dtb = _lane_pad(dt_bias[0])
    prew0, prew1 = row(pre_norm_w[0, 0]), row(pre_norm_w[0, 1])
    postw0, postw1 = row(post_norm_w[0, 0]), row(post_norm_w[0, 1])

    bar_re, bar_im, bb_re, bb_im = _s5_discretise(s5_lam_re[0], s5_lam_im[0], s5_log_step[0], s5_b_re[0], s5_b_im[0])
    s5w = _s5_weights(bar_re, bar_im, bb_re, bb_im, s5_c_re[0], s5_c_im[0], bsz)

    qkvn_c, gb_c, u_c = _input_projection(ctx.reshape(bsz * ctx_len, D_MODEL), mod3, bsz, 0, prew0, wts_in, convw,
                                          alog, dtb, bsz=bsz, seq=ctx_len, seg_len=ctx_len, full=False)
    s_zero = jnp.zeros((bsz, 2 * DN_HEADS, DN_HEAD_DIM, DN_HEAD_DIM), F32)
    _, _, s_ctx = _delta_rule(qkvn_c, gb_c, s_zero, bsz=bsz, seq=ctx_len)
    x_zero = jnp.zeros((2, bsz, S5_LANES), F32)
    (x_ctx,) = _s5_scan(u_c.reshape(bsz, ctx_len, S5_WIDTH), s5w, x_zero, emit_y=False)

    x2d = x.reshape(bsz * seq, D_MODEL)
    qkvn, gb, u, z, gates = _input_projection(x2d, mod3, 0, 1, prew0, wts_in, convw, alog, dtb,
                                                 bsz=bsz, seq=seq, seg_len=GRID_W, full=True)
    o_f, o_b, _ = _delta_rule(qkvn, gb, s_ctx, bsz=bsz, seq=seq)
    y_f, y_b, _ = _s5_scan(u.reshape(bsz, seq, S5_WIDTH), s5w, x_ctx, emit_y=True)
    rows2d = lambda a: a.reshape(bsz * seq, S5_WIDTH)

    w_router = jnp.pad(jnp.concatenate([w_router_group[0], w_router_expert[0]], axis=1),
                       ((0, 0), (0, LANES - N_EXPERT_GROUPS - N_EXPERTS))).astype(BF16)
    b_router = _lane_pad(jnp.concatenate([b_router_group[0], b_router_expert[0]]))
    wts_merge = (row(o_norm_w[0]), w_a_up[0].astype(BF16), row(s5_d[0]), w_glu[0].astype(BF16),
                 w_out[0].astype(BF16), postw0, prew1, w_router, b_router)
    x1, h2, route = _merge(x2d, mod3, o_f, o_b, z, rows2d(y_f), rows2d(y_b), u, gates, wts_merge, bsz=bsz, seq=seq)

    rank, count = _expert_ranks(route)
    n_tok = bsz * seq
    te = EXPERT_TILE
    counts = count[0, :N_EXPERTS].astype(jnp.int32)
    padded = ((counts + te - 1) // te) * te
    ends = jnp.cumsum(padded)
    starts = ends - padded
    ids = route[:, :2].astype(jnp.int32)
    one_hot = ids[:, :, None] == jnp.arange(N_EXPERTS, dtype=jnp.int32)
    pos = jnp.sum(jnp.where(one_hot, starts, 0), axis=-1) + rank[:, :2].astype(jnp.int32)
    pos3 = jnp.swapaxes(pos.reshape(n_tok // TOK_TILE, TOK_TILE, 2), 1, 2).reshape(n_tok // TOK_TILE, 1, 2 * TOK_TILE)
    n_sorted = 2 * n_tok + N_EXPERTS * te
    tile_start = jnp.arange(n_sorted // te, dtype=jnp.int32) * te
    tile_expert = jnp.minimum(jnp.sum((ends[None, :] <= tile_start[:, None]).astype(jnp.int32), axis=1), N_EXPERTS - 1)

    inv3 = _row_tokens(pos3, n_sorted).reshape(n_sorted // te, 1, te)
    ys = _expert_mlp(tile_expert, inv3, h2, w_gate_e[0], w_up_e[0], w_down_e[0])
    out = _combine(pos3, x1, mod3, route, postw1, ys, bsz=bsz, seq=seq)
    return out.reshape(bsz, seq, D_MODEL)
```

```python
import functools

import jax
import jax.numpy as jnp
from jax import lax
from jax.experimental import pallas as pl
from jax.experimental.pallas import tpu as pltpu

F32 = jnp.float32
BF16 = jnp.bfloat16
HIGHEST = lax.Precision.HIGHEST

D_MODEL = 1024
GRID_W = 64
DN_HEADS = 4
DN_HEAD_DIM = 128
DN_WIDTH = DN_HEADS * DN_HEAD_DIM
DN_CHUNK = 64
CONV_K = 5
S5_WIDTH = D_MODEL - DN_WIDTH
S5_GROUP = 16
S5_GROUPS = S5_WIDTH // S5_GROUP
S5_STATE = 64
S5_HALF_GROUPS = S5_GROUPS // 2
S5_HALF_STATE = S5_HALF_GROUPS * S5_STATE
S5_LANES = 2 * 2 * S5_HALF_STATE
N_EXPERT_GROUPS = 4
EXPERTS_PER_GROUP = 8
N_EXPERTS = N_EXPERT_GROUPS * EXPERTS_PER_GROUP
EXPERT_FF = 512
NORM_EPS = 1e-6
L2_EPS = 1e-6
N_MOD = 6
LANES = 128
SUBLANES = 8
TOKEN_TILE_ROWS = D_MODEL // LANES
VMEM_LIMIT = 56 * 1024 * 1024

TOK_TILE = 512
PROJ_TILE = 512
S5_TILE = 64
DN_CHUNKS_PER_STEP = 4
EXPERT_TILE = 512
RANK_TILE = 512
GATHER_BUFFERS = 3


def _cparams(sem):
    return pltpu.CompilerParams(dimension_semantics=sem, vmem_limit_bytes=VMEM_LIMIT)


def _bdot(a, b):
    return jnp.dot(a.astype(BF16), b.astype(BF16), preferred_element_type=F32)


def _bdot_nt(a, b):
    return lax.dot_general(a.astype(BF16), b.astype(BF16), (((1,), (1,)), ((), ())),
                           preferred_element_type=F32)


def _bdot_tn(a, b):
    return lax.dot_general(a.astype(BF16), b.astype(BF16), (((0,), (0,)), ((), ())),
                           preferred_element_type=F32)


def _silu(x):
    return x * jax.nn.sigmoid(x)


def _rms(x, w):
    return x * lax.rsqrt(jnp.mean(x * x, axis=-1, keepdims=True) + NORM_EPS) * w


def _store_token_tiles(ref, val):
    n = val.shape[0]
    for j in range(TOKEN_TILE_ROWS):
        ref[pl.ds(j, n, stride=TOKEN_TILE_ROWS), :] = val[:, j * LANES:(j + 1) * LANES]


def _load_token_tiles(ref, first_token, n):
    return jnp.concatenate([ref[pl.ds(first_token * TOKEN_TILE_ROWS + j, n, stride=TOKEN_TILE_ROWS), :]
                            for j in range(TOKEN_TILE_ROWS)], axis=1)


def _mod_kernel(c_ref, w_ref, b_ref, o_ref):
    o_ref[...] = jnp.dot(_silu(c_ref[...]), w_ref[...], precision=HIGHEST,
                         preferred_element_type=F32) + b_ref[...]


def _modulation(c16, w_mod, b_mod):
    n = w_mod.shape[1]
    return pl.pallas_call(
        _mod_kernel,
        out_shape=jax.ShapeDtypeStruct((c16.shape[0], n), F32),
        grid=(n // D_MODEL,),
        in_specs=[pl.BlockSpec(c16.shape, lambda j: (0, 0)),
                  pl.BlockSpec((D_MODEL, D_MODEL), lambda j: (0, j)),
                  pl.BlockSpec((1, D_MODEL), lambda j: (0, j))],
        out_specs=pl.BlockSpec((c16.shape[0], D_MODEL), lambda j: (0, j)),
        compiler_params=_cparams(("arbitrary",)),
        name="modulation",
    )(c16, w_mod, b_mod)


def _inproj_kernel(x_ref, mod_ref, prew_ref, wqkv_ref, wz_ref, wab_ref, wu_ref, wg_ref, convw_ref,
                   alog_ref, dtb_ref, *out_refs, seg_len, full):
    if full:
        qkvn_ref, gb_ref, u_ref, z_ref, gates_ref = out_refs
    else:
        qkvn_ref, gb_ref, u_ref = out_refs
    tm = x_ref.shape[0]
    mod = mod_ref[0]
    h = _rms(x_ref[...], prew_ref[...]) * (1.0 + mod[1:2, :]) + mod[0:1, :]
    hb = h.astype(BF16)

    qkv = jnp.dot(hb, wqkv_ref[...], preferred_element_type=F32)
    pos = lax.broadcasted_iota(jnp.int32, (tm, 1), 0) % seg_len
    acc = qkv * convw_ref[CONV_K // 2:CONV_K // 2 + 1, :]
    for k in range(CONV_K):
        s = k - CONV_K // 2
        if s == 0:
            continue
        shifted = pltpu.roll(qkv, (-s) % tm, 0)
        ok = (pos + s >= 0) & (pos + s < seg_len)
        acc = acc + jnp.where(ok, shifted, 0.0) * convw_ref[k:k + 1, :]
    act = _silu(acc)
    for j in range(3 * DN_HEADS):
        sl = slice(j * DN_HEAD_DIM, (j + 1) * DN_HEAD_DIM)
        a = act[:, sl]
        if j < 2 * DN_HEADS:
            a = a * lax.rsqrt(jnp.sum(a * a, axis=-1, keepdims=True) + L2_EPS)
        qkvn_ref[:, sl] = a

    ab = jnp.dot(hb, wab_ref[...], preferred_element_type=F32)
    xa = ab + dtb_ref[...]
    softplus = jnp.maximum(xa, 0.0) + jnp.log1p(jnp.exp(-jnp.abs(xa)))
    g = -jnp.exp(alog_ref[...]) * softplus
    lane = lax.broadcasted_iota(jnp.int32, ab.shape, 1)
    gb_ref[...] = jnp.where(lane < 2 * DN_HEADS, g, jax.nn.sigmoid(ab))

    u_ref[...] = jnp.dot(hb, wu_ref[...], preferred_element_type=F32)
    if full:
        z_ref[...] = jnp.dot(hb, wz_ref[...], preferred_element_type=F32)
        gates_ref[...] = jnp.dot(hb, wg_ref[...], preferred_element_type=F32)


def _input_projection(x2d, mod3, mod_row0, mod_row_stride, prew, wts, convw, alog, dtb, *, bsz, seq, seg_len, full):
    wqkv, wz, wab, wu, wg = wts
    tm = min(PROJ_TILE, seq)
    nt = seq // tm
    rows = bsz * seq
    row_map = lambda b, i: (b * nt + i, 0)
    const = lambda b, i: (0, 0)
    resident = lambda a: pl.BlockSpec(a.shape, const, pipeline_mode=pl.Buffered(1))
    out_shape = [jax.ShapeDtypeStruct((rows, 3 * DN_WIDTH), F32),
                 jax.ShapeDtypeStruct((rows, LANES), F32),
                 jax.ShapeDtypeStruct((rows, S5_WIDTH), F32)]
    out_specs = [pl.BlockSpec((tm, 3 * DN_WIDTH), row_map),
                 pl.BlockSpec((tm, LANES), row_map),
                 pl.BlockSpec((tm, S5_WIDTH), row_map)]
    if full:
        out_shape += [jax.ShapeDtypeStruct((rows, DN_WIDTH), F32),
                      jax.ShapeDtypeStruct((rows, 2 * D_MODEL), F32)]
        out_specs += [pl.BlockSpec((tm, DN_WIDTH), row_map),
                      pl.BlockSpec((tm, 2 * D_MODEL), row_map)]
    return pl.pallas_call(
        functools.partial(_inproj_kernel, seg_len=seg_len, full=full),
        out_shape=out_shape,
        grid=(bsz, nt),
        in_specs=[pl.BlockSpec((tm, D_MODEL), row_map),
                  pl.BlockSpec((1, N_MOD, D_MODEL), lambda b, i: (mod_row0 + mod_row_stride * b, 0, 0)),
                  pl.BlockSpec((1, D_MODEL), const),
                  resident(wqkv), resident(wz), resident(wab), resident(wu), resident(wg),
                  pl.BlockSpec(convw.shape, const),
                  pl.BlockSpec((1, LANES), const),
                  pl.BlockSpec((1, LANES), const)],
        out_specs=out_specs,
        compiler_params=_cparams(("arbitrary", "arbitrary")),
        name="input_projection_full" if full else "input_projection_ctx",
    )(x2d, mod3, prew, wqkv, wz, wab, wu, wg, convw, alog, dtb)


def _chunk_cumsum(g, reverse):
    n = g.shape[0]
    row = lax.broadcasted_iota(jnp.int32, (n, 1), 0)
    s = 1
    while s < n:
        if reverse:
            g = g + jnp.where(row < n - s, pltpu.roll(g, n - s, 0), 0.0)
        else:
            g = g + jnp.where(row >= s, pltpu.roll(g, s, 0), 0.0)
        s *= 2
    return g


def _delta_kernel(qf_ref, kf_ref, vf_ref, gbf_ref, qb_ref, kb_ref, vb_ref, gbb_ref, s0_ref,
                  of_ref, ob_ref, sout_ref, state_ref, *, n_steps, chunks):
    c = DN_CHUNK
    i = pl.program_id(1)

    @pl.when(i == 0)
    def _():
        state_ref[...] = s0_ref[0]

    row = lax.broadcasted_iota(jnp.int32, (c, c), 0)
    col = lax.broadcasted_iota(jnp.int32, (c, c), 1)
    eye = (row == col).astype(F32)
    scale = DN_HEAD_DIM ** -0.5
    dirs = ((qf_ref, kf_ref, vf_ref, gbf_ref, of_ref), (qb_ref, kb_ref, vb_ref, gbb_ref, ob_ref))
    chains = {}
    for d, (q_ref, k_ref, v_ref, gb_ref, o_ref) in enumerate(dirs):
        incl = (row >= col) if d == 0 else (row <= col)
        strict = (row > col) if d == 0 else (row < col)
        last = c - 1 if d == 0 else 0
        for sub in range(chunks):
            rs = slice(sub * c, (sub + 1) * c)
            gb = gb_ref[rs, :]
            gcum = _chunk_cumsum(gb, reverse=d == 1)
            gcum_t = gcum.T
            for h in range(DN_HEADS):
                lane = d * DN_HEADS + h
                hs = slice(h * DN_HEAD_DIM, (h + 1) * DN_HEAD_DIM)
                g_col = gcum[:, lane:lane + 1]
                g_row = gcum_t[lane:lane + 1, :]
                g_last = gcum[last:last + 1, lane:lane + 1]
                beta = gb[:, 2 * DN_HEADS + lane:2 * DN_HEADS + lane + 1]
                dec = jnp.where(incl, jnp.exp(jnp.where(incl, g_col - g_row, 0.0)), 0.0)
                q = q_ref[rs, hs] * scale
                k = k_ref[rs, hs]
                v = v_ref[rs, hs]
                k_beta = k * beta
                e_g = jnp.exp(g_col)
                chains[d, sub, h] = dict(
                    lane=lane, rs=rs, hs=hs, o_ref=o_ref, strict=strict, dec=dec, q=q, k=k, k_beta=k_beta, e_g=e_g,
                    rhs=jnp.concatenate([v * beta, k_beta * e_g], axis=1),
                    k_tail=k * jnp.exp(g_last - g_col), e_last=jnp.exp(g_last))
    every = list(chains.values())
    for ch in every:
        kq = _bdot_nt(jnp.concatenate([ch['k_beta'], ch['q']], axis=0), ch['k'])
        ch['npow'] = -jnp.where(ch['strict'], kq[:c] * ch['dec'], 0.0)
        ch['attn'] = kq[c:] * ch['dec']
        ch['t_inv'] = eye + ch['npow']
    for _ in range(5):
        for ch in every:
            ch['npow'] = _bdot(ch['npow'], ch['npow'])
        for ch in every:
            ch['t_inv'] = ch['t_inv'] + _bdot(ch['t_inv'], ch['npow'])
    for ch in every:
        ch['sol'] = _bdot(ch['t_inv'], ch['rhs'])
        ch['q_dec'] = ch['q'] * ch['e_g']

    states = [state_ref[lane] for lane in range(2 * DN_HEADS)]
    for step in range(chunks):
        group = [chains[d, step if d == 0 else chunks - 1 - step, h] for d in range(2) for h in range(DN_HEADS)]
        for ch in group:
            ch['wq'] = _bdot(jnp.concatenate([ch['sol'][:, DN_HEAD_DIM:], ch['q_dec']], axis=0), states[ch['lane']])
        for ch in group:
            ch['v_new'] = ch['sol'][:, :DN_HEAD_DIM] - ch['wq'][:c]
            ch['o_ref'][ch['rs'], ch['hs']] = ch['wq'][c:] + _bdot(ch['attn'], ch['v_new'])
        for ch in group:
            states[ch['lane']] = states[ch['lane']] * ch['e_last'] + _bdot_tn(ch['k_tail'], ch['v_new'])
    for lane in range(2 * DN_HEADS):
        state_ref[lane] = states[lane]

    @pl.when(i == n_steps - 1)
    def _():
        sout_ref[0] = state_ref[...]


def _delta_rule(qkvn, gb, s0, *, bsz, seq):
    chunks = min(DN_CHUNKS_PER_STEP, seq // DN_CHUNK)
    rows = DN_CHUNK * chunks
    nb = seq // rows
    fwd = lambda j: (lambda b, i: (b * nb + i, j))
    bwd = lambda j: (lambda b, i: (b * nb + nb - 1 - i, j))
    blk = lambda m: pl.BlockSpec((rows, DN_WIDTH), m)
    gblk = lambda m: pl.BlockSpec((rows, LANES), m)
    st = pl.BlockSpec((1, 2 * DN_HEADS, DN_HEAD_DIM, DN_HEAD_DIM), lambda b, i: (b, 0, 0, 0))
    return pl.pallas_call(
        functools.partial(_delta_kernel, n_steps=nb, chunks=chunks),
        out_shape=[jax.ShapeDtypeStruct((bsz * seq, DN_WIDTH), F32),
                   jax.ShapeDtypeStruct((bsz * seq, DN_WIDTH), F32),
                   jax.ShapeDtypeStruct(s0.shape, F32)],
        grid=(bsz, nb),
        in_specs=[blk(fwd(0)), blk(fwd(1)), blk(fwd(2)), gblk(fwd(0)),
                  blk(bwd(0)), blk(bwd(1)), blk(bwd(2)), gblk(bwd(0)), st],
        out_specs=[blk(fwd(0)), blk(bwd(0)), st],
        scratch_shapes=[pltpu.VMEM((2 * DN_HEADS, DN_HEAD_DIM, DN_HEAD_DIM), F32)],
        compiler_params=_cparams(("arbitrary", "arbitrary")),
        name="delta_rule",
    )(qkvn, qkvn, qkvn, gb, qkvn, qkvn, qkvn, gb, s0)


def _s5_disc_kernel(lre_ref, lim_ref, ls_ref, bre_ref, bim_ref, are_ref, aim_ref, bbre_ref, bbim_ref):
    lam_re = lre_ref[...]
    lam_im = lim_ref[...]
    dt = jnp.exp(ls_ref[...])
    mag = jnp.exp(lam_re * dt)
    ang = lam_im * dt
    bar_re = mag * jnp.cos(ang)
    bar_im = mag * jnp.sin(ang)
    den = lam_re * lam_re + lam_im * lam_im
    coef_re = ((bar_re - 1.0) * lam_re + bar_im * lam_im) / den
    coef_im = (bar_im * lam_re - (bar_re - 1.0) * lam_im) / den
    are_ref[...] = bar_re
    aim_ref[...] = bar_im
    bbre_ref[...] = coef_re * bre_ref[...] - coef_im * bim_ref[...]
    bbim_ref[...] = coef_re * bim_ref[...] + coef_im * bre_ref[...]


def _s5_discretise(lam_re, lam_im, log_step, b_re, b_im):
    rep = lambda t: jnp.repeat(t.reshape(2 * S5_GROUPS, -1), S5_GROUP, axis=0)
    rows = 2 * S5_GROUPS * S5_GROUP
    tr = lambda t: jnp.swapaxes(t, -1, -2).reshape(rows, S5_STATE)
    outs = pl.pallas_call(
        _s5_disc_kernel,
        out_shape=[jax.ShapeDtypeStruct((rows, S5_STATE), F32)] * 4,
        name="s5_discretise",
    )(rep(lam_re), rep(lam_im), rep(log_step[..., None]), tr(b_re), tr(b_im))
    shp = (2, S5_GROUPS, S5_GROUP, S5_STATE)
    bar_re, bar_im, bb_re, bb_im = (o.reshape(shp) for o in outs)
    return bar_re[:, :, 0], bar_im[:, :, 0], bb_re, bb_im


def _s5_kernel(uf_ref, ub_ref, wb_ref, wcre_ref, wcim_ref, are_ref, aim_ref, x0_ref, *rest, tt, n_tiles, emit_y):
    if emit_y:
        yf_ref, yb_ref, xout_ref, buf_ref, xst_ref, us_ref, ys_ref = rest
    else:
        xout_ref, buf_ref, xst_ref, us_ref = rest
        yf_ref = yb_ref = ys_ref = None
    i = pl.program_id(0)
    bsz = uf_ref.shape[0]
    hs = S5_HALF_STATE
    n_lane_tiles = S5_WIDTH // LANES
    blk = 512
    u_refs = (uf_ref, ub_ref)
    y_refs = (yf_ref, yb_ref)

    @pl.when(i == 0)
    def _():
        xst_ref[...] = x0_ref[...]

    def input_matmul(hh):
        for d in range(2):
            u_h = jnp.concatenate([us_ref[d, 2 * hh], us_ref[d, 2 * hh + 1]], axis=1)
            buf_ref[d, :, hh * 2 * hs:(hh + 1) * 2 * hs] = _bdot(u_h, wb_ref[d, hh])

    def scan(hh):
        for p in range(hs // blk):
            lr = hh * 2 * hs + p * blk
            li = lr + hs
            a = [(are_ref[d, :, lr:lr + blk], aim_ref[d, :, lr:lr + blk]) for d in range(2)]
            x = [(xst_ref[d, :, lr:lr + blk], xst_ref[d, :, li:li + blk]) for d in range(2)]
            for t in range(tt):
                for d in range(2):
                    r0 = (t if d == 0 else tt - 1 - t) * SUBLANES
                    a_re, a_im = a[d]
                    x_re, x_im = x[d]
                    n_re = a_re * x_re - a_im * x_im + buf_ref[d, r0:r0 + SUBLANES, lr:lr + blk]
                    n_im = a_re * x_im + a_im * x_re + buf_ref[d, r0:r0 + SUBLANES, li:li + blk]
                    buf_ref[d, r0:r0 + SUBLANES, lr:lr + blk] = n_re
                    buf_ref[d, r0:r0 + SUBLANES, li:li + blk] = n_im
                    x[d] = (n_re, n_im)
            for d in range(2):
                xst_ref[d, :, lr:lr + blk] = x[d][0]
                xst_ref[d, :, li:li + blk] = x[d][1]

    def output_matmul(hh):
        for d in range(2):
            x_re = buf_ref[d, :, hh * 2 * hs:hh * 2 * hs + hs]
            x_im = buf_ref[d, :, hh * 2 * hs + hs:(hh + 1) * 2 * hs]
            y_h = _bdot(x_re, wcre_ref[d, hh]) - _bdot(x_im, wcim_ref[d, hh])
            ys_ref[d, 2 * hh] = y_h[:, :LANES]
            ys_ref[d, 2 * hh + 1] = y_h[:, LANES:]

    for d in range(2):
        for b in range(bsz):
            for j in range(n_lane_tiles):
                us_ref[d, j, pl.ds(b, tt, stride=bsz), :] = u_refs[d][b, :, j * LANES:(j + 1) * LANES]
    input_matmul(0)
    input_matmul(1)
    scan(0)
    if emit_y:
        output_matmul(0)
    scan(1)
    if emit_y:
        output_matmul(1)
        for d in range(2):
            for b in range(bsz):
                for j in range(n_lane_tiles):
                    y_refs[d][b, :, j * LANES:(j + 1) * LANES] = ys_ref[d, j, pl.ds(b, tt, stride=bsz), :]

    @pl.when(i == n_tiles - 1)
    def _():
        xout_ref[...] = xst_ref[...]


def _s5_scan(u3, s5w, x0, *, emit_y):
    wb, wc_re, wc_im, a_re, a_im = s5w
    bsz, seq, _ = u3.shape
    assert bsz == SUBLANES, "one time step of all batches must fill one sublane group"
    tt = S5_TILE
    nt = seq // tt
    rows = tt * bsz
    full = lambda a: pl.BlockSpec(a.shape, lambda i: (0,) * a.ndim)
    ublk = lambda m: pl.BlockSpec((bsz, tt, S5_WIDTH), m)
    fwd = lambda i: (0, i, 0)
    bwd = lambda i: (0, nt - 1 - i, 0)
    out_shape = [jax.ShapeDtypeStruct(x0.shape, F32)]
    out_specs = [full(x0)]
    scratch = [pltpu.VMEM((2, rows, S5_LANES), F32), pltpu.VMEM(x0.shape, F32),
               pltpu.VMEM((2, S5_WIDTH // LANES, rows, LANES), F32)]
    if emit_y:
        out_shape = [jax.ShapeDtypeStruct(u3.shape, F32)] * 2 + out_shape
        out_specs = [ublk(fwd), ublk(bwd)] + out_specs
        scratch.append(pltpu.VMEM((2, S5_WIDTH // LANES, rows, LANES), F32))
    return pl.pallas_call(
        functools.partial(_s5_kernel, tt=tt, n_tiles=nt, emit_y=emit_y),
        out_shape=out_shape,
        grid=(nt,),
        in_specs=[ublk(fwd), ublk(bwd), full(wb), full(wc_re), full(wc_im), full(a_re), full(a_im), full(x0)],
        out_specs=out_specs,
        scratch_shapes=scratch,
        compiler_params=_cparams(("arbitrary",)),
        name="s5_scan_y" if emit_y else "s5_scan_state",
    )(u3, u3, wb, wc_re, wc_im, a_re, a_im, x0)


def _s5_weights(bar_re, bar_im, bb_re, bb_im, c_re, c_im, bsz):
    hg = S5_HALF_GROUPS
    eye = jnp.eye(hg, dtype=F32)

    def in_mat(bb):
        t = bb.reshape(2, 2, hg, S5_GROUP, S5_STATE)
        t = jnp.einsum('dhicp,ij->dhicjp', t, eye)
        return t.reshape(2, 2, hg * S5_GROUP, hg * S5_STATE)

    def out_mat(cc):
        t = cc.reshape(2, 2, hg, S5_GROUP, S5_STATE)
        t = jnp.einsum('dhicp,ij->dhipjc', t, eye)
        return t.reshape(2, 2, hg * S5_STATE, hg * S5_GROUP)

    wb = jnp.concatenate([in_mat(bb_re), in_mat(bb_im)], axis=-1).astype(BF16)
    wc_re = out_mat(c_re).astype(BF16)
    wc_im = out_mat(c_im).astype(BF16)

    def lanes(a):
        t = a.reshape(2, 2, 1, S5_HALF_STATE)
        t = jnp.broadcast_to(t, (2, 2, 2, S5_HALF_STATE)).reshape(2, 1, S5_LANES)
        return jnp.broadcast_to(t, (2, bsz, S5_LANES))

    return wb, wc_re, wc_im, lanes(bar_re), lanes(bar_im)


def _merge_kernel(x_ref, mod_ref, of_ref, ob_ref, z_ref, yf_ref, yb_ref, u_ref, gates_ref, onw_ref, waup_ref,
                  s5d_ref, wglu_ref, wout_ref, postw_ref, prew_ref, wr_ref, br_ref,
                  x1_ref, h2_ref, route_ref):
    mod = mod_ref[0]
    o_a = of_ref[...] + ob_ref[...]
    z = z_ref[...]
    parts = []
    for h in range(DN_HEADS):
        hs = slice(h * DN_HEAD_DIM, (h + 1) * DN_HEAD_DIM)
        parts.append(_rms(o_a[:, hs], onw_ref[...]) * _silu(z[:, hs]))
    y_a = _bdot(jnp.concatenate(parts, axis=1), waup_ref[...])
    y5 = yf_ref[...] + yb_ref[...] + s5d_ref[...] * u_ref[...]
    glu = _bdot(jax.nn.gelu(y5, approximate=True), wglu_ref[...])
    y_b = glu[:, :D_MODEL] * jax.nn.sigmoid(glu[:, D_MODEL:])
    gates = gates_ref[...]
    y = jax.nn.sigmoid(gates[:, :D_MODEL]) * y_a + jax.nn.sigmoid(gates[:, D_MODEL:]) * y_b
    mixed = _bdot(y, wout_ref[...])
    x1 = x_ref[...] + mod[2:3, :] * _rms(mixed, postw_ref[...])
    x1_ref[...] = x1
    h2 = _rms(x1, prew_ref[...]) * (1.0 + mod[4:5, :]) + mod[3:4, :]
    _store_token_tiles(h2_ref, h2)

    logits = _bdot(h2, wr_ref[...]) + br_ref[...]
    lane = lax.broadcasted_iota(jnp.int32, logits.shape, 1)
    neg = -jnp.inf
    big = jnp.int32(LANES)
    first = lambda hit: jnp.min(jnp.where(hit, lane, big), axis=-1, keepdims=True)
    g_logit = jnp.where(lane < N_EXPERT_GROUPS, logits, neg)
    g_max = jnp.max(g_logit, axis=-1, keepdims=True)
    g_sel = first(g_logit == g_max)
    g_w = 1.0 / jnp.sum(jnp.exp(g_logit - g_max), axis=-1, keepdims=True)
    e_idx = lane - N_EXPERT_GROUPS
    in_group = (e_idx >= 0) & (e_idx < N_EXPERTS) & ((e_idx // EXPERTS_PER_GROUP) == g_sel)
    e_logit = jnp.where(in_group, logits, neg)
    m1 = jnp.max(e_logit, axis=-1, keepdims=True)
    i1 = first(e_logit == m1)
    e_logit2 = jnp.where(lane == i1, neg, e_logit)
    m2 = jnp.max(e_logit2, axis=-1, keepdims=True)
    i2 = first(e_logit2 == m2)
    e2 = jnp.exp(m2 - m1)
    w1 = g_w / (1.0 + e2)
    w2 = g_w * e2 / (1.0 + e2)
    id1 = (i1 - N_EXPERT_GROUPS).astype(F32)
    id2 = (i2 - N_EXPERT_GROUPS).astype(F32)
    route_ref[...] = jnp.where(lane == 0, id1, jnp.where(lane == 1, id2, jnp.where(lane == 2, w1,
                               jnp.where(lane == 3, w2, 0.0))))


def _merge(x2d, mod3, o_f, o_b, z, y_f, y_b, u, gates, wts, *, bsz, seq):
    onw, waup, s5d, wglu, wout, postw, prew, wr, br = wts
    tm = PROJ_TILE
    nt = seq // tm
    rows = bsz * seq
    row_map = lambda b, i: (b * nt + i, 0)
    const = lambda b, i: (0, 0)
    cs = lambda a: pl.BlockSpec(a.shape, const, pipeline_mode=pl.Buffered(1))
    return pl.pallas_call(
        _merge_kernel,
        out_shape=[jax.ShapeDtypeStruct((rows, D_MODEL), F32),
                   jax.ShapeDtypeStruct((rows * TOKEN_TILE_ROWS, LANES), F32),
                   jax.ShapeDtypeStruct((rows, LANES), F32)],
        grid=(bsz, nt),
        in_specs=[pl.BlockSpec((tm, D_MODEL), row_map),
                  pl.BlockSpec((1, N_MOD, D_MODEL), lambda b, i: (b, 0, 0)),
                  pl.BlockSpec((tm, DN_WIDTH), row_map),
                  pl.BlockSpec((tm, DN_WIDTH), row_map),
                  pl.BlockSpec((tm, DN_WIDTH), row_map),
                  pl.BlockSpec((tm, S5_WIDTH), row_map),
                  pl.BlockSpec((tm, S5_WIDTH), row_map),
                  pl.BlockSpec((tm, S5_WIDTH), row_map),
                  pl.BlockSpec((tm, 2 * D_MODEL), row_map),
                  cs(onw), cs(waup), cs(s5d), cs(wglu), cs(wout), cs(postw), cs(prew), cs(wr), cs(br)],
        out_specs=[pl.BlockSpec((tm, D_MODEL), row_map),
                   pl.BlockSpec((tm * TOKEN_TILE_ROWS, LANES), row_map),
                   pl.BlockSpec((tm, LANES), row_map)],
        compiler_params=_cparams(("arbitrary", "arbitrary")),
        name="merge_router",
    )(x2d, mod3, o_f, o_b, z, y_f, y_b, u, gates, onw, waup, s5d, wglu, wout, postw, prew, wr, br)


def _rank_kernel(route_ref, rank_ref, count_ref, carry_ref):
    i = pl.program_id(0)

    @pl.when(i == 0)
    def _():
        carry_ref[...] = jnp.zeros_like(carry_ref)

    route = route_ref[...]
    r = route.shape[0]
    lane = lax.broadcasted_iota(jnp.int32, route.shape, 1).astype(F32)
    oh0 = (lane == route[:, 0:1]).astype(F32)
    oh1 = (lane == route[:, 1:2]).astype(F32)
    both = oh0 + oh1
    row = lax.broadcasted_iota(jnp.int32, (r, r), 0)
    col = lax.broadcasted_iota(jnp.int32, (r, r), 1)
    before = _bdot((row > col).astype(F32), both) + carry_ref[...]
    rank0 = jnp.sum(oh0 * before, axis=-1, keepdims=True)
    rank1 = jnp.sum(oh1 * before, axis=-1, keepdims=True)
    lane_i = lax.broadcasted_iota(jnp.int32, route.shape, 1)
    rank_ref[...] = jnp.where(lane_i == 0, rank0, jnp.where(lane_i == 1, rank1, 0.0))
    carry_ref[...] = carry_ref[...] + jnp.sum(both, axis=0, keepdims=True)
    count_ref[...] = carry_ref[...]


def _expert_ranks(route):
    rows = route.shape[0]
    r = RANK_TILE
    return pl.pallas_call(
        _rank_kernel,
        out_shape=[jax.ShapeDtypeStruct((rows, LANES), F32), jax.ShapeDtypeStruct((1, LANES), F32)],
        grid=(rows // r,),
        in_specs=[pl.BlockSpec((r, LANES), lambda i: (i, 0))],
        out_specs=[pl.BlockSpec((r, LANES), lambda i: (i, 0)), pl.BlockSpec((1, LANES), lambda i: (0, 0))],
        scratch_shapes=[pltpu.VMEM((1, LANES), F32)],
        compiler_params=_cparams(("arbitrary",)),
        name="expert_ranks",
    )(route)


def _token_copy(src_ref, src_token, dst_ref, dst_token, sem):
    r = TOKEN_TILE_ROWS
    return pltpu.make_async_copy(src_ref.at[pl.ds(pl.multiple_of(src_token * r, r), r), :],
                                 dst_ref.at[pl.ds(dst_token * r, r), :], sem)


def _row_token_kernel(pos_ref, fill_ref, inv_ref, sem):
    i = pl.program_id(0)
    tm = pos_ref.shape[-1] // 2
    base = i * tm

    @pl.when(i == 0)
    def _():
        fill = pltpu.make_async_copy(fill_ref, inv_ref, sem)
        fill.start()
        fill.wait()

    def body(t, _):
        for s in range(2):
            p = pos_ref[0, 0, s * tm + t]
            inv_ref[p] = base + t
        return 0

    lax.fori_loop(0, tm, body, 0, unroll=16)


def _row_tokens(pos3, n_rows):
    n_tiles, _, two_tm = pos3.shape
    n_tok = n_tiles * two_tm // 2
    fill = jnp.arange(n_rows, dtype=jnp.int32) % n_tok
    return pl.pallas_call(
        _row_token_kernel,
        out_shape=jax.ShapeDtypeStruct((n_rows,), jnp.int32),
        grid=(n_tiles,),
        in_specs=[pl.BlockSpec((1, 1, two_tm), lambda i: (i, 0, 0), memory_space=pltpu.SMEM),
                  pl.BlockSpec(memory_space=pl.ANY)],
        out_specs=pl.BlockSpec(memory_space=pltpu.SMEM),
        scratch_shapes=[pltpu.SemaphoreType.DMA],
        compiler_params=_cparams(("arbitrary",)),
        name="row_tokens",
    )(pos3, fill)


def _gather_rows(src_ref, idx_ref, n, dst_ref, sem):
    for t in range(n):
        _token_copy(src_ref, idx_ref[0, 0, t], dst_ref, t, sem).start()


def _wait_rows(src_ref, dst_ref, sem):
    pltpu.make_async_copy(src_ref.at[pl.ds(0, dst_ref.shape[0]), :], dst_ref, sem).wait()


def _gather_ring(i, idx_refs, src_ref, n, buf_ref, sems):
    nb = GATHER_BUFFERS

    @pl.when(i == 0)
    def _():
        for k in range(nb - 1):
            _gather_rows(src_ref, idx_refs[k], n, buf_ref.at[k], sems.at[k])

    slot = i % nb
    _wait_rows(src_ref, buf_ref.at[slot], sems.at[slot])
    ahead = (i + nb - 1) % nb
    _gather_rows(src_ref, idx_refs[nb - 1], n, buf_ref.at[ahead], sems.at[ahead])
    return slot


def _gather_drain(i, n_tiles, src_ref, buf_ref, sems):
    @pl.when(i == n_tiles - 1)
    def _():
        for k in range(1, GATHER_BUFFERS):
            s = (i + k) % GATHER_BUFFERS
            _wait_rows(src_ref, buf_ref.at[s], sems.at[s])


def _ring_index_specs(block, n_tiles):
    def spec(k):
        return pl.BlockSpec(block, lambda i, *_: (jnp.minimum(i + k, n_tiles - 1), 0, 0), memory_space=pltpu.SMEM)

    return [spec(k) for k in range(GATHER_BUFFERS)]


def _expert_kernel(te_ref, tv_ref, inv0_ref, inv1_ref, inv2_ref, h2_ref, wg_ref, wu_ref, wd_ref, ys_ref,
                   xbuf_ref, sems, wgb_ref, wub_ref, wdb_ref, *, n_tiles):
    i = pl.program_id(0)
    te = ys_ref.shape[0] // TOKEN_TILE_ROWS

    @pl.when(jnp.logical_or(i == 0, te_ref[i] != te_ref[jnp.maximum(i - 1, 0)]))
    def _():
        wgb_ref[...] = wg_ref[0].astype(BF16)
        wub_ref[...] = wu_ref[0].astype(BF16)
        wdb_ref[...] = wd_ref[0].astype(BF16)

    slot = _gather_ring(i, (inv0_ref, inv1_ref, inv2_ref), h2_ref, te, xbuf_ref, sems)

    @pl.when(tv_ref[i] > 0)
    def _():
        x = _load_token_tiles(xbuf_ref.at[slot], 0, te).astype(BF16)
        hg = jnp.dot(x, wgb_ref[...], preferred_element_type=F32)
        hu = jnp.dot(x, wub_ref[...], preferred_element_type=F32)
        _store_token_tiles(ys_ref, _bdot(_silu(hg) * hu, wdb_ref[...]))

    @pl.when(tv_ref[i] == 0)
    def _():
        ys_ref[...] = jnp.zeros_like(ys_ref)

    _gather_drain(i, n_tiles, h2_ref, xbuf_ref, sems)


def _expert_mlp(tile_expert, tile_valid, inv3, h2, wg, wu, wd):
    n_tiles, _, te = inv3.shape
    rows = te * TOKEN_TILE_ROWS
    return pl.pallas_call(
        functools.partial(_expert_kernel, n_tiles=n_tiles),
        out_shape=jax.ShapeDtypeStruct((n_tiles * rows, LANES), F32),
        grid_spec=pltpu.PrefetchScalarGridSpec(
            num_scalar_prefetch=2,
            grid=(n_tiles,),
            in_specs=_ring_index_specs((1, 1, te), n_tiles) + [
                      pl.BlockSpec(memory_space=pl.ANY),
                      pl.BlockSpec((1, D_MODEL, EXPERT_FF), lambda i, e, v: (e[i], 0, 0)),
                      pl.BlockSpec((1, D_MODEL, EXPERT_FF), lambda i, e, v: (e[i], 0, 0)),
                      pl.BlockSpec((1, EXPERT_FF, D_MODEL), lambda i, e, v: (e[i], 0, 0))],
            out_specs=pl.BlockSpec((rows, LANES), lambda i, e, v: (i, 0)),
            scratch_shapes=[pltpu.VMEM((GATHER_BUFFERS, rows, LANES), F32),
                            pltpu.SemaphoreType.DMA((GATHER_BUFFERS,)),
                            pltpu.VMEM((D_MODEL, EXPERT_FF), BF16), pltpu.VMEM((D_MODEL, EXPERT_FF), BF16),
                            pltpu.VMEM((EXPERT_FF, D_MODEL), BF16)]),
        compiler_params=_cparams(("arbitrary",)),
        name="expert_mlp",
    )(tile_expert, tile_valid, inv3, inv3, inv3, h2, wg, wu, wd)


def _combine_kernel(pos0_ref, pos1_ref, pos2_ref, x1_ref, mod_ref, route_ref, postw_ref, ys_ref, out_ref, buf_ref,
                    sems, *, n_tiles):
    tm = x1_ref.shape[0]
    i = pl.program_id(0)
    slot = _gather_ring(i, (pos0_ref, pos1_ref, pos2_ref), ys_ref, 2 * tm, buf_ref, sems)
    route = route_ref[...]
    moe = (route[:, 2:3] * _load_token_tiles(buf_ref.at[slot], 0, tm)
           + route[:, 3:4] * _load_token_tiles(buf_ref.at[slot], tm, tm))
    out_ref[...] = x1_ref[...] + mod_ref[0][5:6, :] * _rms(moe, postw_ref[...])
    _gather_drain(i, n_tiles, ys_ref, buf_ref, sems)


def _combine(pos3, x1, mod3, route, postw, ys, *, bsz, seq):
    tm = TOK_TILE
    nt = seq // tm
    n_tiles = bsz * nt
    row_map = lambda i: (i, 0)
    return pl.pallas_call(
        functools.partial(_combine_kernel, n_tiles=n_tiles),
        out_shape=jax.ShapeDtypeStruct(x1.shape, F32),
        grid=(n_tiles,),
        in_specs=_ring_index_specs((1, 1, 2 * tm), n_tiles) + [
                  pl.BlockSpec((tm, D_MODEL), row_map),
                  pl.BlockSpec((1, N_MOD, D_MODEL), lambda i: (i // nt, 0, 0)),
                  pl.BlockSpec((tm, LANES), row_map),
                  pl.BlockSpec((1, D_MODEL), lambda i: (0, 0)),
                  pl.BlockSpec(memory_space=pl.ANY)],
        out_specs=pl.BlockSpec((tm, D_MODEL), row_map),
        scratch_shapes=[pltpu.VMEM((GATHER_BUFFERS, 2 * tm * TOKEN_TILE_ROWS, LANES), F32),
                        pltpu.SemaphoreType.DMA((GATHER_BUFFERS,))],
        compiler_params=_cparams(("arbitrary",)),
        name="combine",
    )(pos3, pos3, pos3, x1, mod3, route, postw, ys)


def _lane_pad(v, n=LANES):
    v = v.reshape(1, -1)
    return jnp.pad(v, ((0, 0), (0, n - v.shape[1])))


def kernel(x, c, ctx, c_ctx, w_mod, b_mod, pre_norm_w, post_norm_w, w_in, conv_w, a_log, dt_bias, o_norm_w,
           w_a_up, s5_lam_re, s5_lam_im, s5_log_step, s5_b_re, s5_b_im, s5_c_re, s5_c_im, s5_d, w_glu, w_out,
           w_router_group, b_router_group, w_router_expert, b_router_expert, w_gate_e, w_up_e, w_down_e):
    assert w_mod.shape[0] == 1, "single-layer block"
    bsz, seq, _ = x.shape
    ctx_len = ctx.shape[1]
    row = lambda v: v.reshape(1, -1)

    c16 = jnp.concatenate([c, c_ctx[None, :], jnp.zeros((2 * SUBLANES - bsz - 1, D_MODEL), F32)], axis=0)
    mod3 = _modulation(c16, w_mod[0], row(b_mod[0])).reshape(2 * SUBLANES, N_MOD, D_MODEL)

    wi = w_in[0]
    o_z, o_a, o_u, o_g = 3 * DN_WIDTH, 4 * DN_WIDTH, 4 * DN_WIDTH + 4 * DN_HEADS, 4 * DN_WIDTH + 4 * DN_HEADS + S5_WIDTH
    wts_in = (wi[:, :o_z].astype(BF16), wi[:, o_z:o_a].astype(BF16),
              jnp.pad(wi[:, o_a:o_u], ((0, 0), (0, LANES - 4 * DN_HEADS))).astype(BF16),
              wi[:, o_u:o_g].astype(BF16), wi[:, o_g:].astype(BF16))
    convw = jnp.pad(conv_w[0], ((0, SUBLANES - CONV_K), (0, 0)))
    alog = _lane_pad(a_log[0])
    dtb = _lane_pad(dt_bias[0])
    prew0, prew1 = row(pre_norm_w[0, 0]), row(pre_norm_w[0, 1])
    postw0, postw1 = row(post_norm_w[0, 0]), row(post_norm_w[0, 1])

    bar_re, bar_im, bb_re, bb_im = _s5_discretise(s5_lam_re[0], s5_lam_im[0], s5_log_step[0], s5_b_re[0], s5_b_im[0])
    s5w = _s5_weights(bar_re, bar_im, bb_re, bb_im, s5_c_re[0], s5_c_im[0], bsz)

    qkvn_c, gb_c, u_c = _input_projection(ctx.reshape(bsz * ctx_len, D_MODEL), mod3, bsz, 0, prew0, wts_in, convw,
                                          alog, dtb, bsz=bsz, seq=ctx_len, seg_len=ctx_len, full=False)
    s_zero = jnp.zeros((bsz, 2 * DN_HEADS, DN_HEAD_DIM, DN_HEAD_DIM), F32)
    _, _, s_ctx = _delta_rule(qkvn_c, gb_c, s_zero, bsz=bsz, seq=ctx_len)
    x_zero = jnp.zeros((2, bsz, S5_LANES), F32)
    (x_ctx,) = _s5_scan(u_c.reshape(bsz, ctx_len, S5_WIDTH), s5w, x_zero, emit_y=False)

    x2d = x.reshape(bsz * seq, D_MODEL)
    qkvn, gb, u, z, gates = _input_projection(x2d, mod3, 0, 1, prew0, wts_in, convw, alog, dtb,
                                                 bsz=bsz, seq=seq, seg_len=GRID_W, full=True)
    o_f, o_b, _ = _delta_rule(qkvn, gb, s_ctx, bsz=bsz, seq=seq)
    y_f, y_b, _ = _s5_scan(u.reshape(bsz, seq, S5_WIDTH), s5w, x_ctx, emit_y=True)
    rows2d = lambda a: a.reshape(bsz * seq, S5_WIDTH)

    w_router = jnp.pad(jnp.concatenate([w_router_group[0], w_router_expert[0]], axis=1),
                       ((0, 0), (0, LANES - N_EXPERT_GROUPS - N_EXPERTS))).astype(BF16)
    b_router = _lane_pad(jnp.concatenate([b_router_group[0], b_router_expert[0]]))
    wts_merge = (row(o_norm_w[0]), w_a_up[0].astype(BF16), row(s5_d[0]), w_glu[0].astype(BF16),
                 w_out[0].astype(BF16), postw0, prew1, w_router, b_router)
    x1, h2, route = _merge(x2d, mod3, o_f, o_b, z, rows2d(y_f), rows2d(y_b), u, gates, wts_merge, bsz=bsz, seq=seq)

    rank, count = _expert_ranks(route)
    n_tok = bsz * seq
    te = EXPERT_TILE
    counts = count[0, :N_EXPERTS].astype(jnp.int32)
    padded = ((counts + te - 1) // te) * te
    ends = jnp.cumsum(padded)
    starts = ends - padded
    ids = route[:, :2].astype(jnp.int32)
    one_hot = ids[:, :, None] == jnp.arange(N_EXPERTS, dtype=jnp.int32)
    pos = jnp.sum(jnp.where(one_hot, starts, 0), axis=-1) + rank[:, :2].astype(jnp.int32)
    pos3 = jnp.swapaxes(pos.reshape(n_tok // TOK_TILE, TOK_TILE, 2), 1, 2).reshape(n_tok // TOK_TILE, 1, 2 * TOK_TILE)
    n_sorted = 2 * n_tok + N_EXPERTS * te
    tile_start = jnp.arange(n_sorted // te, dtype=jnp.int32) * te
    tile_valid = (tile_start < ends[-1]).astype(jnp.int32)
    tile_expert = jnp.minimum(jnp.sum((ends[None, :] <= tile_start[:, None]).astype(jnp.int32), axis=1), N_EXPERTS - 1)

    inv3 = _row_tokens(pos3, n_sorted).reshape(n_sorted // te, 1, te)
    ys = _expert_mlp(tile_expert, tile_valid, inv3, h2, w_gate_e[0], w_up_e[0], w_down_e[0])
    out = _combine(pos3, x1, mod3, route, postw1, ys, bsz=bsz, seq=seq)
    return out.reshape(bsz, seq, D_MODEL)
```

```python
import functools

import jax
import jax.numpy as jnp
from jax import lax
from jax.experimental import pallas as pl
from jax.experimental.pallas import tpu as pltpu

F32 = jnp.float32
BF16 = jnp.bfloat16
HIGHEST = lax.Precision.HIGHEST

D_MODEL = 1024
GRID_W = 64
DN_HEADS = 4
DN_HEAD_DIM = 128
DN_WIDTH = DN_HEADS * DN_HEAD_DIM
DN_CHUNK = 64
CONV_K = 5
S5_WIDTH = D_MODEL - DN_WIDTH
S5_GROUP = 16
S5_GROUPS = S5_WIDTH // S5_GROUP
S5_STATE = 64
S5_HALF_GROUPS = S5_GROUPS // 2
S5_HALF_STATE = S5_HALF_GROUPS * S5_STATE
S5_LANES = 2 * 2 * S5_HALF_STATE
N_EXPERT_GROUPS = 4
EXPERTS_PER_GROUP = 8
N_EXPERTS = N_EXPERT_GROUPS * EXPERTS_PER_GROUP
EXPERT_FF = 512
NORM_EPS = 1e-6
L2_EPS = 1e-6
N_MOD = 6
LANES = 128
SUBLANES = 8
TOKEN_TILE_ROWS = D_MODEL // LANES
VMEM_LIMIT = 56 * 1024 * 1024

TOK_TILE = 512
PROJ_TILE = 512
S5_TILE = 64
DN_CHUNKS_PER_STEP = 4
EXPERT_TILE = 512
RANK_TILE = 512
GATHER_BUFFERS = 3


def _cparams(sem):
    return pltpu.CompilerParams(dimension_semantics=sem, vmem_limit_bytes=VMEM_LIMIT)


def _bdot(a, b):
    return jnp.dot(a.astype(BF16), b.astype(BF16), preferred_element_type=F32)


def _bdot_nt(a, b):
    return lax.dot_general(a.astype(BF16), b.astype(BF16), (((1,), (1,)), ((), ())),
                           preferred_element_type=F32)


def _bdot_tn(a, b):
    return lax.dot_general(a.astype(BF16), b.astype(BF16), (((0,), (0,)), ((), ())),
                           preferred_element_type=F32)


def _silu(x):
    return x * jax.nn.sigmoid(x)


def _rms(x, w):
    return x * lax.rsqrt(jnp.mean(x * x, axis=-1, keepdims=True) + NORM_EPS) * w


def _store_token_tiles(ref, val):
    n = val.shape[0]
    for j in range(TOKEN_TILE_ROWS):
        ref[pl.ds(j, n, stride=TOKEN_TILE_ROWS), :] = val[:, j * LANES:(j + 1) * LANES]


def _load_token_tiles(ref, first_token, n):
    return jnp.concatenate([ref[pl.ds(first_token * TOKEN_TILE_ROWS + j, n, stride=TOKEN_TILE_ROWS), :]
                            for j in range(TOKEN_TILE_ROWS)], axis=1)


def _mod_kernel(c_ref, w_ref, b_ref, o_ref):
    o_ref[...] = jnp.dot(_silu(c_ref[...]), w_ref[...], precision=HIGHEST,
                         preferred_element_type=F32) + b_ref[...]


def _modulation(c16, w_mod, b_mod):
    n = w_mod.shape[1]
    return pl.pallas_call(
        _mod_kernel,
        out_shape=jax.ShapeDtypeStruct((c16.shape[0], n), F32),
        grid=(n // D_MODEL,),
        in_specs=[pl.BlockSpec(c16.shape, lambda j: (0, 0)),
                  pl.BlockSpec((D_MODEL, D_MODEL), lambda j: (0, j)),
                  pl.BlockSpec((1, D_MODEL), lambda j: (0, j))],
        out_specs=pl.BlockSpec((c16.shape[0], D_MODEL), lambda j: (0, j)),
        compiler_params=_cparams(("arbitrary",)),
        name="modulation",
    )(c16, w_mod, b_mod)


def _inproj_kernel(x_ref, mod_ref, prew_ref, wqkv_ref, wz_ref, wab_ref, wu_ref, wg_ref, convw_ref,
                   alog_ref, dtb_ref, *out_refs, seg_len, full):
    if full:
        qkvn_ref, gb_ref, u_ref, z_ref, gates_ref = out_refs
    else:
        qkvn_ref, gb_ref, u_ref = out_refs
    tm = x_ref.shape[0]
    mod = mod_ref[0]
    h = _rms(x_ref[...], prew_ref[...]) * (1.0 + mod[1:2, :]) + mod[0:1, :]
    hb = h.astype(BF16)

    slab = 2 * DN_HEAD_DIM
    others = [(u_ref, wu_ref, j) for j in range(S5_WIDTH // slab)]
    if full:
        others += [(z_ref, wz_ref, j) for j in range(DN_WIDTH // slab)]
        others += [(gates_ref, wg_ref, j) for j in range(2 * D_MODEL // slab)]
    n_qkv = 3 * DN_WIDTH // slab
    per = -(-len(others) // n_qkv)
    pos = lax.broadcasted_iota(jnp.int32, (tm, 1), 0) % seg_len
    for j in range(n_qkv):
        cols = slice(j * slab, (j + 1) * slab)
        qkv = jnp.dot(hb, wqkv_ref[:, cols], preferred_element_type=F32)
        for o_ref, w_ref, jj in others[j * per:(j + 1) * per]:
            oc = slice(jj * slab, (jj + 1) * slab)
            o_ref[:, oc] = jnp.dot(hb, w_ref[:, oc], preferred_element_type=F32)
        acc = qkv * convw_ref[CONV_K // 2:CONV_K // 2 + 1, cols]
        for k in range(CONV_K):
            s = k - CONV_K // 2
            if s == 0:
                continue
            shifted = pltpu.roll(qkv, (-s) % tm, 0)
            ok = (pos + s >= 0) & (pos + s < seg_len)
            acc = acc + jnp.where(ok, shifted, 0.0) * convw_ref[k:k + 1, cols]
        act = _silu(acc)
        for h in range(2):
            a = act[:, h * DN_HEAD_DIM:(h + 1) * DN_HEAD_DIM]
            if j < 2 * DN_WIDTH // slab:
                a = a * lax.rsqrt(jnp.sum(a * a, axis=-1, keepdims=True) + L2_EPS)
            qkvn_ref[:, j * slab + h * DN_HEAD_DIM:j * slab + (h + 1) * DN_HEAD_DIM] = a

    ab = jnp.dot(hb, wab_ref[...], preferred_element_type=F32)
    xa = ab + dtb_ref[...]
    softplus = jnp.maximum(xa, 0.0) + jnp.log1p(jnp.exp(-jnp.abs(xa)))
    g = -jnp.exp(alog_ref[...]) * softplus
    lane = lax.broadcasted_iota(jnp.int32, ab.shape, 1)
    gb_ref[...] = jnp.where(lane < 2 * DN_HEADS, g, jax.nn.sigmoid(ab))


def _input_projection(x2d, mod3, mod_row0, mod_row_stride, prew, wts, convw, alog, dtb, *, bsz, seq, seg_len, full):
    wqkv, wz, wab, wu, wg = wts
    tm = min(PROJ_TILE, seq)
    nt = seq // tm
    rows = bsz * seq
    row_map = lambda b, i: (b * nt + i, 0)
    const = lambda b, i: (0, 0)
    resident = lambda a: pl.BlockSpec(a.shape, const, pipeline_mode=pl.Buffered(1))
    out_shape = [jax.ShapeDtypeStruct((rows, 3 * DN_WIDTH), F32),
                 jax.ShapeDtypeStruct((rows, LANES), F32),
                 jax.ShapeDtypeStruct((rows, S5_WIDTH), F32)]
    out_specs = [pl.BlockSpec((tm, 3 * DN_WIDTH), row_map),
                 pl.BlockSpec((tm, LANES), row_map),
                 pl.BlockSpec((tm, S5_WIDTH), row_map)]
    if full:
        out_shape += [jax.ShapeDtypeStruct((rows, DN_WIDTH), F32),
                      jax.ShapeDtypeStruct((rows, 2 * D_MODEL), F32)]
        out_specs += [pl.BlockSpec((tm, DN_WIDTH), row_map),
                      pl.BlockSpec((tm, 2 * D_MODEL), row_map)]
    return pl.pallas_call(
        functools.partial(_inproj_kernel, seg_len=seg_len, full=full),
        out_shape=out_shape,
        grid=(bsz, nt),
        in_specs=[pl.BlockSpec((tm, D_MODEL), row_map),
                  pl.BlockSpec((1, N_MOD, D_MODEL), lambda b, i: (mod_row0 + mod_row_stride * b, 0, 0)),
                  pl.BlockSpec((1, D_MODEL), const),
                  resident(wqkv), resident(wz), resident(wab), resident(wu), resident(wg),
                  pl.BlockSpec(convw.shape, const),
                  pl.BlockSpec((1, LANES), const),
                  pl.BlockSpec((1, LANES), const)],
        out_specs=out_specs,
        compiler_params=_cparams(("arbitrary", "arbitrary")),
        name="input_projection_full" if full else "input_projection_ctx",
    )(x2d, mod3, prew, wqkv, wz, wab, wu, wg, convw, alog, dtb)


def _chunk_cumsum(g, reverse):
    n = g.shape[0]
    row = lax.broadcasted_iota(jnp.int32, (n, 1), 0)
    s = 1
    while s < n:
        if reverse:
            g = g + jnp.where(row < n - s, pltpu.roll(g, n - s, 0), 0.0)
        else:
            g = g + jnp.where(row >= s, pltpu.roll(g, s, 0), 0.0)
        s *= 2
    return g


def _delta_kernel(qf_ref, kf_ref, vf_ref, gbf_ref, qb_ref, kb_ref, vb_ref, gbb_ref, s0_ref,
                  of_ref, ob_ref, sout_ref, state_ref, *, n_steps, chunks):
    c = DN_CHUNK
    i = pl.program_id(1)

    @pl.when(i == 0)
    def _():
        state_ref[...] = s0_ref[0]

    row = lax.broadcasted_iota(jnp.int32, (c, c), 0)
    col = lax.broadcasted_iota(jnp.int32, (c, c), 1)
    eye = (row == col).astype(F32)
    scale = DN_HEAD_DIM ** -0.5
    dirs = ((qf_ref, kf_ref, vf_ref, gbf_ref, of_ref), (qb_ref, kb_ref, vb_ref, gbb_ref, ob_ref))
    chains = {}
    for d, (q_ref, k_ref, v_ref, gb_ref, o_ref) in enumerate(dirs):
        incl = (row >= col) if d == 0 else (row <= col)
        strict = (row > col) if d == 0 else (row < col)
        last = c - 1 if d == 0 else 0
        for sub in range(chunks):
            rs = slice(sub * c, (sub + 1) * c)
            gb = gb_ref[rs, :]
            gcum = _chunk_cumsum(gb, reverse=d == 1)
            gcum_t = gcum.T
            for h in range(DN_HEADS):
                lane = d * DN_HEADS + h
                hs = slice(h * DN_HEAD_DIM, (h + 1) * DN_HEAD_DIM)
                g_col = gcum[:, lane:lane + 1]
                g_row = gcum_t[lane:lane + 1, :]
                g_last = gcum[last:last + 1, lane:lane + 1]
                beta = gb[:, 2 * DN_HEADS + lane:2 * DN_HEADS + lane + 1]
                dec = jnp.where(incl, jnp.exp(jnp.where(incl, g_col - g_row, 0.0)), 0.0)
                q = q_ref[rs, hs] * scale
                k = k_ref[rs, hs]
                v = v_ref[rs, hs]
                k_beta = k * beta
                e_g = jnp.exp(g_col)
                chains[d, sub, h] = dict(
                    lane=lane, rs=rs, hs=hs, o_ref=o_ref, strict=strict, dec=dec, q=q, k=k, k_beta=k_beta, e_g=e_g,
                    rhs=jnp.concatenate([v * beta, k_beta * e_g], axis=1),
                    k_tail=k * jnp.exp(g_last - g_col), e_last=jnp.exp(g_last))
    every = list(chains.values())
    for ch in every:
        kq = _bdot_nt(jnp.concatenate([ch['k_beta'], ch['q']], axis=0), ch['k'])
        ch['npow'] = -jnp.where(ch['strict'], kq[:c] * ch['dec'], 0.0)
        ch['attn'] = kq[c:] * ch['dec']
        ch['t_inv'] = eye + ch['npow']
    for _ in range(5):
        for ch in every:
            ch['npow'] = _bdot(ch['npow'], ch['npow'])
        for ch in every:
            ch['t_inv'] = ch['t_inv'] + _bdot(ch['t_inv'], ch['npow'])
    for ch in every:
        ch['sol'] = _bdot(ch['t_inv'], ch['rhs'])
        ch['q_dec'] = ch['q'] * ch['e_g']

    states = [state_ref[lane] for lane in range(2 * DN_HEADS)]
    for step in range(chunks):
        group = [chains[d, step if d == 0 else chunks - 1 - step, h] for d in range(2) for h in range(DN_HEADS)]
        for ch in group:
            ch['wq'] = _bdot(jnp.concatenate([ch['sol'][:, DN_HEAD_DIM:], ch['q_dec']], axis=0), states[ch['lane']])
        for ch in group:
            ch['v_new'] = ch['sol'][:, :DN_HEAD_DIM] - ch['wq'][:c]
            ch['o_ref'][ch['rs'], ch['hs']] = ch['wq'][c:] + _bdot(ch['attn'], ch['v_new'])
        for ch in group:
            states[ch['lane']] = states[ch['lane']] * ch['e_last'] + _bdot_tn(ch['k_tail'], ch['v_new'])
    for lane in range(2 * DN_HEADS):
        state_ref[lane] = states[lane]

    @pl.when(i == n_steps - 1)
    def _():
        sout_ref[0] = state_ref[...]


def _delta_rule(qkvn, gb, s0, *, bsz, seq):
    chunks = min(DN_CHUNKS_PER_STEP, seq // DN_CHUNK)
    rows = DN_CHUNK * chunks
    nb = seq // rows
    fwd = lambda j: (lambda b, i: (b * nb + i, j))
    bwd = lambda j: (lambda b, i: (b * nb + nb - 1 - i, j))
    blk = lambda m: pl.BlockSpec((rows, DN_WIDTH), m)
    gblk = lambda m: pl.BlockSpec((rows, LANES), m)
    st = pl.BlockSpec((1, 2 * DN_HEADS, DN_HEAD_DIM, DN_HEAD_DIM), lambda b, i: (b, 0, 0, 0))
    return pl.pallas_call(
        functools.partial(_delta_kernel, n_steps=nb, chunks=chunks),
        out_shape=[jax.ShapeDtypeStruct((bsz * seq, DN_WIDTH), F32),
                   jax.ShapeDtypeStruct((bsz * seq, DN_WIDTH), F32),
                   jax.ShapeDtypeStruct(s0.shape, F32)],
        grid=(bsz, nb),
        in_specs=[blk(fwd(0)), blk(fwd(1)), blk(fwd(2)), gblk(fwd(0)),
                  blk(bwd(0)), blk(bwd(1)), blk(bwd(2)), gblk(bwd(0)), st],
        out_specs=[blk(fwd(0)), blk(bwd(0)), st],
        scratch_shapes=[pltpu.VMEM((2 * DN_HEADS, DN_HEAD_DIM, DN_HEAD_DIM), F32)],
        compiler_params=_cparams(("arbitrary", "arbitrary")),
        name="delta_rule",
    )(qkvn, qkvn, qkvn, gb, qkvn, qkvn, qkvn, gb, s0)


def _s5_disc_kernel(lre_ref, lim_ref, ls_ref, bre_ref, bim_ref, are_ref, aim_ref, bbre_ref, bbim_ref):
    lam_re = lre_ref[...]
    lam_im = lim_ref[...]
    dt = jnp.exp(ls_ref[...])
    mag = jnp.exp(lam_re * dt)
    ang = lam_im * dt
    bar_re = mag * jnp.cos(ang)
    bar_im = mag * jnp.sin(ang)
    den = lam_re * lam_re + lam_im * lam_im
    coef_re = ((bar_re - 1.0) * lam_re + bar_im * lam_im) / den
    coef_im = (bar_im * lam_re - (bar_re - 1.0) * lam_im) / den
    are_ref[...] = bar_re
    aim_ref[...] = bar_im
    bbre_ref[...] = coef_re * bre_ref[...] - coef_im * bim_ref[...]
    bbim_ref[...] = coef_re * bim_ref[...] + coef_im * bre_ref[...]


def _s5_discretise(lam_re, lam_im, log_step, b_re, b_im):
    rep = lambda t: jnp.repeat(t.reshape(2 * S5_GROUPS, -1), S5_GROUP, axis=0)
    rows = 2 * S5_GROUPS * S5_GROUP
    tr = lambda t: jnp.swapaxes(t, -1, -2).reshape(rows, S5_STATE)
    outs = pl.pallas_call(
        _s5_disc_kernel,
        out_shape=[jax.ShapeDtypeStruct((rows, S5_STATE), F32)] * 4,
        name="s5_discretise",
    )(rep(lam_re), rep(lam_im), rep(log_step[..., None]), tr(b_re), tr(b_im))
    shp = (2, S5_GROUPS, S5_GROUP, S5_STATE)
    bar_re, bar_im, bb_re, bb_im = (o.reshape(shp) for o in outs)
    return bar_re[:, :, 0], bar_im[:, :, 0], bb_re, bb_im


def _s5_kernel(uf_ref, ub_ref, wb_ref, wcre_ref, wcim_ref, are_ref, aim_ref, x0_ref, *rest, tt, n_tiles, emit_y):
    if emit_y:
        yf_ref, yb_ref, xout_ref, buf_ref, xst_ref, us_ref, ys_ref = rest
    else:
        xout_ref, buf_ref, xst_ref, us_ref = rest
        yf_ref = yb_ref = ys_ref = None
    i = pl.program_id(0)
    bsz = uf_ref.shape[0]
    hs = S5_HALF_STATE
    n_lane_tiles = S5_WIDTH // LANES
    blk = 512
    u_refs = (uf_ref, ub_ref)
    y_refs = (yf_ref, yb_ref)

    @pl.when(i == 0)
    def _():
        xst_ref[...] = x0_ref[...]

    def input_matmul(hh):
        for d in range(2):
            u_h = jnp.concatenate([us_ref[d, 2 * hh], us_ref[d, 2 * hh + 1]], axis=1)
            buf_ref[d, :, hh * 2 * hs:(hh + 1) * 2 * hs] = _bdot(u_h, wb_ref[d, hh])

    def scan(hh):
        for p in range(hs // blk):
            lr = hh * 2 * hs + p * blk
            li = lr + hs
            a = [(are_ref[d, :, lr:lr + blk], aim_ref[d, :, lr:lr + blk]) for d in range(2)]
            x = [(xst_ref[d, :, lr:lr + blk], xst_ref[d, :, li:li + blk]) for d in range(2)]
            for t in range(tt):
                for d in range(2):
                    r0 = (t if d == 0 else tt - 1 - t) * SUBLANES
                    a_re, a_im = a[d]
                    x_re, x_im = x[d]
                    n_re = a_re * x_re - a_im * x_im + buf_ref[d, r0:r0 + SUBLANES, lr:lr + blk]
                    n_im = a_re * x_im + a_im * x_re + buf_ref[d, r0:r0 + SUBLANES, li:li + blk]
                    buf_ref[d, r0:r0 + SUBLANES, lr:lr + blk] = n_re
                    buf_ref[d, r0:r0 + SUBLANES, li:li + blk] = n_im
                    x[d] = (n_re, n_im)
            for d in range(2):
                xst_ref[d, :, lr:lr + blk] = x[d][0]
                xst_ref[d, :, li:li + blk] = x[d][1]

    def output_matmul(hh):
        for d in range(2):
            x_re = buf_ref[d, :, hh * 2 * hs:hh * 2 * hs + hs]
            x_im = buf_ref[d, :, hh * 2 * hs + hs:(hh + 1) * 2 * hs]
            y_h = _bdot(x_re, wcre_ref[d, hh]) - _bdot(x_im, wcim_ref[d, hh])
            ys_ref[d, 2 * hh] = y_h[:, :LANES]
            ys_ref[d, 2 * hh + 1] = y_h[:, LANES:]

    for d in range(2):
        for b in range(bsz):
            for j in range(n_lane_tiles):
                us_ref[d, j, pl.ds(b, tt, stride=bsz), :] = u_refs[d][b, :, j * LANES:(j + 1) * LANES]
    input_matmul(0)
    input_matmul(1)
    scan(0)
    if emit_y:
        output_matmul(0)
    scan(1)
    if emit_y:
        output_matmul(1)
        for d in range(2):
            for b in range(bsz):
                for j in range(n_lane_tiles):
                    y_refs[d][b, :, j * LANES:(j + 1) * LANES] = ys_ref[d, j, pl.ds(b, tt, stride=bsz), :]

    @pl.when(i == n_tiles - 1)
    def _():
        xout_ref[...] = xst_ref[...]


def _s5_scan(u3, s5w, x0, *, emit_y):
    wb, wc_re, wc_im, a_re, a_im = s5w
    bsz, seq, _ = u3.shape
    assert bsz == SUBLANES, "one time step of all batches must fill one sublane group"
    tt = S5_TILE
    nt = seq // tt
    rows = tt * bsz
    full = lambda a: pl.BlockSpec(a.shape, lambda i: (0,) * a.ndim)
    ublk = lambda m: pl.BlockSpec((bsz, tt, S5_WIDTH), m)
    fwd = lambda i: (0, i, 0)
    bwd = lambda i: (0, nt - 1 - i, 0)
    out_shape = [jax.ShapeDtypeStruct(x0.shape, F32)]
    out_specs = [full(x0)]
    scratch = [pltpu.VMEM((2, rows, S5_LANES), F32), pltpu.VMEM(x0.shape, F32),
               pltpu.VMEM((2, S5_WIDTH // LANES, rows, LANES), F32)]
    if emit_y:
        out_shape = [jax.ShapeDtypeStruct(u3.shape, F32)] * 2 + out_shape
        out_specs = [ublk(fwd), ublk(bwd)] + out_specs
        scratch.append(pltpu.VMEM((2, S5_WIDTH // LANES, rows, LANES), F32))
    return pl.pallas_call(
        functools.partial(_s5_kernel, tt=tt, n_tiles=nt, emit_y=emit_y),
        out_shape=out_shape,
        grid=(nt,),
        in_specs=[ublk(fwd), ublk(bwd), full(wb), full(wc_re), full(wc_im), full(a_re), full(a_im), full(x0)],
        out_specs=out_specs,
        scratch_shapes=scratch,
        compiler_params=_cparams(("arbitrary",)),
        name="s5_scan_y" if emit_y else "s5_scan_state",
    )(u3, u3, wb, wc_re, wc_im, a_re, a_im, x0)


def _s5_weights(bar_re, bar_im, bb_re, bb_im, c_re, c_im, bsz):
    hg = S5_HALF_GROUPS
    eye = jnp.eye(hg, dtype=F32)

    def in_mat(bb):
        t = bb.reshape(2, 2, hg, S5_GROUP, S5_STATE)
        t = jnp.einsum('dhicp,ij->dhicjp', t, eye)
        return t.reshape(2, 2, hg * S5_GROUP, hg * S5_STATE)

    def out_mat(cc):
        t = cc.reshape(2, 2, hg, S5_GROUP, S5_STATE)
        t = jnp.einsum('dhicp,ij->dhipjc', t, eye)
        return t.reshape(2, 2, hg * S5_STATE, hg * S5_GROUP)

    wb = jnp.concatenate([in_mat(bb_re), in_mat(bb_im)], axis=-1).astype(BF16)
    wc_re = out_mat(c_re).astype(BF16)
    wc_im = out_mat(c_im).astype(BF16)

    def lanes(a):
        t = a.reshape(2, 2, 1, S5_HALF_STATE)
        t = jnp.broadcast_to(t, (2, 2, 2, S5_HALF_STATE)).reshape(2, 1, S5_LANES)
        return jnp.broadcast_to(t, (2, bsz, S5_LANES))

    return wb, wc_re, wc_im, lanes(bar_re), lanes(bar_im)


def _merge_kernel(x_ref, mod_ref, of_ref, ob_ref, z_ref, yf_ref, yb_ref, u_ref, gates_ref, onw_ref, waup_ref,
                  s5d_ref, wglu_ref, wout_ref, postw_ref, prew_ref, wr_ref, br_ref,
                  x1_ref, h2_ref, route_ref):
    mod = mod_ref[0]
    o_a = of_ref[...] + ob_ref[...]
    z = z_ref[...]
    parts = []
    for h in range(DN_HEADS):
        hs = slice(h * DN_HEAD_DIM, (h + 1) * DN_HEAD_DIM)
        parts.append(_rms(o_a[:, hs], onw_ref[...]) * _silu(z[:, hs]))
    y_a = _bdot(jnp.concatenate(parts, axis=1), waup_ref[...])
    y5 = yf_ref[...] + yb_ref[...] + s5d_ref[...] * u_ref[...]
    glu = _bdot(jax.nn.gelu(y5, approximate=True), wglu_ref[...])
    y_b = glu[:, :D_MODEL] * jax.nn.sigmoid(glu[:, D_MODEL:])
    gates = gates_ref[...]
    y = jax.nn.sigmoid(gates[:, :D_MODEL]) * y_a + jax.nn.sigmoid(gates[:, D_MODEL:]) * y_b
    mixed = _bdot(y, wout_ref[...])
    x1 = x_ref[...] + mod[2:3, :] * _rms(mixed, postw_ref[...])
    x1_ref[...] = x1
    h2 = _rms(x1, prew_ref[...]) * (1.0 + mod[4:5, :]) + mod[3:4, :]
    _store_token_tiles(h2_ref, h2)

    logits = _bdot(h2, wr_ref[...]) + br_ref[...]
    lane = lax.broadcasted_iota(jnp.int32, logits.shape, 1)
    neg = -jnp.inf
    big = jnp.int32(LANES)
    first = lambda hit: jnp.min(jnp.where(hit, lane, big), axis=-1, keepdims=True)
    g_logit = jnp.where(lane < N_EXPERT_GROUPS, logits, neg)
    g_max = jnp.max(g_logit, axis=-1, keepdims=True)
    g_sel = first(g_logit == g_max)
    g_w = 1.0 / jnp.sum(jnp.exp(g_logit - g_max), axis=-1, keepdims=True)
    e_idx = lane - N_EXPERT_GROUPS
    in_group = (e_idx >= 0) & (e_idx < N_EXPERTS) & ((e_idx // EXPERTS_PER_GROUP) == g_sel)
    e_logit = jnp.where(in_group, logits, neg)
    m1 = jnp.max(e_logit, axis=-1, keepdims=True)
    i1 = first(e_logit == m1)
    e_logit2 = jnp.where(lane == i1, neg, e_logit)
    m2 = jnp.max(e_logit2, axis=-1, keepdims=True)
    i2 = first(e_logit2 == m2)
    e2 = jnp.exp(m2 - m1)
    w1 = g_w / (1.0 + e2)
    w2 = g_w * e2 / (1.0 + e2)
    id1 = (i1 - N_EXPERT_GROUPS).astype(F32)
    id2 = (i2 - N_EXPERT_GROUPS).astype(F32)
    route_ref[...] = jnp.where(lane == 0, id1, jnp.where(lane == 1, id2, jnp.where(lane == 2, w1,
                               jnp.where(lane == 3, w2, 0.0))))


def _merge(x2d, mod3, o_f, o_b, z, y_f, y_b, u, gates, wts, *, bsz, seq):
    onw, waup, s5d, wglu, wout, postw, prew, wr, br = wts
    tm = PROJ_TILE
    nt = seq // tm
    rows = bsz * seq
    row_map = lambda b, i: (b * nt + i, 0)
    const = lambda b, i: (0, 0)
    cs = lambda a: pl.BlockSpec(a.shape, const, pipeline_mode=pl.Buffered(1))
    return pl.pallas_call(
        _merge_kernel,
        out_shape=[jax.ShapeDtypeStruct((rows, D_MODEL), F32),
                   jax.ShapeDtypeStruct((rows * TOKEN_TILE_ROWS, LANES), F32),
                   jax.ShapeDtypeStruct((rows, LANES), F32)],
        grid=(bsz, nt),
        in_specs=[pl.BlockSpec((tm, D_MODEL), row_map),
                  pl.BlockSpec((1, N_MOD, D_MODEL), lambda b, i: (b, 0, 0)),
                  pl.BlockSpec((tm, DN_WIDTH), row_map),
                  pl.BlockSpec((tm, DN_WIDTH), row_map),
                  pl.BlockSpec((tm, DN_WIDTH), row_map),
                  pl.BlockSpec((tm, S5_WIDTH), row_map),
                  pl.BlockSpec((tm, S5_WIDTH), row_map),
                  pl.BlockSpec((tm, S5_WIDTH), row_map),
                  pl.BlockSpec((tm, 2 * D_MODEL), row_map),
                  cs(onw), cs(waup), cs(s5d), cs(wglu), cs(wout), cs(postw), cs(prew), cs(wr), cs(br)],
        out_specs=[pl.BlockSpec((tm, D_MODEL), row_map),
                   pl.BlockSpec((tm * TOKEN_TILE_ROWS, LANES), row_map),
                   pl.BlockSpec((tm, LANES), row_map)],
        compiler_params=_cparams(("arbitrary", "arbitrary")),
        name="merge_router",
    )(x2d, mod3, o_f, o_b, z, y_f, y_b, u, gates, onw, waup, s5d, wglu, wout, postw, prew, wr, br)


def _rank_kernel(route_ref, rank_ref, count_ref, carry_ref):
    i = pl.program_id(0)

    @pl.when(i == 0)
    def _():
        carry_ref[...] = jnp.zeros_like(carry_ref)

    route = route_ref[...]
    r = route.shape[0]
    lane = lax.broadcasted_iota(jnp.int32, route.shape, 1).astype(F32)
    oh0 = (lane == route[:, 0:1]).astype(F32)
    oh1 = (lane == route[:, 1:2]).astype(F32)
    both = oh0 + oh1
    row = lax.broadcasted_iota(jnp.int32, (r, r), 0)
    col = lax.broadcasted_iota(jnp.int32, (r, r), 1)
    before = _bdot((row > col).astype(F32), both) + carry_ref[...]
    rank0 = jnp.sum(oh0 * before, axis=-1, keepdims=True)
    rank1 = jnp.sum(oh1 * before, axis=-1, keepdims=True)
    lane_i = lax.broadcasted_iota(jnp.int32, route.shape, 1)
    rank_ref[...] = jnp.where(lane_i == 0, rank0, jnp.where(lane_i == 1, rank1, 0.0))
    carry_ref[...] = carry_ref[...] + jnp.sum(both, axis=0, keepdims=True)
    count_ref[...] = carry_ref[...]


def _expert_ranks(route):
    rows = route.shape[0]
    r = RANK_TILE
    return pl.pallas_call(
        _rank_kernel,
        out_shape=[jax.ShapeDtypeStruct((rows, LANES), F32), jax.ShapeDtypeStruct((1, LANES), F32)],
        grid=(rows // r,),
        in_specs=[pl.BlockSpec((r, LANES), lambda i: (i, 0))],
        out_specs=[pl.BlockSpec((r, LANES), lambda i: (i, 0)), pl.BlockSpec((1, LANES), lambda i: (0, 0))],
        scratch_shapes=[pltpu.VMEM((1, LANES), F32)],
        compiler_params=_cparams(("arbitrary",)),
        name="expert_ranks",
    )(route)


def _token_copy(src_ref, src_token, dst_ref, dst_token, sem):
    r = TOKEN_TILE_ROWS
    return pltpu.make_async_copy(src_ref.at[pl.ds(pl.multiple_of(src_token * r, r), r), :],
                                 dst_ref.at[pl.ds(dst_token * r, r), :], sem)


def _row_token_kernel(pos_ref, fill_ref, inv_ref, sem):
    i = pl.program_id(0)
    tm = pos_ref.shape[-1] // 2
    base = i * tm

    @pl.when(i == 0)
    def _():
        fill = pltpu.make_async_copy(fill_ref, inv_ref, sem)
        fill.start()
        fill.wait()

    def body(t, _):
        for s in range(2):
            p = pos_ref[0, 0, s * tm + t]
            inv_ref[p] = base + t
        return 0

    lax.fori_loop(0, tm, body, 0, unroll=16)


def _row_tokens(pos3, n_rows):
    n_tiles, _, two_tm = pos3.shape
    n_tok = n_tiles * two_tm // 2
    fill = jnp.arange(n_rows, dtype=jnp.int32) % n_tok
    return pl.pallas_call(
        _row_token_kernel,
        out_shape=jax.ShapeDtypeStruct((n_rows,), jnp.int32),
        grid=(n_tiles,),
        in_specs=[pl.BlockSpec((1, 1, two_tm), lambda i: (i, 0, 0), memory_space=pltpu.SMEM),
                  pl.BlockSpec(memory_space=pl.ANY)],
        out_specs=pl.BlockSpec(memory_space=pltpu.SMEM),
        scratch_shapes=[pltpu.SemaphoreType.DMA],
        compiler_params=_cparams(("arbitrary",)),
        name="row_tokens",
    )(pos3, fill)


def _gather_rows(src_ref, idx_ref, n, dst_ref, sem):
    for t in range(n):
        _token_copy(src_ref, idx_ref[0, 0, t], dst_ref, t, sem).start(priority=t % 2)


def _wait_rows(src_ref, dst_ref, sem):
    pltpu.make_async_copy(src_ref.at[pl.ds(0, dst_ref.shape[0]), :], dst_ref, sem).wait()


def _gather_ring(i, idx_refs, src_ref, n, buf_ref, sems):
    nb = GATHER_BUFFERS

    @pl.when(i == 0)
    def _():
        for k in range(nb - 1):
            _gather_rows(src_ref, idx_refs[k], n, buf_ref.at[k], sems.at[k])

    slot = i % nb
    _wait_rows(src_ref, buf_ref.at[slot], sems.at[slot])
    ahead = (i + nb - 1) % nb
    _gather_rows(src_ref, idx_refs[nb - 1], n, buf_ref.at[ahead], sems.at[ahead])
    return slot


def _gather_drain(i, n_tiles, src_ref, buf_ref, sems):
    @pl.when(i == n_tiles - 1)
    def _():
        for k in range(1, GATHER_BUFFERS):
            s = (i + k) % GATHER_BUFFERS
            _wait_rows(src_ref, buf_ref.at[s], sems.at[s])


def _ring_index_specs(block, n_tiles):
    def spec(k):
        return pl.BlockSpec(block, lambda i, *_: (jnp.minimum(i + k, n_tiles - 1), 0, 0), memory_space=pltpu.SMEM)

    return [spec(k) for k in range(GATHER_BUFFERS)]


def _expert_kernel(te_ref, tv_ref, inv0_ref, inv1_ref, inv2_ref, h2_ref, wg_ref, wu_ref, wd_ref, ys_ref,
                   xbuf_ref, sems, wgb_ref, wub_ref, wdb_ref, *, n_tiles):
    i = pl.program_id(0)
    te = ys_ref.shape[0] // TOKEN_TILE_ROWS

    @pl.when(jnp.logical_or(i == 0, te_ref[i] != te_ref[jnp.maximum(i - 1, 0)]))
    def _():
        wgb_ref[...] = wg_ref[0].astype(BF16)
        wub_ref[...] = wu_ref[0].astype(BF16)
        wdb_ref[...] = wd_ref[0].astype(BF16)

    slot = _gather_ring(i, (inv0_ref, inv1_ref, inv2_ref), h2_ref, te, xbuf_ref, sems)

    @pl.when(tv_ref[i] > 0)
    def _():
        x = _load_token_tiles(xbuf_ref.at[slot], 0, te).astype(BF16)
        hg = jnp.dot(x, wgb_ref[...], preferred_element_type=F32)
        hu = jnp.dot(x, wub_ref[...], preferred_element_type=F32)
        _store_token_tiles(ys_ref, _bdot(_silu(hg) * hu, wdb_ref[...]))

    @pl.when(tv_ref[i] == 0)
    def _():
        ys_ref[...] = jnp.zeros_like(ys_ref)

    _gather_drain(i, n_tiles, h2_ref, xbuf_ref, sems)


def _expert_mlp(tile_expert, tile_valid, inv3, h2, wg, wu, wd):
    n_tiles, _, te = inv3.shape
    rows = te * TOKEN_TILE_ROWS
    return pl.pallas_call(
        functools.partial(_expert_kernel, n_tiles=n_tiles),
        out_shape=jax.ShapeDtypeStruct((n_tiles * rows, LANES), F32),
        grid_spec=pltpu.PrefetchScalarGridSpec(
            num_scalar_prefetch=2,
            grid=(n_tiles,),
            in_specs=_ring_index_specs((1, 1, te), n_tiles) + [
                      pl.BlockSpec(memory_space=pl.ANY),
                      pl.BlockSpec((1, D_MODEL, EXPERT_FF), lambda i, e, v: (e[i], 0, 0)),
                      pl.BlockSpec((1, D_MODEL, EXPERT_FF), lambda i, e, v: (e[i], 0, 0)),
                      pl.BlockSpec((1, EXPERT_FF, D_MODEL), lambda i, e, v: (e[i], 0, 0))],
            out_specs=pl.BlockSpec((rows, LANES), lambda i, e, v: (i, 0)),
            scratch_shapes=[pltpu.VMEM((GATHER_BUFFERS, rows, LANES), F32),
                            pltpu.SemaphoreType.DMA((GATHER_BUFFERS,)),
                            pltpu.VMEM((D_MODEL, EXPERT_FF), BF16), pltpu.VMEM((D_MODEL, EXPERT_FF), BF16),
                            pltpu.VMEM((EXPERT_FF, D_MODEL), BF16)]),
        compiler_params=_cparams(("arbitrary",)),
        name="expert_mlp",
    )(tile_expert, tile_valid, inv3, inv3, inv3, h2, wg, wu, wd)


def _combine_kernel(pos0_ref, pos1_ref, pos2_ref, x1_ref, mod_ref, route_ref, postw_ref, ys_ref, out_ref, buf_ref,
                    sems, *, n_tiles):
    tm = x1_ref.shape[0]
    i = pl.program_id(0)
    slot = _gather_ring(i, (pos0_ref, pos1_ref, pos2_ref), ys_ref, 2 * tm, buf_ref, sems)
    route = route_ref[...]
    moe = (route[:, 2:3] * _load_token_tiles(buf_ref.at[slot], 0, tm)
           + route[:, 3:4] * _load_token_tiles(buf_ref.at[slot], tm, tm))
    out_ref[...] = x1_ref[...] + mod_ref[0][5:6, :] * _rms(moe, postw_ref[...])
    _gather_drain(i, n_tiles, ys_ref, buf_ref, sems)


def _combine(pos3, x1, mod3, route, postw, ys, *, bsz, seq):
    tm = TOK_TILE
    nt = seq // tm
    n_tiles = bsz * nt
    row_map = lambda i: (i, 0)
    return pl.pallas_call(
        functools.partial(_combine_kernel, n_tiles=n_tiles),
        out_shape=jax.ShapeDtypeStruct(x1.shape, F32),
        grid=(n_tiles,),
        in_specs=_ring_index_specs((1, 1, 2 * tm), n_tiles) + [
                  pl.BlockSpec((tm, D_MODEL), row_map),
                  pl.BlockSpec((1, N_MOD, D_MODEL), lambda i: (i // nt, 0, 0)),
                  pl.BlockSpec((tm, LANES), row_map),
                  pl.BlockSpec((1, D_MODEL), lambda i: (0, 0)),
                  pl.BlockSpec(memory_space=pl.ANY)],
        out_specs=pl.BlockSpec((tm, D_MODEL), row_map),
        scratch_shapes=[pltpu.VMEM((GATHER_BUFFERS, 2 * tm * TOKEN_TILE_ROWS, LANES), F32),
                        pltpu.SemaphoreType.DMA((GATHER_BUFFERS,))],
        compiler_params=_cparams(("arbitrary",)),
        name="combine",
    )(pos3, pos3, pos3, x1, mod3, route, postw, ys)


def _lane_pad(v, n=LANES):
    v = v.reshape(1, -1)
    return jnp.pad(v, ((0, 0), (0, n - v.shape[1])))


def kernel(x, c, ctx, c_ctx, w_mod, b_mod, pre_norm_w, post_norm_w, w_in, conv_w, a_log, dt_bias, o_norm_w,
           w_a_up, s5_lam_re, s5_lam_im, s5_log_step, s5_b_re, s5_b_im, s5_c_re, s5_c_im, s5_d, w_glu, w_out,
           w_router_group, b_router_group, w_router_expert, b_router_expert, w_gate_e, w_up_e, w_down_e):
    assert w_mod.shape[0] == 1, "single-layer block"
    bsz, seq, _ = x.shape
    ctx_len = ctx.shape[1]
    row = lambda v: v.reshape(1, -1)

    c16 = jnp.concatenate([c, c_ctx[None, :], jnp.zeros((2 * SUBLANES - bsz - 1, D_MODEL), F32)], axis=0)
    mod3 = _modulation(c16, w_mod[0], row(b_mod[0])).reshape(2 * SUBLANES, N_MOD, D_MODEL)

    wi = w_in[0]
    o_z, o_a, o_u, o_g = 3 * DN_WIDTH, 4 * DN_WIDTH, 4 * DN_WIDTH + 4 * DN_HEADS, 4 * DN_WIDTH + 4 * DN_HEADS + S5_WIDTH
    wts_in = (wi[:, :o_z].astype(BF16), wi[:, o_z:o_a].astype(BF16),
              jnp.pad(wi[:, o_a:o_u], ((0, 0), (0, LANES - 4 * DN_HEADS))).astype(BF16),
              wi[:, o_u:o_g].astype(BF16), wi[:, o_g:].astype(BF16))
    convw = jnp.pad(conv_w[0], ((0, SUBLANES - CONV_K), (0, 0)))
    alog = _lane_pad(a_log[0])
    dtb = _lane_pad(dt_bias[0])
    prew0, prew1 = row(pre_norm_w[0, 0]), row(pre_norm_w[0, 1])
    postw0, postw1 = row(post_norm_w[0, 0]), row(post_norm_w[0, 1])

    bar_re, bar_im, bb_re, bb_im = _s5_discretise(s5_lam_re[0], s5_lam_im[0], s5_log_step[0], s5_b_re[0], s5_b_im[0])
    s5w = _s5_weights(bar_re, bar_im, bb_re, bb_im, s5_c_re[0], s5_c_im[0], bsz)

    qkvn_c, gb_c, u_c = _input_projection(ctx.reshape(bsz * ctx_len, D_MODEL), mod3, bsz, 0, prew0, wts_in, convw,
                                          alog, dtb, bsz=bsz, seq=ctx_len, seg_len=ctx_len, full=False)
    s_zero = jnp.zeros((bsz, 2 * DN_HEADS, DN_HEAD_DIM, DN_HEAD_DIM), F32)
    _, _, s_ctx = _delta_rule(qkvn_c, gb_c, s_zero, bsz=bsz, seq=ctx_len)
    x_zero = jnp.zeros((2, bsz, S5_LANES), F32)
    (x_ctx,) = _s5_scan(u_c.reshape(bsz, ctx_len, S5_WIDTH), s5w, x_zero, emit_y=False)

    x2d = x.reshape(bsz * seq, D_MODEL)
    qkvn, gb, u, z, gates = _input_projection(x2d, mod3, 0, 1, prew0, wts_in, convw, alog, dtb,
                                                 bsz=bsz, seq=seq, seg_len=GRID_W, full=True)
    o_f, o_b, _ = _delta_rule(qkvn, gb, s_ctx, bsz=bsz, seq=seq)
    y_f, y_b, _ = _s5_scan(u.reshape(bsz, seq, S5_WIDTH), s5w, x_ctx, emit_y=True)
    rows2d = lambda a: a.reshape(bsz * seq, S5_WIDTH)

    w_router = jnp.pad(jnp.concatenate([w_router_group[0], w_router_expert[0]], axis=1),
                       ((0, 0), (0, LANES - N_EXPERT_GROUPS - N_EXPERTS))).astype(BF16)
    b_router = _lane_pad(jnp.concatenate([b_router_group[0], b_router_expert[0]]))
    wts_merge = (row(o_norm_w[0]), w_a_up[0].astype(BF16), row(s5_d[0]), w_glu[0].astype(BF16),
                 w_out[0].astype(BF16), postw0, prew1, w_router, b_router)
    x1, h2, route = _merge(x2d, mod3, o_f, o_b, z, rows2d(y_f), rows2d(y_b), u, gates, wts_merge, bsz=bsz, seq=seq)

    rank, count = _expert_ranks(route)
    n_tok = bsz * seq
    te = EXPERT_TILE
    counts = count[0, :N_EXPERTS].astype(jnp.int32)
    padded = ((counts + te - 1) // te) * te
    ends = jnp.cumsum(padded)
    starts = ends - padded
    ids = route[:, :2].astype(jnp.int32)
    one_hot = ids[:, :, None] == jnp.arange(N_EXPERTS, dtype=jnp.int32)
    pos = jnp.sum(jnp.where(one_hot, starts, 0), axis=-1) + rank[:, :2].astype(jnp.int32)
    pos3 = jnp.swapaxes(pos.reshape(n_tok // TOK_TILE, TOK_TILE, 2), 1, 2).reshape(n_tok // TOK_TILE, 1, 2 * TOK_TILE)
    n_sorted = 2 * n_tok + N_EXPERTS * te
    tile_start = jnp.arange(n_sorted // te, dtype=jnp.int32) * te
    tile_valid = (tile_start < ends[-1]).astype(jnp.int32)
    tile_expert = jnp.minimum(jnp.sum((ends[None, :] <= tile_start[:, None]).astype(jnp.int32), axis=1), N_EXPERTS - 1)

    inv3 = _row_tokens(pos3, n_sorted).reshape(n_sorted // te, 1, te)
    ys = _expert_mlp(tile_expert, tile_valid, inv3, h2, w_gate_e[0], w_up_e[0], w_down_e[0])
    out = _combine(pos3, x1, mod3, route, postw1, ys, bsz=bsz, seq=seq)
    return out.reshape(bsz, seq, D_MODEL)
```

```python
import functools

import jax
import jax.numpy as jnp
from jax import lax
from jax.experimental import pallas as pl
from jax.experimental.pallas import tpu as pltpu

F32 = jnp.float32
BF16 = jnp.bfloat16
HIGHEST = lax.Precision.HIGHEST

D_MODEL = 1024
GRID_W = 64
DN_HEADS = 4
DN_HEAD_DIM = 128
DN_WIDTH = DN_HEADS * DN_HEAD_DIM
DN_CHUNK = 64
CONV_K = 5
S5_WIDTH = D_MODEL - DN_WIDTH
S5_GROUP = 16
S5_GROUPS = S5_WIDTH // S5_GROUP
S5_STATE = 64
S5_HALF_GROUPS = S5_GROUPS // 2
S5_HALF_STATE = S5_HALF_GROUPS * S5_STATE
S5_LANES = 2 * 2 * S5_HALF_STATE
N_EXPERT_GROUPS = 4
EXPERTS_PER_GROUP = 8
N_EXPERTS = N_EXPERT_GROUPS * EXPERTS_PER_GROUP
EXPERT_FF = 512
NORM_EPS = 1e-6
L2_EPS = 1e-6
N_MOD = 6
LANES = 128
SUBLANES = 8
TOKEN_TILE_ROWS = D_MODEL // LANES
VMEM_LIMIT = 56 * 1024 * 1024

TOK_TILE = 512
PROJ_TILE = 512
S5_TILE = 64
DN_CHUNKS_PER_STEP = 8
EXPERT_TILE = 512
RANK_TILE = 512
GATHER_BUFFERS = 3


def _cparams(sem):
    return pltpu.CompilerParams(dimension_semantics=sem, vmem_limit_bytes=VMEM_LIMIT)


def _bdot(a, b):
    return jnp.dot(a.astype(BF16), b.astype(BF16), preferred_element_type=F32)


def _bdot_nt(a, b):
    return lax.dot_general(a.astype(BF16), b.astype(BF16), (((1,), (1,)), ((), ())),
                           preferred_element_type=F32)


def _bdot_tn(a, b):
    return lax.dot_general(a.astype(BF16), b.astype(BF16), (((0,), (0,)), ((), ())),
                           preferred_element_type=F32)


def _silu(x):
    return x * jax.nn.sigmoid(x)


def _rms(x, w):
    return x * lax.rsqrt(jnp.mean(x * x, axis=-1, keepdims=True) + NORM_EPS) * w


def _store_token_tiles(ref, val):
    n = val.shape[0]
    for j in range(TOKEN_TILE_ROWS):
        ref[pl.ds(j, n, stride=TOKEN_TILE_ROWS), :] = val[:, j * LANES:(j + 1) * LANES]


def _load_token_tiles(ref, first_token, n):
    return jnp.concatenate([ref[pl.ds(first_token * TOKEN_TILE_ROWS + j, n, stride=TOKEN_TILE_ROWS), :]
                            for j in range(TOKEN_TILE_ROWS)], axis=1)


def _mod_kernel(c_ref, w_ref, b_ref, o_ref):
    o_ref[...] = jnp.dot(_silu(c_ref[...]), w_ref[...], precision=HIGHEST,
                         preferred_element_type=F32) + b_ref[...]


def _modulation(c16, w_mod, b_mod):
    n = w_mod.shape[1]
    return pl.pallas_call(
        _mod_kernel,
        out_shape=jax.ShapeDtypeStruct((c16.shape[0], n), F32),
        grid=(n // D_MODEL,),
        in_specs=[pl.BlockSpec(c16.shape, lambda j: (0, 0)),
                  pl.BlockSpec((D_MODEL, D_MODEL), lambda j: (0, j)),
                  pl.BlockSpec((1, D_MODEL), lambda j: (0, j))],
        out_specs=pl.BlockSpec((c16.shape[0], D_MODEL), lambda j: (0, j)),
        compiler_params=_cparams(("arbitrary",)),
        name="modulation",
    )(c16, w_mod, b_mod)


def _inproj_kernel(x_ref, mod_ref, prew_ref, wqkv_ref, wz_ref, wab_ref, wu_ref, wg_ref, convw_ref,
                   alog_ref, dtb_ref, *out_refs, seg_len, full):
    if full:
        qkvn_ref, gb_ref, u_ref, z_ref, gates_ref = out_refs
    else:
        qkvn_ref, gb_ref, u_ref = out_refs
    tm = x_ref.shape[0]
    mod = mod_ref[0]
    h = _rms(x_ref[...], prew_ref[...]) * (1.0 + mod[1:2, :]) + mod[0:1, :]
    hb = h.astype(BF16)

    slab = 2 * DN_HEAD_DIM
    others = [(u_ref, wu_ref, j) for j in range(S5_WIDTH // slab)]
    if full:
        others += [(z_ref, wz_ref, j) for j in range(DN_WIDTH // slab)]
        others += [(gates_ref, wg_ref, j) for j in range(2 * D_MODEL // slab)]
    n_qkv = 3 * DN_WIDTH // slab
    per = -(-len(others) // n_qkv)
    pos = lax.broadcasted_iota(jnp.int32, (tm, 1), 0) % seg_len
    for j in range(n_qkv):
        cols = slice(j * slab, (j + 1) * slab)
        qkv = jnp.dot(hb, wqkv_ref[:, cols], preferred_element_type=F32)
        for o_ref, w_ref, jj in others[j * per:(j + 1) * per]:
            oc = slice(jj * slab, (jj + 1) * slab)
            o_ref[:, oc] = jnp.dot(hb, w_ref[:, oc], preferred_element_type=F32)
        acc = qkv * convw_ref[CONV_K // 2:CONV_K // 2 + 1, cols]
        for k in range(CONV_K):
            s = k - CONV_K // 2
            if s == 0:
                continue
            shifted = pltpu.roll(qkv, (-s) % tm, 0)
            ok = (pos + s >= 0) & (pos + s < seg_len)
            acc = acc + jnp.where(ok, shifted, 0.0) * convw_ref[k:k + 1, cols]
        act = _silu(acc)
        for h in range(2):
            a = act[:, h * DN_HEAD_DIM:(h + 1) * DN_HEAD_DIM]
            if j < 2 * DN_WIDTH // slab:
                a = a * lax.rsqrt(jnp.sum(a * a, axis=-1, keepdims=True) + L2_EPS)
            qkvn_ref[:, j * slab + h * DN_HEAD_DIM:j * slab + (h + 1) * DN_HEAD_DIM] = a

    ab = jnp.dot(hb, wab_ref[...], preferred_element_type=F32)
    xa = ab + dtb_ref[...]
    softplus = jnp.maximum(xa, 0.0) + jnp.log1p(jnp.exp(-jnp.abs(xa)))
    g = -jnp.exp(alog_ref[...]) * softplus
    lane = lax.broadcasted_iota(jnp.int32, ab.shape, 1)
    gb_ref[...] = jnp.where(lane < 2 * DN_HEADS, g, jax.nn.sigmoid(ab))


def _input_projection(x2d, mod3, mod_row0, mod_row_stride, prew, wts, convw, alog, dtb, *, bsz, seq, seg_len, full):
    wqkv, wz, wab, wu, wg = wts
    tm = min(PROJ_TILE, seq)
    nt = seq // tm
    rows = bsz * seq
    row_map = lambda b, i: (b * nt + i, 0)
    const = lambda b, i: (0, 0)
    resident = lambda a: pl.BlockSpec(a.shape, const, pipeline_mode=pl.Buffered(1))
    out_shape = [jax.ShapeDtypeStruct((rows, 3 * DN_WIDTH), F32),
                 jax.ShapeDtypeStruct((rows, LANES), F32),
                 jax.ShapeDtypeStruct((rows, S5_WIDTH), F32)]
    out_specs = [pl.BlockSpec((tm, 3 * DN_WIDTH), row_map),
                 pl.BlockSpec((tm, LANES), row_map),
                 pl.BlockSpec((tm, S5_WIDTH), row_map)]
    if full:
        out_shape += [jax.ShapeDtypeStruct((rows, DN_WIDTH), F32),
                      jax.ShapeDtypeStruct((rows, 2 * D_MODEL), F32)]
        out_specs += [pl.BlockSpec((tm, DN_WIDTH), row_map),
                      pl.BlockSpec((tm, 2 * D_MODEL), row_map)]
    return pl.pallas_call(
        functools.partial(_inproj_kernel, seg_len=seg_len, full=full),
        out_shape=out_shape,
        grid=(bsz, nt),
        in_specs=[pl.BlockSpec((tm, D_MODEL), row_map),
                  pl.BlockSpec((1, N_MOD, D_MODEL), lambda b, i: (mod_row0 + mod_row_stride * b, 0, 0)),
                  pl.BlockSpec((1, D_MODEL), const),
                  resident(wqkv), resident(wz), resident(wab), resident(wu), resident(wg),
                  pl.BlockSpec(convw.shape, const),
                  pl.BlockSpec((1, LANES), const),
                  pl.BlockSpec((1, LANES), const)],
        out_specs=out_specs,
        compiler_params=_cparams(("arbitrary", "arbitrary")),
        name="input_projection_full" if full else "input_projection_ctx",
    )(x2d, mod3, prew, wqkv, wz, wab, wu, wg, convw, alog, dtb)


def _chunk_cumsum(g, reverse):
    n = g.shape[0]
    row = lax.broadcasted_iota(jnp.int32, (n, 1), 0)
    s = 1
    while s < n:
        if reverse:
            g = g + jnp.where(row < n - s, pltpu.roll(g, n - s, 0), 0.0)
        else:
            g = g + jnp.where(row >= s, pltpu.roll(g, s, 0), 0.0)
        s *= 2
    return g


def _delta_kernel(qf_ref, kf_ref, vf_ref, gbf_ref, qb_ref, kb_ref, vb_ref, gbb_ref, s0_ref,
                  of_ref, ob_ref, sout_ref, state_ref, *, n_steps, chunks):
    c = DN_CHUNK
    i = pl.program_id(1)

    @pl.when(i == 0)
    def _():
        state_ref[...] = s0_ref[0]

    row = lax.broadcasted_iota(jnp.int32, (c, c), 0)
    col = lax.broadcasted_iota(jnp.int32, (c, c), 1)
    eye = (row == col).astype(F32)
    scale = DN_HEAD_DIM ** -0.5
    dirs = ((qf_ref, kf_ref, vf_ref, gbf_ref, of_ref), (qb_ref, kb_ref, vb_ref, gbb_ref, ob_ref))
    chains = {}
    for d, (q_ref, k_ref, v_ref, gb_ref, o_ref) in enumerate(dirs):
        incl = (row >= col) if d == 0 else (row <= col)
        strict = (row > col) if d == 0 else (row < col)
        last = c - 1 if d == 0 else 0
        for sub in range(chunks):
            rs = slice(sub * c, (sub + 1) * c)
            gb = gb_ref[rs, :]
            gcum = _chunk_cumsum(gb, reverse=d == 1)
            gcum_t = gcum.T
            for h in range(DN_HEADS):
                lane = d * DN_HEADS + h
                hs = slice(h * DN_HEAD_DIM, (h + 1) * DN_HEAD_DIM)
                g_col = gcum[:, lane:lane + 1]
                g_row = gcum_t[lane:lane + 1, :]
                g_last = gcum[last:last + 1, lane:lane + 1]
                beta = gb[:, 2 * DN_HEADS + lane:2 * DN_HEADS + lane + 1]
                dec = jnp.where(incl, jnp.exp(jnp.where(incl, g_col - g_row, 0.0)), 0.0)
                q = q_ref[rs, hs] * scale
                k = k_ref[rs, hs]
                v = v_ref[rs, hs]
                k_beta = k * beta
                e_g = jnp.exp(g_col)
                chains[d, sub, h] = dict(
                    lane=lane, rs=rs, hs=hs, o_ref=o_ref, strict=strict, dec=dec, q=q, k=k, k_beta=k_beta, e_g=e_g,
                    rhs=jnp.concatenate([v * beta, k_beta * e_g], axis=1),
                    k_tail=k * jnp.exp(g_last - g_col), e_last=jnp.exp(g_last))
    every = list(chains.values())
    for ch in every:
        kq = _bdot_nt(jnp.concatenate([ch['k_beta'], ch['q']], axis=0), ch['k'])
        ch['npow'] = -jnp.where(ch['strict'], kq[:c] * ch['dec'], 0.0)
        ch['attn'] = kq[c:] * ch['dec']
        ch['t_inv'] = eye + ch['npow']
    for _ in range(5):
        for ch in every:
            ch['npow'] = _bdot(ch['npow'], ch['npow'])
        for ch in every:
            ch['t_inv'] = ch['t_inv'] + _bdot(ch['t_inv'], ch['npow'])
    for ch in every:
        ch['sol'] = _bdot(ch['t_inv'], ch['rhs'])
        ch['q_dec'] = ch['q'] * ch['e_g']

    states = [state_ref[lane] for lane in range(2 * DN_HEADS)]
    for step in range(chunks):
        group = [chains[d, step if d == 0 else chunks - 1 - step, h] for d in range(2) for h in range(DN_HEADS)]
        for ch in group:
            ch['wq'] = _bdot(jnp.concatenate([ch['sol'][:, DN_HEAD_DIM:], ch['q_dec']], axis=0), states[ch['lane']])
        for ch in group:
            ch['v_new'] = ch['sol'][:, :DN_HEAD_DIM] - ch['wq'][:c]
            ch['o_ref'][ch['rs'], ch['hs']] = ch['wq'][c:] + _bdot(ch['attn'], ch['v_new'])
        for ch in group:
            states[ch['lane']] = states[ch['lane']] * ch['e_last'] + _bdot_tn(ch['k_tail'], ch['v_new'])
    for lane in range(2 * DN_HEADS):
        state_ref[lane] = states[lane]

    @pl.when(i == n_steps - 1)
    def _():
        sout_ref[0] = state_ref[...]


def _delta_rule(qkvn, gb, s0, *, bsz, seq):
    chunks = min(DN_CHUNKS_PER_STEP, seq // DN_CHUNK)
    rows = DN_CHUNK * chunks
    nb = seq // rows
    fwd = lambda j: (lambda b, i: (b * nb + i, j))
    bwd = lambda j: (lambda b, i: (b * nb + nb - 1 - i, j))
    blk = lambda m: pl.BlockSpec((rows, DN_WIDTH), m)
    gblk = lambda m: pl.BlockSpec((rows, LANES), m)
    st = pl.BlockSpec((1, 2 * DN_HEADS, DN_HEAD_DIM, DN_HEAD_DIM), lambda b, i: (b, 0, 0, 0))
    return pl.pallas_call(
        functools.partial(_delta_kernel, n_steps=nb, chunks=chunks),
        out_shape=[jax.ShapeDtypeStruct((bsz * seq, DN_WIDTH), F32),
                   jax.ShapeDtypeStruct((bsz * seq, DN_WIDTH), F32),
                   jax.ShapeDtypeStruct(s0.shape, F32)],
        grid=(bsz, nb),
        in_specs=[blk(fwd(0)), blk(fwd(1)), blk(fwd(2)), gblk(fwd(0)),
                  blk(bwd(0)), blk(bwd(1)), blk(bwd(2)), gblk(bwd(0)), st],
        out_specs=[blk(fwd(0)), blk(bwd(0)), st],
        scratch_shapes=[pltpu.VMEM((2 * DN_HEADS, DN_HEAD_DIM, DN_HEAD_DIM), F32)],
        compiler_params=_cparams(("arbitrary", "arbitrary")),
        name="delta_rule",
    )(qkvn, qkvn, qkvn, gb, qkvn, qkvn, qkvn, gb, s0)


def _s5_disc_kernel(lre_ref, lim_ref, ls_ref, bre_ref, bim_ref, are_ref, aim_ref, bbre_ref, bbim_ref):
    lam_re = lre_ref[...]
    lam_im = lim_ref[...]
    dt = jnp.exp(ls_ref[...])
    mag = jnp.exp(lam_re * dt)
    ang = lam_im * dt
    bar_re = mag * jnp.cos(ang)
    bar_im = mag * jnp.sin(ang)
    den = lam_re * lam_re + lam_im * lam_im
    coef_re = ((bar_re - 1.0) * lam_re + bar_im * lam_im) / den
    coef_im = (bar_im * lam_re - (bar_re - 1.0) * lam_im) / den
    are_ref[...] = bar_re
    aim_ref[...] = bar_im
    bbre_ref[...] = coef_re * bre_ref[...] - coef_im * bim_ref[...]
    bbim_ref[...] = coef_re * bim_ref[...] + coef_im * bre_ref[...]


def _s5_discretise(lam_re, lam_im, log_step, b_re, b_im):
    rep = lambda t: jnp.repeat(t.reshape(2 * S5_GROUPS, -1), S5_GROUP, axis=0)
    rows = 2 * S5_GROUPS * S5_GROUP
    tr = lambda t: jnp.swapaxes(t, -1, -2).reshape(rows, S5_STATE)
    outs = pl.pallas_call(
        _s5_disc_kernel,
        out_shape=[jax.ShapeDtypeStruct((rows, S5_STATE), F32)] * 4,
        name="s5_discretise",
    )(rep(lam_re), rep(lam_im), rep(log_step[..., None]), tr(b_re), tr(b_im))
    shp = (2, S5_GROUPS, S5_GROUP, S5_STATE)
    bar_re, bar_im, bb_re, bb_im = (o.reshape(shp) for o in outs)
    return bar_re[:, :, 0], bar_im[:, :, 0], bb_re, bb_im


def _s5_kernel(uf_ref, ub_ref, wb_ref, wcre_ref, wcim_ref, are_ref, aim_ref, x0_ref, *rest, tt, n_tiles, emit_y):
    if emit_y:
        yf_ref, yb_ref, xout_ref, buf_ref, xst_ref, us_ref, ys_ref = rest
    else:
        xout_ref, buf_ref, xst_ref, us_ref = rest
        yf_ref = yb_ref = ys_ref = None
    i = pl.program_id(0)
    bsz = uf_ref.shape[0]
    hs = S5_HALF_STATE
    n_lane_tiles = S5_WIDTH // LANES
    blk = 512
    u_refs = (uf_ref, ub_ref)
    y_refs = (yf_ref, yb_ref)

    @pl.when(i == 0)
    def _():
        xst_ref[...] = x0_ref[...]

    def input_matmul(hh):
        for d in range(2):
            u_h = jnp.concatenate([us_ref[d, 2 * hh], us_ref[d, 2 * hh + 1]], axis=1)
            buf_ref[d, :, hh * 2 * hs:(hh + 1) * 2 * hs] = _bdot(u_h, wb_ref[d, hh])

    def scan(hh):
        for p in range(hs // blk):
            lr = hh * 2 * hs + p * blk
            li = lr + hs
            a = [(are_ref[d, :, lr:lr + blk], aim_ref[d, :, lr:lr + blk]) for d in range(2)]
            x = [(xst_ref[d, :, lr:lr + blk], xst_ref[d, :, li:li + blk]) for d in range(2)]
            for t in range(tt):
                for d in range(2):
                    r0 = (t if d == 0 else tt - 1 - t) * SUBLANES
                    a_re, a_im = a[d]
                    x_re, x_im = x[d]
                    n_re = a_re * x_re - a_im * x_im + buf_ref[d, r0:r0 + SUBLANES, lr:lr + blk]
                    n_im = a_re * x_im + a_im * x_re + buf_ref[d, r0:r0 + SUBLANES, li:li + blk]
                    buf_ref[d, r0:r0 + SUBLANES, lr:lr + blk] = n_re
                    buf_ref[d, r0:r0 + SUBLANES, li:li + blk] = n_im
                    x[d] = (n_re, n_im)
            for d in range(2):
                xst_ref[d, :, lr:lr + blk] = x[d][0]
                xst_ref[d, :, li:li + blk] = x[d][1]

    def output_matmul(hh):
        for d in range(2):
            x_re = buf_ref[d, :, hh * 2 * hs:hh * 2 * hs + hs]
            x_im = buf_ref[d, :, hh * 2 * hs + hs:(hh + 1) * 2 * hs]
            y_h = _bdot(x_re, wcre_ref[d, hh]) - _bdot(x_im, wcim_ref[d, hh])
            ys_ref[d, 2 * hh] = y_h[:, :LANES]
            ys_ref[d, 2 * hh + 1] = y_h[:, LANES:]

    for d in range(2):
        for b in range(bsz):
            for j in range(n_lane_tiles):
                us_ref[d, j, pl.ds(b, tt, stride=bsz), :] = u_refs[d][b, :, j * LANES:(j + 1) * LANES]
    input_matmul(0)
    input_matmul(1)
    scan(0)
    if emit_y:
        output_matmul(0)
    scan(1)
    if emit_y:
        output_matmul(1)
        for d in range(2):
            for b in range(bsz):
                for j in range(n_lane_tiles):
                    y_refs[d][b, :, j * LANES:(j + 1) * LANES] = ys_ref[d, j, pl.ds(b, tt, stride=bsz), :]

    @pl.when(i == n_tiles - 1)
    def _():
        xout_ref[...] = xst_ref[...]


def _s5_scan(u3, s5w, x0, *, emit_y):
    wb, wc_re, wc_im, a_re, a_im = s5w
    bsz, seq, _ = u3.shape
    assert bsz == SUBLANES, "one time step of all batches must fill one sublane group"
    tt = S5_TILE
    nt = seq // tt
    rows = tt * bsz
    full = lambda a: pl.BlockSpec(a.shape, lambda i: (0,) * a.ndim)
    ublk = lambda m: pl.BlockSpec((bsz, tt, S5_WIDTH), m)
    fwd = lambda i: (0, i, 0)
    bwd = lambda i: (0, nt - 1 - i, 0)
    out_shape = [jax.ShapeDtypeStruct(x0.shape, F32)]
    out_specs = [full(x0)]
    scratch = [pltpu.VMEM((2, rows, S5_LANES), F32), pltpu.VMEM(x0.shape, F32),
               pltpu.VMEM((2, S5_WIDTH // LANES, rows, LANES), F32)]
    if emit_y:
        out_shape = [jax.ShapeDtypeStruct(u3.shape, F32)] * 2 + out_shape
        out_specs = [ublk(fwd), ublk(bwd)] + out_specs
        scratch.append(pltpu.VMEM((2, S5_WIDTH // LANES, rows, LANES), F32))
    return pl.pallas_call(
        functools.partial(_s5_kernel, tt=tt, n_tiles=nt, emit_y=emit_y),
        out_shape=out_shape,
        grid=(nt,),
        in_specs=[ublk(fwd), ublk(bwd), full(wb), full(wc_re), full(wc_im), full(a_re), full(a_im), full(x0)],
        out_specs=out_specs,
        scratch_shapes=scratch,
        compiler_params=_cparams(("arbitrary",)),
        name="s5_scan_y" if emit_y else "s5_scan_state",
    )(u3, u3, wb, wc_re, wc_im, a_re, a_im, x0)


def _s5_weights(bar_re, bar_im, bb_re, bb_im, c_re, c_im, bsz):
    hg = S5_HALF_GROUPS
    eye = jnp.eye(hg, dtype=F32)

    def in_mat(bb):
        t = bb.reshape(2, 2, hg, S5_GROUP, S5_STATE)
        t = jnp.einsum('dhicp,ij->dhicjp', t, eye)
        return t.reshape(2, 2, hg * S5_GROUP, hg * S5_STATE)

    def out_mat(cc):
        t = cc.reshape(2, 2, hg, S5_GROUP, S5_STATE)
        t = jnp.einsum('dhicp,ij->dhipjc', t, eye)
        return t.reshape(2, 2, hg * S5_STATE, hg * S5_GROUP)

    wb = jnp.concatenate([in_mat(bb_re), in_mat(bb_im)], axis=-1).astype(BF16)
    wc_re = out_mat(c_re).astype(BF16)
    wc_im = out_mat(c_im).astype(BF16)

    def lanes(a):
        t = a.reshape(2, 2, 1, S5_HALF_STATE)
        t = jnp.broadcast_to(t, (2, 2, 2, S5_HALF_STATE)).reshape(2, 1, S5_LANES)
        return jnp.broadcast_to(t, (2, bsz, S5_LANES))

    return wb, wc_re, wc_im, lanes(bar_re), lanes(bar_im)


def _merge_kernel(x_ref, mod_ref, of_ref, ob_ref, z_ref, yf_ref, yb_ref, u_ref, gates_ref, onw_ref, waup_ref,
                  s5d_ref, wglu_ref, wout_ref, postw_ref, prew_ref, wr_ref, br_ref,
                  x1_ref, h2_ref, route_ref):
    mod = mod_ref[0]
    o_a = of_ref[...] + ob_ref[...]
    z = z_ref[...]
    parts = []
    for h in range(DN_HEADS):
        hs = slice(h * DN_HEAD_DIM, (h + 1) * DN_HEAD_DIM)
        parts.append(_rms(o_a[:, hs], onw_ref[...]) * _silu(z[:, hs]))
    y_a = _bdot(jnp.concatenate(parts, axis=1), waup_ref[...])
    y5 = yf_ref[...] + yb_ref[...] + s5d_ref[...] * u_ref[...]
    glu = _bdot(jax.nn.gelu(y5, approximate=True), wglu_ref[...])
    y_b = glu[:, :D_MODEL] * jax.nn.sigmoid(glu[:, D_MODEL:])
    gates = gates_ref[...]
    y = jax.nn.sigmoid(gates[:, :D_MODEL]) * y_a + jax.nn.sigmoid(gates[:, D_MODEL:]) * y_b
    mixed = _bdot(y, wout_ref[...])
    x1 = x_ref[...] + mod[2:3, :] * _rms(mixed, postw_ref[...])
    x1_ref[...] = x1
    h2 = _rms(x1, prew_ref[...]) * (1.0 + mod[4:5, :]) + mod[3:4, :]
    _store_token_tiles(h2_ref, h2)

    logits = _bdot(h2, wr_ref[...]) + br_ref[...]
    lane = lax.broadcasted_iota(jnp.int32, logits.shape, 1)
    neg = -jnp.inf
    big = jnp.int32(LANES)
    first = lambda hit: jnp.min(jnp.where(hit, lane, big), axis=-1, keepdims=True)
    g_logit = jnp.where(lane < N_EXPERT_GROUPS, logits, neg)
    g_max = jnp.max(g_logit, axis=-1, keepdims=True)
    g_sel = first(g_logit == g_max)
    g_w = 1.0 / jnp.sum(jnp.exp(g_logit - g_max), axis=-1, keepdims=True)
    e_idx = lane - N_EXPERT_GROUPS
    in_group = (e_idx >= 0) & (e_idx < N_EXPERTS) & ((e_idx // EXPERTS_PER_GROUP) == g_sel)
    e_logit = jnp.where(in_group, logits, neg)
    m1 = jnp.max(e_logit, axis=-1, keepdims=True)
    i1 = first(e_logit == m1)
    e_logit2 = jnp.where(lane == i1, neg, e_logit)
    m2 = jnp.max(e_logit2, axis=-1, keepdims=True)
    i2 = first(e_logit2 == m2)
    e2 = jnp.exp(m2 - m1)
    w1 = g_w / (1.0 + e2)
    w2 = g_w * e2 / (1.0 + e2)
    id1 = (i1 - N_EXPERT_GROUPS).astype(F32)
    id2 = (i2 - N_EXPERT_GROUPS).astype(F32)
    route_ref[...] = jnp.where(lane == 0, id1, jnp.where(lane == 1, id2, jnp.where(lane == 2, w1,
                               jnp.where(lane == 3, w2, 0.0))))


def _merge(x2d, mod3, o_f, o_b, z, y_f, y_b, u, gates, wts, *, bsz, seq):
    onw, waup, s5d, wglu, wout, postw, prew, wr, br = wts
    tm = PROJ_TILE
    nt = seq // tm
    rows = bsz * seq
    row_map = lambda b, i: (b * nt + i, 0)
    const = lambda b, i: (0, 0)
    cs = lambda a: pl.BlockSpec(a.shape, const, pipeline_mode=pl.Buffered(1))
    return pl.pallas_call(
        _merge_kernel,
        out_shape=[jax.ShapeDtypeStruct((rows, D_MODEL), F32),
                   jax.ShapeDtypeStruct((rows * TOKEN_TILE_ROWS, LANES), F32),
                   jax.ShapeDtypeStruct((rows, LANES), F32)],
        grid=(bsz, nt),
        in_specs=[pl.BlockSpec((tm, D_MODEL), row_map),
                  pl.BlockSpec((1, N_MOD, D_MODEL), lambda b, i: (b, 0, 0)),
                  pl.BlockSpec((tm, DN_WIDTH), row_map),
                  pl.BlockSpec((tm, DN_WIDTH), row_map),
                  pl.BlockSpec((tm, DN_WIDTH), row_map),
                  pl.BlockSpec((tm, S5_WIDTH), row_map),
                  pl.BlockSpec((tm, S5_WIDTH), row_map),
                  pl.BlockSpec((tm, S5_WIDTH), row_map),
                  pl.BlockSpec((tm, 2 * D_MODEL), row_map),
                  cs(onw), cs(waup), cs(s5d), cs(wglu), cs(wout), cs(postw), cs(prew), cs(wr), cs(br)],
        out_specs=[pl.BlockSpec((tm, D_MODEL), row_map),
                   pl.BlockSpec((tm * TOKEN_TILE_ROWS, LANES), row_map),
                   pl.BlockSpec((tm, LANES), row_map)],
        compiler_params=_cparams(("arbitrary", "arbitrary")),
        name="merge_router",
    )(x2d, mod3, o_f, o_b, z, y_f, y_b, u, gates, onw, waup, s5d, wglu, wout, postw, prew, wr, br)


def _rank_kernel(route_ref, rank_ref, count_ref, carry_ref):
    i = pl.program_id(0)

    @pl.when(i == 0)
    def _():
        carry_ref[...] = jnp.zeros_like(carry_ref)

    route = route_ref[...]
    r = route.shape[0]
    lane = lax.broadcasted_iota(jnp.int32, route.shape, 1).astype(F32)
    oh0 = (lane == route[:, 0:1]).astype(F32)
    oh1 = (lane == route[:, 1:2]).astype(F32)
    both = oh0 + oh1
    row = lax.broadcasted_iota(jnp.int32, (r, r), 0)
    col = lax.broadcasted_iota(jnp.int32, (r, r), 1)
    before = _bdot((row > col).astype(F32), both) + carry_ref[...]
    rank0 = jnp.sum(oh0 * before, axis=-1, keepdims=True)
    rank1 = jnp.sum(oh1 * before, axis=-1, keepdims=True)
    lane_i = lax.broadcasted_iota(jnp.int32, route.shape, 1)
    rank_ref[...] = jnp.where(lane_i == 0, rank0, jnp.where(lane_i == 1, rank1, 0.0))
    carry_ref[...] = carry_ref[...] + jnp.sum(both, axis=0, keepdims=True)
    count_ref[...] = carry_ref[...]


def _expert_ranks(route):
    rows = route.shape[0]
    r = RANK_TILE
    return pl.pallas_call(
        _rank_kernel,
        out_shape=[jax.ShapeDtypeStruct((rows, LANES), F32), jax.ShapeDtypeStruct((1, LANES), F32)],
        grid=(rows // r,),
        in_specs=[pl.BlockSpec((r, LANES), lambda i: (i, 0))],
        out_specs=[pl.BlockSpec((r, LANES), lambda i: (i, 0)), pl.BlockSpec((1, LANES), lambda i: (0, 0))],
        scratch_shapes=[pltpu.VMEM((1, LANES), F32)],
        compiler_params=_cparams(("arbitrary",)),
        name="expert_ranks",
    )(route)


def _token_copy(src_ref, src_token, dst_ref, dst_token, sem):
    r = TOKEN_TILE_ROWS
    return pltpu.make_async_copy(src_ref.at[pl.ds(pl.multiple_of(src_token * r, r), r), :],
                                 dst_ref.at[pl.ds(dst_token * r, r), :], sem)


def _row_token_kernel(pos_ref, fill_ref, inv_ref, sem):
    i = pl.program_id(0)
    tm = pos_ref.shape[-1] // 2
    base = i * tm

    @pl.when(i == 0)
    def _():
        fill = pltpu.make_async_copy(fill_ref, inv_ref, sem)
        fill.start()
        fill.wait()

    def body(t, _):
        for s in range(2):
            p = pos_ref[0, 0, s * tm + t]
            inv_ref[p] = base + t
        return 0

    lax.fori_loop(0, tm, body, 0, unroll=16)


def _row_tokens(pos3, n_rows):
    n_tiles, _, two_tm = pos3.shape
    n_tok = n_tiles * two_tm // 2
    fill = jnp.arange(n_rows, dtype=jnp.int32) % n_tok
    return pl.pallas_call(
        _row_token_kernel,
        out_shape=jax.ShapeDtypeStruct((n_rows,), jnp.int32),
        grid=(n_tiles,),
        in_specs=[pl.BlockSpec((1, 1, two_tm), lambda i: (i, 0, 0), memory_space=pltpu.SMEM),
                  pl.BlockSpec(memory_space=pl.ANY)],
        out_specs=pl.BlockSpec(memory_space=pltpu.SMEM),
        scratch_shapes=[pltpu.SemaphoreType.DMA],
        compiler_params=_cparams(("arbitrary",)),
        name="row_tokens",
    )(pos3, fill)


def _gather_rows(src_ref, idx_ref, n, dst_ref, sem):
    for t in range(n):
        _token_copy(src_ref, idx_ref[0, 0, t], dst_ref, t, sem).start(priority=t % 2)


def _wait_rows(src_ref, dst_ref, sem):
    pltpu.make_async_copy(src_ref.at[pl.ds(0, dst_ref.shape[0]), :], dst_ref, sem).wait()


def _gather_ring(i, idx_refs, src_ref, n, buf_ref, sems):
    nb = GATHER_BUFFERS

    @pl.when(i == 0)
    def _():
        for k in range(nb - 1):
            _gather_rows(src_ref, idx_refs[k], n, buf_ref.at[k], sems.at[k])

    slot = i % nb
    _wait_rows(src_ref, buf_ref.at[slot], sems.at[slot])
    ahead = (i + nb - 1) % nb
    _gather_rows(src_ref, idx_refs[nb - 1], n, buf_ref.at[ahead], sems.at[ahead])
    return slot


def _gather_drain(i, n_tiles, src_ref, buf_ref, sems):
    @pl.when(i == n_tiles - 1)
    def _():
        for k in range(1, GATHER_BUFFERS):
            s = (i + k) % GATHER_BUFFERS
            _wait_rows(src_ref, buf_ref.at[s], sems.at[s])


def _ring_index_specs(block, n_tiles):
    def spec(k):
        return pl.BlockSpec(block, lambda i, *_: (jnp.minimum(i + k, n_tiles - 1), 0, 0), memory_space=pltpu.SMEM)

    return [spec(k) for k in range(GATHER_BUFFERS)]


def _expert_kernel(te_ref, tv_ref, inv0_ref, inv1_ref, inv2_ref, h2_ref, wg_ref, wu_ref, wd_ref, ys_ref,
                   xbuf_ref, sems, wgb_ref, wub_ref, wdb_ref, *, n_tiles):
    i = pl.program_id(0)
    te = ys_ref.shape[0] // TOKEN_TILE_ROWS

    @pl.when(jnp.logical_or(i == 0, te_ref[i] != te_ref[jnp.maximum(i - 1, 0)]))
    def _():
        wgb_ref[...] = wg_ref[0].astype(BF16)
        wub_ref[...] = wu_ref[0].astype(BF16)
        wdb_ref[...] = wd_ref[0].astype(BF16)

    slot = _gather_ring(i, (inv0_ref, inv1_ref, inv2_ref), h2_ref, te, xbuf_ref, sems)

    @pl.when(tv_ref[i] > 0)
    def _():
        x = _load_token_tiles(xbuf_ref.at[slot], 0, te).astype(BF16)
        hg = jnp.dot(x, wgb_ref[...], preferred_element_type=F32)
        hu = jnp.dot(x, wub_ref[...], preferred_element_type=F32)
        _store_token_tiles(ys_ref, _bdot(_silu(hg) * hu, wdb_ref[...]))

    @pl.when(tv_ref[i] == 0)
    def _():
        ys_ref[...] = jnp.zeros_like(ys_ref)

    _gather_drain(i, n_tiles, h2_ref, xbuf_ref, sems)


def _expert_mlp(tile_expert, tile_valid, inv3, h2, wg, wu, wd):
    n_tiles, _, te = inv3.shape
    rows = te * TOKEN_TILE_ROWS
    return pl.pallas_call(
        functools.partial(_expert_kernel, n_tiles=n_tiles),
        out_shape=jax.ShapeDtypeStruct((n_tiles * rows, LANES), F32),
        grid_spec=pltpu.PrefetchScalarGridSpec(
            num_scalar_prefetch=2,
            grid=(n_tiles,),
            in_specs=_ring_index_specs((1, 1, te), n_tiles) + [
                      pl.BlockSpec(memory_space=pl.ANY),
                      pl.BlockSpec((1, D_MODEL, EXPERT_FF), lambda i, e, v: (e[i], 0, 0)),
                      pl.BlockSpec((1, D_MODEL, EXPERT_FF), lambda i, e, v: (e[i], 0, 0)),
                      pl.BlockSpec((1, EXPERT_FF, D_MODEL), lambda i, e, v: (e[i], 0, 0))],
            out_specs=pl.BlockSpec((rows, LANES), lambda i, e, v: (i, 0)),
            scratch_shapes=[pltpu.VMEM((GATHER_BUFFERS, rows, LANES), F32),
                            pltpu.SemaphoreType.DMA((GATHER_BUFFERS,)),
                            pltpu.VMEM((D_MODEL, EXPERT_FF), BF16), pltpu.VMEM((D_MODEL, EXPERT_FF), BF16),
                            pltpu.VMEM((EXPERT_FF, D_MODEL), BF16)]),
        compiler_params=_cparams(("arbitrary",)),
        name="expert_mlp",
    )(tile_expert, tile_valid, inv3, inv3, inv3, h2, wg, wu, wd)


def _combine_kernel(pos0_ref, pos1_ref, pos2_ref, x1_ref, mod_ref, route_ref, postw_ref, ys_ref, out_ref, buf_ref,
                    sems, *, n_tiles):
    tm = x1_ref.shape[0]
    i = pl.program_id(0)
    slot = _gather_ring(i, (pos0_ref, pos1_ref, pos2_ref), ys_ref, 2 * tm, buf_ref, sems)
    route = route_ref[...]
    moe = (route[:, 2:3] * _load_token_tiles(buf_ref.at[slot], 0, tm)
           + route[:, 3:4] * _load_token_tiles(buf_ref.at[slot], tm, tm))
    out_ref[...] = x1_ref[...] + mod_ref[0][5:6, :] * _rms(moe, postw_ref[...])
    _gather_drain(i, n_tiles, ys_ref, buf_ref, sems)


def _combine(pos3, x1, mod3, route, postw, ys, *, bsz, seq):
    tm = TOK_TILE
    nt = seq // tm
    n_tiles = bsz * nt
    row_map = lambda i: (i, 0)
    return pl.pallas_call(
        functools.partial(_combine_kernel, n_tiles=n_tiles),
        out_shape=jax.ShapeDtypeStruct(x1.shape, F32),
        grid=(n_tiles,),
        in_specs=_ring_index_specs((1, 1, 2 * tm), n_tiles) + [
                  pl.BlockSpec((tm, D_MODEL), row_map),
                  pl.BlockSpec((1, N_MOD, D_MODEL), lambda i: (i // nt, 0, 0)),
                  pl.BlockSpec((tm, LANES), row_map),
                  pl.BlockSpec((1, D_MODEL), lambda i: (0, 0)),
                  pl.BlockSpec(memory_space=pl.ANY)],
        out_specs=pl.BlockSpec((tm, D_MODEL), row_map),
        scratch_shapes=[pltpu.VMEM((GATHER_BUFFERS, 2 * tm * TOKEN_TILE_ROWS, LANES), F32),
                        pltpu.SemaphoreType.DMA((GATHER_BUFFERS,))],
        compiler_params=_cparams(("arbitrary",)),
        name="combine",
    )(pos3, pos3, pos3, x1, mod3, route, postw, ys)


def _lane_pad(v, n=LANES):
    v = v.reshape(1, -1)
    return jnp.pad(v, ((0, 0), (0, n - v.shape[1])))


def kernel(x, c, ctx, c_ctx, w_mod, b_mod, pre_norm_w, post_norm_w, w_in, conv_w, a_log, dt_bias, o_norm_w,
           w_a_up, s5_lam_re, s5_lam_im, s5_log_step, s5_b_re, s5_b_im, s5_c_re, s5_c_im, s5_d, w_glu, w_out,
           w_router_group, b_router_group, w_router_expert, b_router_expert, w_gate_e, w_up_e, w_down_e):
    assert w_mod.shape[0] == 1, "single-layer block"
    bsz, seq, _ = x.shape
    ctx_len = ctx.shape[1]
    row = lambda v: v.reshape(1, -1)

    c16 = jnp.concatenate([c, c_ctx[None, :], jnp.zeros((2 * SUBLANES - bsz - 1, D_MODEL), F32)], axis=0)
    mod3 = _modulation(c16, w_mod[0], row(b_mod[0])).reshape(2 * SUBLANES, N_MOD, D_MODEL)

    wi = w_in[0]
    o_z, o_a, o_u, o_g = 3 * DN_WIDTH, 4 * DN_WIDTH, 4 * DN_WIDTH + 4 * DN_HEADS, 4 * DN_WIDTH + 4 * DN_HEADS + S5_WIDTH
    wts_in = (wi[:, :o_z].astype(BF16), wi[:, o_z:o_a].astype(BF16),
              jnp.pad(wi[:, o_a:o_u], ((0, 0), (0, LANES - 4 * DN_HEADS))).astype(BF16),
              wi[:, o_u:o_g].astype(BF16), wi[:, o_g:].astype(BF16))
    convw = jnp.pad(conv_w[0], ((0, SUBLANES - CONV_K), (0, 0)))
    alog = _lane_pad(a_log[0])
    dtb = _lane_pad(dt_bias[0])
    prew0, prew1 = row(pre_norm_w[0, 0]), row(pre_norm_w[0, 1])
    postw0, postw1 = row(post_norm_w[0, 0]), row(post_norm_w[0, 1])

    bar_re, bar_im, bb_re, bb_im = _s5_discretise(s5_lam_re[0], s5_lam_im[0], s5_log_step[0], s5_b_re[0], s5_b_im[0])
    s5w = _s5_weights(bar_re, bar_im, bb_re, bb_im, s5_c_re[0], s5_c_im[0], bsz)

    qkvn_c, gb_c, u_c = _input_projection(ctx.reshape(bsz * ctx_len, D_MODEL), mod3, bsz, 0, prew0, wts_in, convw,
                                          alog, dtb, bsz=bsz, seq=ctx_len, seg_len=ctx_len, full=False)
    s_zero = jnp.zeros((bsz, 2 * DN_HEADS, DN_HEAD_DIM, DN_HEAD_DIM), F32)
    _, _, s_ctx = _delta_rule(qkvn_c, gb_c, s_zero, bsz=bsz, seq=ctx_len)
    x_zero = jnp.zeros((2, bsz, S5_LANES), F32)
    (x_ctx,) = _s5_scan(u_c.reshape(bsz, ctx_len, S5_WIDTH), s5w, x_zero, emit_y=False)

    x2d = x.reshape(bsz * seq, D_MODEL)
    qkvn, gb, u, z, gates = _input_projection(x2d, mod3, 0, 1, prew0, wts_in, convw, alog, dtb,
                                                 bsz=bsz, seq=seq, seg_len=GRID_W, full=True)
    o_f, o_b, _ = _delta_rule(qkvn, gb, s_ctx, bsz=bsz, seq=seq)
    y_f, y_b, _ = _s5_scan(u.reshape(bsz, seq, S5_WIDTH), s5w, x_ctx, emit_y=True)
    rows2d = lambda a: a.reshape(bsz * seq, S5_WIDTH)

    w_router = jnp.pad(jnp.concatenate([w_router_group[0], w_router_expert[0]], axis=1),
                       ((0, 0), (0, LANES - N_EXPERT_GROUPS - N_EXPERTS))).astype(BF16)
    b_router = _lane_pad(jnp.concatenate([b_router_group[0], b_router_expert[0]]))
    wts_merge = (row(o_norm_w[0]), w_a_up[0].astype(BF16), row(s5_d[0]), w_glu[0].astype(BF16),
                 w_out[0].astype(BF16), postw0, prew1, w_router, b_router)
    x1, h2, route = _merge(x2d, mod3, o_f, o_b, z, rows2d(y_f), rows2d(y_b), u, gates, wts_merge, bsz=bsz, seq=seq)

    rank, count = _expert_ranks(route)
    n_tok = bsz * seq
    te = EXPERT_TILE
    counts = count[0, :N_EXPERTS].astype(jnp.int32)
    padded = ((counts + te - 1) // te) * te
    ends = jnp.cumsum(padded)
    starts = ends - padded
    ids = route[:, :2].astype(jnp.int32)
    one_hot = ids[:, :, None] == jnp.arange(N_EXPERTS, dtype=jnp.int32)
    pos = jnp.sum(jnp.where(one_hot, starts, 0), axis=-1) + rank[:, :2].astype(jnp.int32)
    pos3 = jnp.swapaxes(pos.reshape(n_tok // TOK_TILE, TOK_TILE, 2), 1, 2).reshape(n_tok // TOK_TILE, 1, 2 * TOK_TILE)
    n_sorted = 2 * n_tok + N_EXPERTS * te
    tile_start = jnp.arange(n_sorted // te, dtype=jnp.int32) * te
    tile_valid = (tile_start < ends[-1]).astype(jnp.int32)
    tile_expert = jnp.minimum(jnp.sum((ends[None, :] <= tile_start[:, None]).astype(jnp.int32), axis=1), N_EXPERTS - 1)

    inv3 = _row_tokens(pos3, n_sorted).reshape(n_sorted // te, 1, te)
    ys = _expert_mlp(tile_expert, tile_valid, inv3, h2, w_gate_e[0], w_up_e[0], w_down_e[0])
    out = _combine(pos3, x1, mod3, route, postw1, ys, bsz=bsz, seq=seq)
    return out.reshape(bsz, seq, D_MODEL)
```
